```python
import jax, jax.numpy as jnp
from jax import lax
import numpy as np

D_MODEL = 1024
BATCH = 8
SEQ = 4096
DEPTH = 1

N_MEM = 256
MOBA_HEADS = 8
MOBA_HD = 64
MOBA_W = MOBA_HEADS * MOBA_HD
MOBA_BLOCK = 256
MOBA_TOPK = 3
MOBA_QCHUNK = 32
HGRN_HEADS = 4
HGRN_DK = 128
HGRN_DV = 128
HGRN_FW = HGRN_HEADS * HGRN_DK
HGRN_W = HGRN_HEADS * HGRN_DV
HGRN_CHUNK = 32
XA_HEADS = 4
XA_HD = 128
XA_W = XA_HEADS * XA_HD
N_BRANCH = 3
PROJ_WIDTHS = (MOBA_W, MOBA_W, MOBA_W, MOBA_W,
               HGRN_FW, HGRN_W, HGRN_FW, HGRN_W,
               XA_W, XA_W,
               D_MODEL, D_MODEL, D_MODEL)
PROJ_TOTAL = sum(PROJ_WIDTHS)
EPS = 1e-6

kernel_name = "hybrid_moba_hgrn2_memxattn_gated"


def _rmsnorm(x, w):
    x32 = x.astype(jnp.float32)
    y = x32 * lax.rsqrt(jnp.mean(x32 * x32, axis=-1, keepdims=True) + EPS)
    return (y * w.astype(jnp.float32)).astype(x.dtype)


def _alibi_slopes(n):
    return jnp.exp2(-8.0 * jnp.arange(1, n + 1, dtype=jnp.float32) / n)


def _moba(q, k, v):
    bsz, s, nh, dh = q.shape
    nb = -(-s // MOBA_BLOCK)
    topk = min(MOBA_TOPK, nb)
    scale = dh ** -0.5
    slopes = _alibi_slopes(nh)
    q = q.transpose(0, 2, 1, 3)
    k = k.transpose(0, 2, 1, 3)
    v = v.transpose(0, 2, 1, 3)
    pad = nb * MOBA_BLOCK - s
    kp = jnp.pad(k, ((0, 0), (0, 0), (0, pad), (0, 0)))
    vp = jnp.pad(v, ((0, 0), (0, 0), (0, pad), (0, 0)))
    kb = kp.reshape(bsz, nh, nb, MOBA_BLOCK, dh)
    vb = vp.reshape(bsz, nh, nb, MOBA_BLOCK, dh)
    kbar = jnp.mean(kb.astype(jnp.float32), axis=3)
    pos = jnp.arange(s)
    qblk = pos // MOBA_BLOCK
    gate = jnp.einsum('bhtd,bhnd->bhtn', q.astype(jnp.float32), kbar)
    past = jnp.arange(nb)[None, :] < qblk[:, None]
    gate = jnp.where(past, gate, -jnp.inf)
    gval, gidx = lax.top_k(gate, topk)
    gok = jnp.isfinite(gval)

    nq = s // MOBA_QCHUNK

    def to_chunks(a):
        a = a.reshape(bsz, nh, nq, MOBA_QCHUNK, *a.shape[3:])
        return jnp.moveaxis(a, 2, 0)

    bi = jnp.arange(bsz)[:, None, None, None]
    hi = jnp.arange(nh)[None, :, None, None]
    offs = jnp.arange(MOBA_BLOCK)

    def chunk(args):
        c, qc, ic, okc = args
        t = c * MOBA_QCHUNK + jnp.arange(MOBA_QCHUNK)
        start = (c * MOBA_QCHUNK) // MOBA_BLOCK * MOBA_BLOCK
        ko = lax.dynamic_slice_in_dim(kp, start, MOBA_BLOCK, axis=2)
        vo = lax.dynamic_slice_in_dim(vp, start, MOBA_BLOCK, axis=2)
        so = start + offs
        dist_o = (t[:, None] - so[None, :]).astype(jnp.float32)
        lo = (jnp.einsum('bhqd,bhkd->bhqk', qc, ko).astype(jnp.float32) * scale
              - slopes[:, None, None] * dist_o)
        lo = jnp.where(so[None, :] <= t[:, None], lo, -jnp.inf)
        kg = kb[bi, hi, ic]
        vg = vb[bi, hi, ic]
        sg = ic[..., None] * MOBA_BLOCK + offs
        dist_s = (t[None, None, :, None, None] - sg).astype(jnp.float32)
        ls = (jnp.einsum('bhqd,bhqjkd->bhqjk', qc, kg).astype(jnp.float32) * scale
              - slopes[None, :, None, None, None] * dist_s)
        ls = jnp.where(okc[..., None], ls, -jnp.inf)
        logits = jnp.concatenate(
            [ls.reshape(bsz, nh, MOBA_QCHUNK, topk * MOBA_BLOCK), lo], axis=-1)
        p = jax.nn.softmax(logits, axis=-1)
        ps = p[..., :topk * MOBA_BLOCK].reshape(bsz, nh, MOBA_QCHUNK, topk, MOBA_BLOCK).astype(v.dtype)
        po = p[..., topk * MOBA_BLOCK:].astype(v.dtype)
        return (jnp.einsum('bhqjk,bhqjkd->bhqd', ps, vg)
                + jnp.einsum('bhqk,bhkd->bhqd', po, vo))

    out = lax.map(chunk, (jnp.arange(nq), to_chunks(q), to_chunks(gidx), to_chunks(gok)))
    out = jnp.moveaxis(out, 0, 2).reshape(bsz, nh, s, dh)
    return out.transpose(0, 2, 1, 3).reshape(bsz, s, nh * dh)


def _hgrn2(f_logit, inp, qry, lb):
    bsz, s, nh, dk = f_logit.shape
    dv = inp.shape[-1]
    c = HGRN_CHUNK
    n = s // c
    fl = f_logit.astype(jnp.float32)
    log_f = jnp.log(lb + (1.0 - lb) * jax.nn.sigmoid(fl))
    kk = (1.0 - lb) * jax.nn.sigmoid(-fl)

    def chunks(a):
        return a.reshape(bsz, n, c, nh, a.shape[-1]).transpose(0, 3, 1, 2, 4)

    log_f, kk = chunks(log_f), chunks(kk)
    qq = chunks(qry.astype(jnp.float32))
    vv = chunks(inp.astype(jnp.float32))
    b = jnp.cumsum(log_f, axis=3)
    q_t = qq * jnp.exp(b)
    k_t = kk * jnp.exp(-b)
    a = jnp.einsum('bhnid,bhnjd->bhnij', q_t, k_t)
    tril = jnp.tril(jnp.ones((c, c), dtype=bool))
    a = jnp.where(tril, a, 0.0)
    o_intra = jnp.einsum('bhnij,bhnje->bhnie', a, vv)
    b_last = b[..., -1:, :]
    u = jnp.einsum('bhncd,bhnce->bhnde', kk * jnp.exp(b_last - b), vv)
    decay = jnp.exp(b_last[..., 0, :])

    def step(state, xs):
        dec, uu = xs
        return dec[..., None] * state + uu, state

    s0 = jnp.zeros((bsz, nh, dk, dv), jnp.float32)
    _, s_prev = lax.scan(step, s0, (jnp.moveaxis(decay, 2, 0), jnp.moveaxis(u, 2, 0)))
    o_inter = jnp.einsum('bhnid,nbhde->bhnie', q_t, s_prev)
    o = o_intra + o_inter
    return o.transpose(0, 2, 3, 1, 4).reshape(bsz, s, nh, dv)


def _mem_xattn(q, mem_n, w_kv):
    bsz, s, _ = q.shape
    m = mem_n.shape[1]
    kv = mem_n @ w_kv
    km, vm = jnp.split(kv, 2, axis=-1)
    km = km.reshape(bsz, m, XA_HEADS, XA_HD)
    vm = vm.reshape(bsz, m, XA_HEADS, XA_HD)
    qh = q.reshape(bsz, s, XA_HEADS, XA_HD)
    logits = jnp.einsum('bshd,bmhd->bhsm', qh, km).astype(jnp.float32) * (XA_HD ** -0.5)
    p = jax.nn.softmax(logits, axis=-1).astype(vm.dtype)
    return jnp.einsum('bhsm,bmhd->bshd', p, vm).reshape(bsz, s, XA_W)


def setup_inputs(seed: int = 0) -> dict:
    key = jax.random.key(seed)
    ks = jax.random.split(key, 13)
    f32 = jnp.float32
    nrm = lambda k, shp, sc: jax.random.normal(k, shp, f32) * sc
    return {
        "x": nrm(ks[0], (BATCH, SEQ, D_MODEL), 1.0),
        "mem": nrm(ks[1], (BATCH, N_MEM, D_MODEL), 1.0),
        "pre_norm_w": 1.0 + nrm(ks[2], (DEPTH, D_MODEL), 0.1),
        "w_in": nrm(ks[3], (DEPTH, D_MODEL, PROJ_TOTAL), D_MODEL ** -0.5),
        "hgrn_lb_logits": nrm(ks[4], (DEPTH + 1, HGRN_FW), 0.1),
        "hgrn_norm_w": 1.0 + nrm(ks[5], (DEPTH, HGRN_DV), 0.1),
        "mem_norm_w": 1.0 + nrm(ks[6], (DEPTH, D_MODEL), 0.1),
        "w_mem_kv": nrm(ks[7], (DEPTH, D_MODEL, 2 * XA_W), D_MODEL ** -0.5),
        "w_branch_a": nrm(ks[8], (DEPTH, MOBA_W, D_MODEL), MOBA_W ** -0.5),
        "w_branch_b": nrm(ks[9], (DEPTH, HGRN_W, D_MODEL), HGRN_W ** -0.5),
        "w_branch_c": nrm(ks[10], (DEPTH, XA_W, D_MODEL), XA_W ** -0.5),
        "w_out": nrm(ks[11], (DEPTH, D_MODEL, D_MODEL), D_MODEL ** -0.5),
        "post_norm_w": 1.0 + nrm(ks[12], (DEPTH, D_MODEL), 0.1),
    }


def reference(x, mem, pre_norm_w, w_in, hgrn_lb_logits, hgrn_norm_w, mem_norm_w, w_mem_kv,
              w_branch_a, w_branch_b, w_branch_c, w_out, post_norm_w):
    bsz, s, _ = x.shape
    split_idx = [int(v) for v in np.cumsum(PROJ_WIDTHS)[:-1]]
    lb_all = jnp.cumsum(jax.nn.softmax(hgrn_lb_logits.astype(jnp.float32), axis=0), axis=0)
    for l in range(DEPTH):
        h = _rmsnorm(x, pre_norm_w[l])
        proj = h @ w_in[l]
        (qa, ka, va, za, fb, ib, qb, gb, qc, zc,
         gate_a, gate_b, gate_c) = jnp.split(proj, split_idx, axis=-1)

        hs = lambda t: t.reshape(bsz, s, MOBA_HEADS, MOBA_HD)
        ya = _moba(hs(qa), hs(ka), hs(va)) * jax.nn.silu(za)

        lb = lb_all[l].reshape(HGRN_HEADS, HGRN_DK)
        ob = _hgrn2(fb.reshape(bsz, s, HGRN_HEADS, HGRN_DK),
                    ib.reshape(bsz, s, HGRN_HEADS, HGRN_DV),
                    qb.reshape(bsz, s, HGRN_HEADS, HGRN_DK), lb)
        ob = _rmsnorm(ob, hgrn_norm_w[l]).reshape(bsz, s, HGRN_W).astype(x.dtype)
        yb = ob * jax.nn.silu(gb)

        mem_n = _rmsnorm(mem, mem_norm_w[l])
        yc = _mem_xattn(qc, mem_n, w_mem_kv[l]) * jax.nn.silu(zc)

        merged = (jax.nn.sigmoid(gate_a) * (ya @ w_branch_a[l])
                  + jax.nn.sigmoid(gate_b) * (yb @ w_branch_b[l])
                  + jax.nn.sigmoid(gate_c) * (yc @ w_branch_c[l]))
        y = merged @ w_out[l]
        x = x + _rmsnorm(y, post_norm_w[l])
    return x
```

```python
import functools

import jax
import jax.numpy as jnp
from jax import lax
from jax.experimental import pallas as pl
from jax.experimental.pallas import tpu as pltpu

F32 = jnp.float32
BF16 = jnp.bfloat16
EPS = 1e-6

MOBA_HEADS, MOBA_HD, MOBA_BLOCK, MOBA_TOPK = 8, 64, 256, 3
HGRN_HEADS, HGRN_D, HGRN_CHUNK = 4, 128, 32
XA_HEADS, XA_HD = 4, 128
BRANCH_W = 512
LANES = 128
COL_QA, COL_KA, COL_VA, COL_ZA = 0, 512, 1024, 1536
COL_FB, COL_IB, COL_QB, COL_GB = 2048, 2560, 3072, 3584
COL_QC, COL_ZC = 4096, 4608
COL_GATE_A, COL_GATE_B, COL_GATE_C = 5120, 6144, 7168
PROJ_TOTAL = COL_GATE_C + 1024

NEG_BIG = -1e30
VMEM_LIMIT = 56 * 1024 * 1024


def _sigmoid(z):
    return 1.0 / (1.0 + jnp.exp(-z))


def _silu(z):
    return z * _sigmoid(z)


def _split3(a):
    hi = a.astype(BF16).astype(F32)
    r = a - hi
    mid = r.astype(BF16).astype(F32)
    lo = (r - mid).astype(BF16).astype(F32)
    return hi, mid, lo


def _in_proj_body(x_ref, nw_ref, w_ref, o_ref, h_ref):
    @pl.when(pl.program_id(1) == 0)
    def _():
        x = x_ref[...]
        ms = jnp.mean(x * x, axis=-1, keepdims=True)
        h_ref[...] = (x * lax.rsqrt(ms + EPS) * nw_ref[...]).astype(BF16)

    o_ref[...] = jnp.dot(h_ref[...], w_ref[...], preferred_element_type=F32).astype(o_ref.dtype)


def _in_proj(x2, norm_w, w_bf16, tm=1024, tn=1024):
    t, d = x2.shape
    n = w_bf16.shape[1]
    return pl.pallas_call(
        _in_proj_body,
        grid=(t // tm, n // tn),
        in_specs=[
            pl.BlockSpec((tm, d), lambda i, j: (i, 0)),
            pl.BlockSpec((1, d), lambda i, j: (0, 0)),
            pl.BlockSpec((d, tn), lambda i, j: (0, j)),
        ],
        out_specs=pl.BlockSpec((tm, tn), lambda i, j: (i, j)),
        out_shape=jax.ShapeDtypeStruct((t, n), BF16),
        scratch_shapes=[pltpu.VMEM((tm, d), BF16)],
        compiler_params=pltpu.CompilerParams(
            dimension_semantics=("parallel", "arbitrary"), vmem_limit_bytes=VMEM_LIMIT),
        name="in_proj",
    )(x2, norm_w.reshape(1, d), w_bf16)


def _moba_aug_keys(s, nb_lanes=16):
    row = jnp.arange(s)
    blk = row // MOBA_BLOCK
    off = (row % MOBA_BLOCK).astype(F32)
    onehot = (blk[:, None] == jnp.arange(nb_lanes)[None, :]).astype(F32)
    bias_lanes = jnp.concatenate([onehot, onehot, onehot, off[:, None], off[:, None], off[:, None],
                                  jnp.zeros((s, 64 - 3 * nb_lanes - 3), F32)], axis=1)
    zeros = jnp.zeros((s, 64), F32)
    a0 = jnp.concatenate([zeros, bias_lanes], axis=1)
    a1 = jnp.concatenate([bias_lanes, zeros], axis=1)
    return jnp.stack([a0, a1]).astype(BF16)


def _moba_body(slopes_ref, q_ref, k_ref, v_ref, z_ref, aug_ref, avg_ref, causal_ref, o_ref,
               ka_ref, vt_ref, kbar_ref):
    s = q_ref.shape[0]
    nb = s // MOBA_BLOCK
    hp = pl.program_id(1)
    blk = MOBA_BLOCK
    lane = lax.broadcasted_iota(jnp.int32, (1, LANES), 1)
    first_half = lane < 64

    k_all = k_ref[...]
    kbar_ref[...] = jnp.dot(avg_ref[...], k_all, preferred_element_type=F32)
    for e in range(2):
        keep = first_half if e == 0 else jnp.logical_not(first_half)
        for j in range(nb):
            rows = slice(j * blk, (j + 1) * blk)
            ka_ref[e, j] = jnp.where(keep, k_all[rows], aug_ref[e, rows, :])
    for j in range(nb):
        vt_ref[j] = v_ref[j * blk:(j + 1) * blk, :].astype(F32).T.astype(BF16)

    kbar = kbar_ref[...]
    kb_pieces = []
    for e in range(2):
        keep = first_half if e == 0 else jnp.logical_not(first_half)
        kb_pieces.append([p.astype(BF16) for p in _split3(jnp.where(keep, kbar, 0.0))])

    nidx = lax.broadcasted_iota(jnp.int32, (16, blk), 0)
    qoff = lax.broadcasted_iota(jnp.int32, (16, blk), 1)
    row16 = lax.broadcasted_iota(jnp.int32, (16, blk), 0)
    slope_rows = []
    for e in range(2):
        sl = slopes_ref[2 * hp + e]
        s_hi, s_mid, s_lo = _split3(jnp.full((16, blk), sl, F32))
        slope_rows.append((sl, jnp.where(row16 == 0, s_hi, jnp.where(row16 == 1, s_mid,
                                         jnp.where(row16 == 2, s_lo, 0.0)))))

    def q_block(i, carry):
        r0 = pl.multiple_of(i * blk, blk)
        qt = (q_ref[pl.ds(r0, blk), :].astype(F32) * (MOBA_HD ** -0.5)).T
        qt_bf = qt.astype(BF16)
        dist0 = (qoff + (i - nidx) * blk).astype(F32)

        qa = []
        for e in range(2):
            g = (jnp.dot(kb_pieces[e][0], qt_bf, preferred_element_type=F32)
                 + jnp.dot(kb_pieces[e][1], qt_bf, preferred_element_type=F32)
                 + jnp.dot(kb_pieces[e][2], qt_bf, preferred_element_type=F32))
            cnt = jnp.zeros((16, blk), jnp.int32)
            for m in range(nb):
                gm = g[m:m + 1, :]
                ahead = (gm > g) | ((gm == g) & (m < nidx))
                cnt = cnt + jnp.where(ahead, jnp.where(m < i, 1, 0), 0)
            sel = ((nidx < i) & (cnt < MOBA_TOPK)) | (nidx == i)
            sl, srow = slope_rows[e]
            bias = jnp.where(sel, -sl * dist0, NEG_BIG)
            b_hi, b_mid, b_lo = _split3(bias)
            qh = qt[0:64] if e == 0 else qt[64:128]
            parts = [qh, b_hi, b_mid, b_lo, srow] if e == 0 else [b_hi, b_mid, b_lo, srow, qh]
            qa.append(jnp.concatenate(parts, axis=0).astype(BF16))

        def tile(j, st, masked):
            out = []
            for e in range(2):
                m_run, l_run, acc = st[e]
                sT = jnp.dot(ka_ref[e, j], qa[e], preferred_element_type=F32)
                if masked:
                    sT = jnp.where(causal_ref[...] > 0.5, sT, -jnp.inf)
                m_new = jnp.maximum(m_run, jnp.max(sT, axis=0, keepdims=True))
                alpha = jnp.exp(m_run - m_new)
                p = jnp.exp(sT - m_new)
                l_new = alpha * l_run + jnp.sum(p, axis=0, keepdims=True)
                vt = vt_ref[j, e * 64:(e + 1) * 64, :]
                acc = alpha * acc + jnp.dot(vt, p.astype(BF16), preferred_element_type=F32)
                out.append((m_new, l_new, acc))
            return tuple(out)

        init = tuple((jnp.full((1, blk), -jnp.inf, F32), jnp.zeros((1, blk), F32),
                      jnp.zeros((64, blk), F32)) for _ in range(2))
        st = lax.fori_loop(0, i, lambda j, c: tile(j, c, False), init)
        st = tile(i, st, True)
        ot = jnp.concatenate([st[0][2] / st[0][1], st[1][2] / st[1][1]], axis=0)
        z = z_ref[pl.ds(r0, blk), :].astype(F32)
        o_ref[pl.ds(r0, blk), :] = (ot.T * _silu(z)).astype(o_ref.dtype)
        return carry

    lax.fori_loop(0, nb, q_block, 0)


def _moba(proj, slopes, bsz, s):
    nb = s // MOBA_BLOCK
    assert nb <= 16
    n_pairs = MOBA_HEADS // 2
    aug = _moba_aug_keys(s)
    avg = ((jnp.arange(16)[:, None] == (jnp.arange(s) // MOBA_BLOCK)[None, :]).astype(F32)
           / MOBA_BLOCK).astype(BF16)
    causal = (jnp.arange(MOBA_BLOCK)[:, None] <= jnp.arange(MOBA_BLOCK)[None, :]).astype(F32)
    col = lambda c0: (lambda b, hp, *_: (b, c0 // LANES + hp))
    return pl.pallas_call(
        _moba_body,
        grid_spec=pltpu.PrefetchScalarGridSpec(
            num_scalar_prefetch=1,
            grid=(bsz, n_pairs),
            in_specs=[
                pl.BlockSpec((s, LANES), col(COL_QA)),
                pl.BlockSpec((s, LANES), col(COL_KA)),
                pl.BlockSpec((s, LANES), col(COL_VA)),
                pl.BlockSpec((s, LANES), col(COL_ZA)),
                pl.BlockSpec((2, s, LANES), lambda b, hp, *_: (0, 0, 0)),
                pl.BlockSpec((16, s), lambda b, hp, *_: (0, 0)),
                pl.BlockSpec((MOBA_BLOCK, MOBA_BLOCK), lambda b, hp, *_: (0, 0)),
            ],
            out_specs=pl.BlockSpec((s, LANES), lambda b, hp, *_: (b, hp)),
            scratch_shapes=[
                pltpu.VMEM((2, nb, MOBA_BLOCK, LANES), BF16),
                pltpu.VMEM((nb, LANES, MOBA_BLOCK), BF16),
                pltpu.VMEM((16, LANES), F32),
            ],
        ),
        out_shape=jax.ShapeDtypeStruct((bsz * s, BRANCH_W), BF16),
        compiler_params=pltpu.CompilerParams(
            dimension_semantics=("parallel", "parallel"), vmem_limit_bytes=VMEM_LIMIT),
        name="moba",
    )(slopes, proj, proj, proj, proj, aug, avg, causal)


HGRN_TILE = 256


def _hgrn_body(f_ref, i_ref, q_ref, g_ref, lbl_ref, nw_ref, lj_ref, tril_ref, o_ref, st_ref):
    s = f_ref.shape[0]
    r = HGRN_TILE
    c = HGRN_CHUNK
    n_chunks = r // c
    logits = lbl_ref[:, 0, :]
    ex = jnp.exp(logits - jnp.max(logits, axis=0, keepdims=True))
    lb = ex[0:1, :] / jnp.sum(ex, axis=0, keepdims=True)
    st_ref[...] = jnp.zeros_like(st_ref)
    rowc = lax.broadcasted_iota(jnp.int32, (r, 1), 0) // c

    def tile(t, carry):
        r0 = pl.multiple_of(t * r, r)
        fl = f_ref[pl.ds(r0, r), :].astype(F32)
        log_f = jnp.log(lb + (1.0 - lb) * _sigmoid(fl))
        kk = (1.0 - lb) * _sigmoid(-fl)
        hi, mid, lo = _split3(log_f)
        lj = lj_ref[...]
        bb = (jnp.dot(lj, hi.astype(BF16), preferred_element_type=F32)
              + jnp.dot(lj, mid.astype(BF16), preferred_element_type=F32)
              + jnp.dot(lj, lo.astype(BF16), preferred_element_type=F32))
        b = bb[0:r]
        b_last = bb[r:2 * r]
        qq = q_ref[pl.ds(r0, r), :].astype(F32)
        vv = i_ref[pl.ds(r0, r), :]
        q_t = (qq * jnp.exp(b)).astype(BF16)
        k_t = (kk * jnp.exp(-b)).astype(BF16)
        kd = kk * jnp.exp(b_last - b)
        a = lax.dot_general(q_t, k_t, (((1,), (1,)), ((), ())), preferred_element_type=F32)
        a = a * tril_ref[...]
        o = jnp.dot(a.astype(BF16), vv, preferred_element_type=F32)
        vt = vv.astype(F32).T.astype(BF16)
        dec = jnp.exp(b_last)
        o_inter = []
        st = st_ref[...]
        for ci in range(n_chunks):
            rows = slice(ci * c, (ci + 1) * c)
            o_inter.append(lax.dot_general(q_t[rows], st.astype(BF16), (((1,), (1,)), ((), ())),
                                           preferred_element_type=F32))
            kd_c = jnp.where(rowc == ci, kd, 0.0).astype(BF16)
            u_t = jnp.dot(vt, kd_c, preferred_element_type=F32)
            st = st * dec[ci * c:ci * c + 1, :] + u_t
        st_ref[...] = st
        o = o + jnp.concatenate(o_inter, axis=0)
        on = o * lax.rsqrt(jnp.mean(o * o, axis=-1, keepdims=True) + EPS) * nw_ref[...]
        g = g_ref[pl.ds(r0, r), :].astype(F32)
        o_ref[pl.ds(r0, r), :] = (on * _silu(g)).astype(o_ref.dtype)
        return carry

    lax.fori_loop(0, s // r, tile, 0)


def _hgrn(proj, lb_logits, norm_w, bsz, s):
    r, c = HGRN_TILE, HGRN_CHUNK
    ri = jnp.arange(r)
    same = (ri[:, None] // c) == (ri[None, :] // c)
    tril = (same & (ri[None, :] <= ri[:, None])).astype(F32)
    lj = jnp.concatenate([tril, same.astype(F32)], axis=0).astype(BF16)
    lbl = lb_logits.reshape(lb_logits.shape[0], HGRN_HEADS, 1, HGRN_D)
    col = lambda c0: (lambda b, h: (b, c0 // LANES + h))
    const = lambda shape: pl.BlockSpec(shape, lambda b, h: (0,) * len(shape))
    return pl.pallas_call(
        _hgrn_body,
        grid=(bsz, HGRN_HEADS),
        in_specs=[
            pl.BlockSpec((s, LANES), col(COL_FB)),
            pl.BlockSpec((s, LANES), col(COL_IB)),
            pl.BlockSpec((s, LANES), col(COL_QB)),
            pl.BlockSpec((s, LANES), col(COL_GB)),
            pl.BlockSpec((lb_logits.shape[0], None, 1, HGRN_D), lambda b, h: (0, h, 0, 0)),
            const((1, HGRN_D)),
            const((2 * r, r)),
            const((r, r)),
        ],
        out_specs=pl.BlockSpec((s, LANES), lambda b, h: (b, h)),
        out_shape=jax.ShapeDtypeStruct((bsz * s, BRANCH_W), BF16),
        scratch_shapes=[pltpu.VMEM((HGRN_D, HGRN_D), F32)],
        compiler_params=pltpu.CompilerParams(
            dimension_semantics=("parallel", "parallel"), vmem_limit_bytes=VMEM_LIMIT),
        name="hgrn",
    )(proj, proj, proj, proj, lbl, norm_w.reshape(1, HGRN_D), lj, tril)


def _xattn_body(q_ref, z_ref, mem_ref, mw_ref, wkv_ref, o_ref, km_ref, vm_ref):
    @pl.when(pl.program_id(1) == 0)
    def _():
        m = mem_ref[0]
        mn = m * lax.rsqrt(jnp.mean(m * m, axis=-1, keepdims=True) + EPS) * mw_ref[...]
        kv = jnp.dot(mn.astype(BF16), wkv_ref[...], preferred_element_type=F32)
        km_ref[...] = kv[:, :BRANCH_W].astype(BF16)
        vm_ref[...] = kv[:, BRANCH_W:].astype(BF16)

    for h in range(XA_HEADS):
        cols = slice(h * XA_HD, (h + 1) * XA_HD)
        logits = lax.dot_general(q_ref[:, cols], km_ref[:, cols], (((1,), (1,)), ((), ())),
                                 preferred_element_type=F32) * (XA_HD ** -0.5)
        mx = jnp.max(logits, axis=-1, keepdims=True)
        p = jnp.exp(logits - mx)
        den = jnp.sum(p, axis=-1, keepdims=True)
        o = jnp.dot(p.astype(BF16), vm_ref[:, cols], preferred_element_type=F32) / den
        o_ref[:, cols] = (o * _silu(z_ref[:, cols].astype(F32))).astype(o_ref.dtype)


def _xattn(proj, mem, mem_norm_w, wkv_bf16, bsz, s, tq=1024):
    n_mem, d = mem.shape[1], mem.shape[2]
    nq = s // tq
    return pl.pallas_call(
        _xattn_body,
        grid=(bsz, nq),
        in_specs=[
            pl.BlockSpec((tq, BRANCH_W), lambda b, i: (b * nq + i, COL_QC // BRANCH_W)),
            pl.BlockSpec((tq, BRANCH_W), lambda b, i: (b * nq + i, COL_ZC // BRANCH_W)),
            pl.BlockSpec((1, n_mem, d), lambda b, i: (b, 0, 0)),
            pl.BlockSpec((1, d), lambda b, i: (0, 0)),
            pl.BlockSpec((d, 2 * BRANCH_W), lambda b, i: (0, 0)),
        ],
        out_specs=pl.BlockSpec((tq, BRANCH_W), lambda b, i: (b * nq + i, 0)),
        out_shape=jax.ShapeDtypeStruct((bsz * s, BRANCH_W), BF16),
        scratch_shapes=[pltpu.VMEM((n_mem, BRANCH_W), BF16), pltpu.VMEM((n_mem, BRANCH_W), BF16)],
        compiler_params=pltpu.CompilerParams(
            dimension_semantics=("parallel", "arbitrary"), vmem_limit_bytes=VMEM_LIMIT),
        name="xattn",
    )(proj, proj, mem, mem_norm_w.reshape(1, d), wkv_bf16)


def _merge_body(x_ref, ga_ref, gb_ref, gc_ref, ya_ref, yb_ref, yc_ref,
                wa_ref, wb_ref, wc_ref, wo_ref, pw_ref, o_ref):
    def branch(g_ref, y_ref, w_ref):
        return _sigmoid(g_ref[...].astype(F32)) * jnp.dot(y_ref[...], w_ref[...],
                                                          preferred_element_type=F32)

    merged = branch(ga_ref, ya_ref, wa_ref) + branch(gb_ref, yb_ref, wb_ref) + branch(gc_ref, yc_ref, wc_ref)
    y = jnp.dot(merged.astype(BF16), wo_ref[...], preferred_element_type=F32)
    yn = y * lax.rsqrt(jnp.mean(y * y, axis=-1, keepdims=True) + EPS) * pw_ref[...]
    o_ref[...] = x_ref[...] + yn


def _merge(x2, proj, ya, yb, yc, wa, wb, wc, wo, post_w, tm=512):
    t, d = x2.shape
    row = lambda c: (lambda i: (i, c))
    const = lambda shape: pl.BlockSpec(shape, lambda i: (0, 0))
    return pl.pallas_call(
        _merge_body,
        grid=(t // tm,),
        in_specs=[
            pl.BlockSpec((tm, d), row(0)),
            pl.BlockSpec((tm, d), row(COL_GATE_A // d)),
            pl.BlockSpec((tm, d), row(COL_GATE_B // d)),
            pl.BlockSpec((tm, d), row(COL_GATE_C // d)),
            pl.BlockSpec((tm, BRANCH_W), row(0)),
            pl.BlockSpec((tm, BRANCH_W), row(0)),
            pl.BlockSpec((tm, BRANCH_W), row(0)),
            const((BRANCH_W, d)), const((BRANCH_W, d)), const((BRANCH_W, d)),
            const((d, d)), const((1, d)),
        ],
        out_specs=pl.BlockSpec((tm, d), row(0)),
        out_shape=jax.ShapeDtypeStruct((t, d), F32),
        compiler_params=pltpu.CompilerParams(
            dimension_semantics=("parallel",), vmem_limit_bytes=VMEM_LIMIT),
        name="merge",
    )(x2, proj, proj, proj, ya, yb, yc, wa, wb, wc, wo, post_w.reshape(1, d))


def kernel(x, mem, pre_norm_w, w_in, hgrn_lb_logits, hgrn_norm_w, mem_norm_w, w_mem_kv,
           w_branch_a, w_branch_b, w_branch_c, w_out, post_norm_w):
    bsz, s, d = x.shape
    assert w_in.shape[0] == 1 and w_in.shape[2] == PROJ_TOTAL and d == 1024
    assert s % MOBA_BLOCK == 0 and s % HGRN_TILE == 0
    x2 = x.reshape(bsz * s, d)
    proj = _in_proj(x2, pre_norm_w[0], w_in[0].astype(BF16))
    slopes = jnp.exp2(-8.0 * jnp.arange(1, MOBA_HEADS + 1, dtype=F32) / MOBA_HEADS)
    ya = _moba(proj, slopes, bsz, s)
    yb = _hgrn(proj, hgrn_lb_logits, hgrn_norm_w[0], bsz, s)
    yc = _xattn(proj, mem, mem_norm_w[0], w_mem_kv[0].astype(BF16), bsz, s)
    out = _merge(x2, proj, ya, yb, yc, w_branch_a[0].astype(BF16), w_branch_b[0].astype(BF16),
                 w_branch_c[0].astype(BF16), w_out[0].astype(BF16), post_norm_w[0])
    return out.reshape(bsz, s, d)
```

```python
import jax
import jax.numpy as jnp
from jax import lax
from jax.experimental import pallas as pl
from jax.experimental.pallas import tpu as pltpu

F32 = jnp.float32
BF16 = jnp.bfloat16
EPS = 1e-6

MOBA_HEADS, MOBA_HD, MOBA_BLOCK, MOBA_TOPK = 8, 64, 256, 3
HGRN_HEADS, HGRN_D, HGRN_CHUNK = 4, 128, 32
XA_HEADS, XA_HD = 4, 128
BRANCH_W = 512
LANES = 128
COL_QA, COL_KA, COL_VA, COL_ZA = 0, 512, 1024, 1536
COL_FB, COL_IB, COL_QB, COL_GB = 2048, 2560, 3072, 3584
COL_QC, COL_ZC = 4096, 4608
COL_GATE_A, COL_GATE_B, COL_GATE_C = 5120, 6144, 7168
PROJ_TOTAL = COL_GATE_C + 1024

NEG_BIG = -1e30
VMEM_LIMIT = 56 * 1024 * 1024


def _sigmoid(z):
    return 1.0 / (1.0 + jnp.exp(-z))


def _silu(z):
    return z * _sigmoid(z)


def _split3(a):
    hi = a.astype(BF16).astype(F32)
    r = a - hi
    mid = r.astype(BF16).astype(F32)
    lo = (r - mid).astype(BF16).astype(F32)
    return hi, mid, lo


def _in_proj_body(x_ref, nw_ref, w_ref, o_ref, h_ref):
    @pl.when(pl.program_id(1) == 0)
    def _():
        x = x_ref[...]
        ms = jnp.mean(x * x, axis=-1, keepdims=True)
        h_ref[...] = (x * lax.rsqrt(ms + EPS) * nw_ref[...]).astype(BF16)

    o_ref[...] = jnp.dot(h_ref[...], w_ref[...], preferred_element_type=F32).astype(o_ref.dtype)


def _in_proj(x2, norm_w, w_bf16, tm=1024, tn=1024):
    t, d = x2.shape
    n = w_bf16.shape[1]
    return pl.pallas_call(
        _in_proj_body,
        grid=(t // tm, n // tn),
        in_specs=[
            pl.BlockSpec((tm, d), lambda i, j: (i, 0)),
            pl.BlockSpec((1, d), lambda i, j: (0, 0)),
            pl.BlockSpec((d, tn), lambda i, j: (0, j)),
        ],
        out_specs=pl.BlockSpec((tm, tn), lambda i, j: (i, j)),
        out_shape=jax.ShapeDtypeStruct((t, n), BF16),
        scratch_shapes=[pltpu.VMEM((tm, d), BF16)],
        compiler_params=pltpu.CompilerParams(
            dimension_semantics=("parallel", "arbitrary"), vmem_limit_bytes=VMEM_LIMIT),
        name="in_proj",
    )(x2, norm_w.reshape(1, d), w_bf16)


def _moba_aug_keys(s, nb_lanes=16):
    row = jnp.arange(s)
    blk = row // MOBA_BLOCK
    off = (row % MOBA_BLOCK).astype(F32)
    onehot = (blk[:, None] == jnp.arange(nb_lanes)[None, :]).astype(F32)
    bias_lanes = jnp.concatenate([onehot, onehot, onehot, off[:, None], off[:, None], off[:, None],
                                  jnp.zeros((s, 64 - 3 * nb_lanes - 3), F32)], axis=1)
    zeros = jnp.zeros((s, 64), F32)
    a0 = jnp.concatenate([zeros, bias_lanes], axis=1)
    a1 = jnp.concatenate([bias_lanes, zeros], axis=1)
    return jnp.stack([a0, a1]).astype(BF16)


def _moba_body(slopes_ref, q_ref, k_ref, v_ref, z_ref, aug_ref, avg_ref, causal_ref, o_ref,
               ka_ref, vt_ref, kbp_ref, qa_ref, m_ref, l_ref, acc_ref):
    s = k_ref.shape[0]
    nb = s // MOBA_BLOCK
    blk = MOBA_BLOCK
    nh = MOBA_HEADS
    i = pl.program_id(1)

    @pl.when(i == 0)
    def _():
        lane = lax.broadcasted_iota(jnp.int32, (1, LANES), 1)
        first_half = lane < 64
        kbar = jnp.dot(avg_ref[...], k_ref[...], preferred_element_type=F32)
        for h in range(nh):
            hp, e = h // 2, h % 2
            cols = slice(hp * LANES, (hp + 1) * LANES)
            keep = first_half if e == 0 else jnp.logical_not(first_half)
            for j in range(nb):
                rows = slice(j * blk, (j + 1) * blk)
                ka_ref[h, j] = jnp.where(keep, k_ref[rows, cols], aug_ref[e, rows, :])
            pieces = _split3(jnp.where(keep, kbar[:, cols], 0.0))
            for p in range(3):
                kbp_ref[(h * 3 + p) * 16:(h * 3 + p + 1) * 16, :] = pieces[p].astype(BF16)
        for j in range(nb):
            vt_ref[j] = v_ref[j * blk:(j + 1) * blk, :].astype(F32).T.astype(BF16)

    nidx = lax.broadcasted_iota(jnp.int32, (16, blk), 0)
    qoff = lax.broadcasted_iota(jnp.int32, (16, blk), 1)
    dist0 = (qoff + (i - nidx) * blk).astype(F32)
    for hp in range(nh // 2):
        cols = slice(hp * LANES, (hp + 1) * LANES)
        qt = (q_ref[:, cols].astype(F32) * (MOBA_HD ** -0.5)).T
        gates = jnp.dot(kbp_ref[hp * 96:(hp + 1) * 96, :], qt.astype(BF16),
                        preferred_element_type=F32)
        for e in range(2):
            h = 2 * hp + e
            g = gates[e * 48:e * 48 + 16] + gates[e * 48 + 16:e * 48 + 32] + gates[e * 48 + 32:e * 48 + 48]
            cnt = jnp.zeros((16, blk), jnp.int32)
            for m in range(nb):
                gm = g[m:m + 1, :]
                ahead = (gm > g) | ((gm == g) & (m < nidx))
                cnt = cnt + jnp.where(ahead, jnp.where(m < i, 1, 0), 0)
            sel = ((nidx < i) & (cnt < MOBA_TOPK)) | (nidx == i)
            sl = slopes_ref[h]
            bias = jnp.where(sel, -sl * dist0, NEG_BIG)
            b_hi, b_mid, b_lo = _split3(bias)
            s_hi, s_mid, s_lo = _split3(jnp.full((16, blk), sl, F32))
            srow = jnp.where(nidx == 0, s_hi, jnp.where(nidx == 1, s_mid, jnp.where(nidx == 2, s_lo, 0.0)))
            qh = qt[0:64] if e == 0 else qt[64:128]
            parts = [qh, b_hi, b_mid, b_lo, srow] if e == 0 else [b_hi, b_mid, b_lo, srow, qh]
            qa_ref[h] = jnp.concatenate(parts, axis=0).astype(BF16)

    m_ref[...] = jnp.full(m_ref.shape, -jnp.inf, F32)
    l_ref[...] = jnp.zeros(l_ref.shape, F32)
    acc_ref[...] = jnp.zeros(acc_ref.shape, F32)

    def tile(j, masked):
        scores = [jnp.dot(ka_ref[h, j], qa_ref[h], preferred_element_type=F32)
                  for h in range(nh)]
        probs, alphas = [], []
        for h in range(nh):
            sT = scores[h]
            if masked:
                sT = jnp.where(causal_ref[...] > 0.5, sT, -jnp.inf)
            m_old = m_ref[h]
            m_new = jnp.maximum(m_old, jnp.max(sT, axis=0, keepdims=True))
            alpha = jnp.exp(m_old - m_new)
            p = jnp.exp(sT - m_new)
            l_ref[h] = alpha * l_ref[h] + jnp.sum(p, axis=0, keepdims=True)
            m_ref[h] = m_new
            probs.append(p.astype(BF16))
            alphas.append(alpha)
        for h in range(nh):
            vt = vt_ref[j, h * 64:(h + 1) * 64, :]
            acc_ref[h] = alphas[h] * acc_ref[h] + jnp.dot(vt, probs[h], preferred_element_type=F32)

    def past(j, carry):
        tile(j, False)
        return carry

    lax.fori_loop(0, i, past, 0)
    tile(i, True)
    ot = jnp.concatenate([acc_ref[h] / l_ref[h] for h in range(nh)], axis=0)
    o_ref[...] = (ot.T * _silu(z_ref[...].astype(F32))).astype(o_ref.dtype)


def _moba(proj, slopes, bsz, s):
    nb = s // MOBA_BLOCK
    assert nb <= 16
    blk = MOBA_BLOCK
    aug = _moba_aug_keys(s)
    avg = ((jnp.arange(16)[:, None] == (jnp.arange(s) // blk)[None, :]).astype(F32) / blk).astype(BF16)
    causal = (jnp.arange(blk)[:, None] <= jnp.arange(blk)[None, :]).astype(F32)
    qblock = lambda c0: (lambda b, i, *_: (b * nb + i, c0 // BRANCH_W))
    whole = lambda c0: (lambda b, i, *_: (b, c0 // BRANCH_W))
    return pl.pallas_call(
        _moba_body,
        grid_spec=pltpu.PrefetchScalarGridSpec(
            num_scalar_prefetch=1,
            grid=(bsz, nb),
            in_specs=[
                pl.BlockSpec((blk, BRANCH_W), qblock(COL_QA)),
                pl.BlockSpec((s, BRANCH_W), whole(COL_KA)),
                pl.BlockSpec((s, BRANCH_W), whole(COL_VA)),
                pl.BlockSpec((blk, BRANCH_W), qblock(COL_ZA)),
                pl.BlockSpec((2, s, LANES), lambda b, i, *_: (0, 0, 0)),
                pl.BlockSpec((16, s), lambda b, i, *_: (0, 0)),
                pl.BlockSpec((blk, blk), lambda b, i, *_: (0, 0)),
            ],
            out_specs=pl.BlockSpec((blk, BRANCH_W), lambda b, i, *_: (b * nb + i, 0)),
            scratch_shapes=[
                pltpu.VMEM((MOBA_HEADS, nb, blk, LANES), BF16),
                pltpu.VMEM((nb, BRANCH_W, blk), BF16),
                pltpu.VMEM((MOBA_HEADS * 48, LANES), BF16),
                pltpu.VMEM((MOBA_HEADS, LANES, blk), BF16),
                pltpu.VMEM((MOBA_HEADS, 1, blk), F32),
                pltpu.VMEM((MOBA_HEADS, 1, blk), F32),
                pltpu.VMEM((MOBA_HEADS, MOBA_HD, blk), F32),
            ],
        ),
        out_shape=jax.ShapeDtypeStruct((bsz * s, BRANCH_W), BF16),
        compiler_params=pltpu.CompilerParams(
            dimension_semantics=("parallel", "arbitrary"), vmem_limit_bytes=VMEM_LIMIT),
        name="moba",
    )(slopes, proj, proj, proj, proj, aug, avg, causal)


HGRN_TILE = 256


def _hgrn_body(f_ref, i_ref, q_ref, g_ref, lbl_ref, nw_ref, lj_ref, tril_ref, o_ref, st_ref):
    r = HGRN_TILE
    c = HGRN_CHUNK
    n_chunks = r // c

    @pl.when(pl.program_id(1) == 0)
    def _():
        st_ref[...] = jnp.zeros_like(st_ref)

    logits = lbl_ref[...]
    ex = jnp.exp(logits - jnp.max(logits, axis=0, keepdims=True))
    lb = ex[0:1, :] / jnp.sum(ex, axis=0, keepdims=True)
    rowc = lax.broadcasted_iota(jnp.int32, (r, 1), 0) // c

    fl = f_ref[...].astype(F32)
    log_f = jnp.log(lb + (1.0 - lb) * _sigmoid(fl))
    kk = (1.0 - lb) * _sigmoid(-fl)
    hi, mid, lo = _split3(log_f)
    lj = lj_ref[...]
    bb = (jnp.dot(lj, hi.astype(BF16), preferred_element_type=F32)
          + jnp.dot(lj, mid.astype(BF16), preferred_element_type=F32)
          + jnp.dot(lj, lo.astype(BF16), preferred_element_type=F32))
    b = bb[0:r]
    b_last = bb[r:2 * r]
    q_t_all = (q_ref[...].astype(F32) * jnp.exp(b)).astype(BF16)
    k_t_all = (kk * jnp.exp(-b)).astype(BF16)
    kd_all = kk * jnp.exp(b_last - b)
    dec_all = jnp.exp(b_last)

    nt_dims = (((1,), (1,)), ((), ()))
    heads = range(HGRN_HEADS)
    cols = [slice(h * HGRN_D, (h + 1) * HGRN_D) for h in heads]
    q_t = [q_t_all[:, cols[h]] for h in heads]
    a = [lax.dot_general(q_t[h], k_t_all[:, cols[h]], nt_dims, preferred_element_type=F32) for h in heads]
    u_t = []
    for h in heads:
        vt = i_ref[:, cols[h]].astype(F32).T.astype(BF16)
        kd = kd_all[:, cols[h]]
        u_t.append([jnp.dot(vt, jnp.where(rowc == ci, kd, 0.0).astype(BF16), preferred_element_type=F32)
                    for ci in range(n_chunks)])
    o_intra = [jnp.dot((a[h] * tril_ref[...]).astype(BF16), i_ref[:, cols[h]], preferred_element_type=F32)
               for h in heads]
    states = []
    for h in heads:
        st = st_ref[h]
        dec = dec_all[:, cols[h]]
        before = []
        for ci in range(n_chunks):
            before.append(st.astype(BF16))
            st = st * dec[ci * c:ci * c + 1, :] + u_t[h][ci]
        st_ref[h] = st
        states.append(before)
    for h in heads:
        o_inter = [lax.dot_general(q_t[h][ci * c:(ci + 1) * c], states[h][ci], nt_dims,
                                   preferred_element_type=F32) for ci in range(n_chunks)]
        o = o_intra[h] + jnp.concatenate(o_inter, axis=0)
        on = o * lax.rsqrt(jnp.mean(o * o, axis=-1, keepdims=True) + EPS) * nw_ref[...]
        o_ref[:, cols[h]] = (on * _silu(g_ref[:, cols[h]].astype(F32))).astype(o_ref.dtype)


def _hgrn(proj, lb_logits, norm_w, bsz, s):
    r, c = HGRN_TILE, HGRN_CHUNK
    nt = s // r
    ri = jnp.arange(r)
    same = (ri[:, None] // c) == (ri[None, :] // c)
    tril = (same & (ri[None, :] <= ri[:, None])).astype(F32)
    lj = jnp.concatenate([tril, same.astype(F32)], axis=0).astype(BF16)
    col = lambda c0: (lambda b, t: (b * nt + t, c0 // BRANCH_W))
    const = lambda shape: pl.BlockSpec(shape, lambda b, t: (0,) * len(shape))
    return pl.pallas_call(
        _hgrn_body,
        grid=(bsz, nt),
        in_specs=[
            pl.BlockSpec((r, BRANCH_W), col(COL_FB)),
            pl.BlockSpec((r, BRANCH_W), col(COL_IB)),
            pl.BlockSpec((r, BRANCH_W), col(COL_QB)),
            pl.BlockSpec((r, BRANCH_W), col(COL_GB)),
            const((lb_logits.shape[0], BRANCH_W)),
            const((1, HGRN_D)),
            const((2 * r, r)),
            const((r, r)),
        ],
        out_specs=pl.BlockSpec((r, BRANCH_W), lambda b, t: (b * nt + t, 0)),
        out_shape=jax.ShapeDtypeStruct((bsz * s, BRANCH_W), BF16),
        scratch_shapes=[pltpu.VMEM((HGRN_HEADS, HGRN_D, HGRN_D), F32)],
        compiler_params=pltpu.CompilerParams(
            dimension_semantics=("parallel", "arbitrary"), vmem_limit_bytes=VMEM_LIMIT),
        name="hgrn",
    )(proj, proj, proj, proj, lb_logits, norm_w.reshape(1, HGRN_D), lj, tril)


def _xattn_body(q_ref, z_ref, mem_ref, mw_ref, wkv_ref, o_ref, km_ref, vm_ref):
    @pl.when(pl.program_id(1) == 0)
    def _():
        m = mem_ref[0]
        mn = m * lax.rsqrt(jnp.mean(m * m, axis=-1, keepdims=True) + EPS) * mw_ref[...]
        kv = jnp.dot(mn.astype(BF16), wkv_ref[...], preferred_element_type=F32)
        km_ref[...] = kv[:, :BRANCH_W].astype(BF16)
        vm_ref[...] = kv[:, BRANCH_W:].astype(BF16)

    for h in range(XA_HEADS):
        cols = slice(h * XA_HD, (h + 1) * XA_HD)
        logits = lax.dot_general(q_ref[:, cols], km_ref[:, cols], (((1,), (1,)), ((), ())),
                                 preferred_element_type=F32) * (XA_HD ** -0.5)
        mx = jnp.max(logits, axis=-1, keepdims=True)
        p = jnp.exp(logits - mx)
        den = jnp.sum(p, axis=-1, keepdims=True)
        o = jnp.dot(p.astype(BF16), vm_ref[:, cols], preferred_element_type=F32) / den
        o_ref[:, cols] = (o * _silu(z_ref[:, cols].astype(F32))).astype(o_ref.dtype)


def _xattn(proj, mem, mem_norm_w, wkv_bf16, bsz, s, tq=1024):
    n_mem, d = mem.shape[1], mem.shape[2]
    nq = s // tq
    return pl.pallas_call(
        _xattn_body,
        grid=(bsz, nq),
        in_specs=[
            pl.BlockSpec((tq, BRANCH_W), lambda b, i: (b * nq + i, COL_QC // BRANCH_W)),
            pl.BlockSpec((tq, BRANCH_W), lambda b, i: (b * nq + i, COL_ZC // BRANCH_W)),
            pl.BlockSpec((1, n_mem, d), lambda b, i: (b, 0, 0)),
            pl.BlockSpec((1, d), lambda b, i: (0, 0)),
            pl.BlockSpec((d, 2 * BRANCH_W), lambda b, i: (0, 0)),
        ],
        out_specs=pl.BlockSpec((tq, BRANCH_W), lambda b, i: (b * nq + i, 0)),
        out_shape=jax.ShapeDtypeStruct((bsz * s, BRANCH_W), BF16),
        scratch_shapes=[pltpu.VMEM((n_mem, BRANCH_W), BF16), pltpu.VMEM((n_mem, BRANCH_W), BF16)],
        compiler_params=pltpu.CompilerParams(
            dimension_semantics=("parallel", "arbitrary"), vmem_limit_bytes=VMEM_LIMIT),
        name="xattn",
    )(proj, proj, mem, mem_norm_w.reshape(1, d), wkv_bf16)


def _merge_body(x_ref, ga_ref, gb_ref, gc_ref, ya_ref, yb_ref, yc_ref,
                wa_ref, wb_ref, wc_ref, wo_ref, pw_ref, o_ref):
    def branch(g_ref, y_ref, w_ref):
        return _sigmoid(g_ref[...].astype(F32)) * jnp.dot(y_ref[...], w_ref[...],
                                                          preferred_element_type=F32)

    merged = branch(ga_ref, ya_ref, wa_ref) + branch(gb_ref, yb_ref, wb_ref) + branch(gc_ref, yc_ref, wc_ref)
    y = jnp.dot(merged.astype(BF16), wo_ref[...], preferred_element_type=F32)
    yn = y * lax.rsqrt(jnp.mean(y * y, axis=-1, keepdims=True) + EPS) * pw_ref[...]
    o_ref[...] = x_ref[...] + yn


def _merge(x2, proj, ya, yb, yc, wa, wb, wc, wo, post_w, tm=512):
    t, d = x2.shape
    row = lambda c: (lambda i: (i, c))
    const = lambda shape: pl.BlockSpec(shape, lambda i: (0, 0))
    return pl.pallas_call(
        _merge_body,
        grid=(t // tm,),
        in_specs=[
            pl.BlockSpec((tm, d), row(0)),
            pl.BlockSpec((tm, d), row(COL_GATE_A // d)),
            pl.BlockSpec((tm, d), row(COL_GATE_B // d)),
            pl.BlockSpec((tm, d), row(COL_GATE_C // d)),
            pl.BlockSpec((tm, BRANCH_W), row(0)),
            pl.BlockSpec((tm, BRANCH_W), row(0)),
            pl.BlockSpec((tm, BRANCH_W), row(0)),
            const((BRANCH_W, d)), const((BRANCH_W, d)), const((BRANCH_W, d)),
            const((d, d)), const((1, d)),
        ],
        out_specs=pl.BlockSpec((tm, d), row(0)),
        out_shape=jax.ShapeDtypeStruct((t, d), F32),
        compiler_params=pltpu.CompilerParams(
            dimension_semantics=("parallel",), vmem_limit_bytes=VMEM_LIMIT),
        name="merge",
    )(x2, proj, proj, proj, ya, yb, yc, wa, wb, wc, wo, post_w.reshape(1, d))


def kernel(x, mem, pre_norm_w, w_in, hgrn_lb_logits, hgrn_norm_w, mem_norm_w, w_mem_kv,
           w_branch_a, w_branch_b, w_branch_c, w_out, post_norm_w):
    bsz, s, d = x.shape
    assert w_in.shape[0] == 1 and w_in.shape[2] == PROJ_TOTAL and d == 1024
    assert s % MOBA_BLOCK == 0 and s % HGRN_TILE == 0
    x2 = x.reshape(bsz * s, d)
    proj = _in_proj(x2, pre_norm_w[0], w_in[0].astype(BF16))
    slopes = jnp.exp2(-8.0 * jnp.arange(1, MOBA_HEADS + 1, dtype=F32) / MOBA_HEADS)
    ya = _moba(proj, slopes, bsz, s)
    yb = _hgrn(proj, hgrn_lb_logits, hgrn_norm_w[0], bsz, s)
    yc = _xattn(proj, mem, mem_norm_w[0], w_mem_kv[0].astype(BF16), bsz, s)
    out = _merge(x2, proj, ya, yb, yc, w_branch_a[0].astype(BF16), w_branch_b[0].astype(BF16),
                 w_branch_c[0].astype(BF16), w_out[0].astype(BF16), post_norm_w[0])
    return out.reshape(bsz, s, d)
```

```python
import jax
import jax.numpy as jnp
from jax import lax
from jax.experimental import pallas as pl
from jax.experimental.pallas import tpu as pltpu

F32 = jnp.float32
BF16 = jnp.bfloat16
EPS = 1e-6

MOBA_HEADS, MOBA_HD, MOBA_BLOCK, MOBA_TOPK = 8, 64, 256, 3
HGRN_HEADS, HGRN_D, HGRN_CHUNK = 4, 128, 32
XA_HEADS, XA_HD = 4, 128
BRANCH_W = 512
LANES = 128
COL_QA, COL_KA, COL_VA, COL_ZA = 0, 512, 1024, 1536
COL_FB, COL_IB, COL_QB, COL_GB = 2048, 2560, 3072, 3584
COL_QC, COL_ZC = 4096, 4608
COL_GATE_A, COL_GATE_B, COL_GATE_C = 5120, 6144, 7168
PROJ_TOTAL = COL_GATE_C + 1024

NEG_BIG = -1e30
LOG2E = 1.4426950408889634
MOBA_VROWS = MOBA_HD + 16
VMEM_LIMIT = 56 * 1024 * 1024


def _sigmoid(z):
    return 1.0 / (1.0 + jnp.exp(-z))


def _silu(z):
    return z * _sigmoid(z)


def _split3(a):
    hi = a.astype(BF16).astype(F32)
    r = a - hi
    mid = r.astype(BF16).astype(F32)
    lo = (r - mid).astype(BF16).astype(F32)
    return hi, mid, lo


def _in_proj_body(x_ref, nw_ref, w_ref, o_ref, h_ref):
    @pl.when(pl.program_id(1) == 0)
    def _():
        x = x_ref[...]
        ms = jnp.mean(x * x, axis=-1, keepdims=True)
        h_ref[...] = (x * lax.rsqrt(ms + EPS) * nw_ref[...]).astype(BF16)

    o_ref[...] = jnp.dot(h_ref[...], w_ref[...], preferred_element_type=F32).astype(o_ref.dtype)


def _in_proj(x2, norm_w, w_bf16, tm=1024, tn=2048):
    t, d = x2.shape
    n = w_bf16.shape[1]
    return pl.pallas_call(
        _in_proj_body,
        grid=(t // tm, n // tn),
        in_specs=[
            pl.BlockSpec((tm, d), lambda i, j: (i, 0)),
            pl.BlockSpec((1, d), lambda i, j: (0, 0)),
            pl.BlockSpec((d, tn), lambda i, j: (0, j)),
        ],
        out_specs=pl.BlockSpec((tm, tn), lambda i, j: (i, j)),
        out_shape=jax.ShapeDtypeStruct((t, n), BF16),
        scratch_shapes=[pltpu.VMEM((tm, d), BF16)],
        compiler_params=pltpu.CompilerParams(
            dimension_semantics=("parallel", "arbitrary"), vmem_limit_bytes=VMEM_LIMIT),
        name="in_proj",
    )(x2, norm_w.reshape(1, d), w_bf16)


def _moba_aug_keys(s, nb_lanes=16):
    row = jnp.arange(s)
    blk = row // MOBA_BLOCK
    off = (row % MOBA_BLOCK).astype(F32)
    onehot = (blk[:, None] == jnp.arange(nb_lanes)[None, :]).astype(F32)
    bias_lanes = jnp.concatenate([onehot, onehot, onehot, off[:, None], off[:, None], off[:, None],
                                  jnp.zeros((s, 64 - 3 * nb_lanes - 3), F32)], axis=1)
    zeros = jnp.zeros((s, 64), F32)
    a0 = jnp.concatenate([zeros, bias_lanes], axis=1)
    a1 = jnp.concatenate([bias_lanes, zeros], axis=1)
    return jnp.stack([a0, a1]).astype(BF16)


def _moba_body(slopes_ref, q_ref, k_ref, v_ref, z_ref, aug_ref, avg_ref, causal_ref, o_ref,
               ka_ref, vt_ref, kbp_ref, qa_ref, m_ref, acc_ref, p_ref, al_ref):
    s = k_ref.shape[0]
    nb = s // MOBA_BLOCK
    blk = MOBA_BLOCK
    nh = MOBA_HEADS
    hd = MOBA_HD
    i = pl.program_id(1)

    @pl.when(i == 0)
    def _():
        lane = lax.broadcasted_iota(jnp.int32, (1, LANES), 1)
        first_half = lane < 64
        kbar = jnp.dot(avg_ref[...], k_ref[...], preferred_element_type=F32)
        for h in range(nh):
            hp, e = h // 2, h % 2
            cols = slice(hp * LANES, (hp + 1) * LANES)
            keep = first_half if e == 0 else jnp.logical_not(first_half)
            for j in range(nb):
                rows = slice(j * blk, (j + 1) * blk)
                ka_ref[h, j] = jnp.where(keep, k_ref[rows, cols], aug_ref[e, rows, :])
            pieces = _split3(jnp.where(keep, kbar[:, cols], 0.0))
            for p in range(3):
                kbp_ref[(h * 3 + p) * 16:(h * 3 + p + 1) * 16, :] = pieces[p].astype(BF16)
        ones_row = jnp.where(lax.broadcasted_iota(jnp.int32, (MOBA_VROWS - hd, blk), 0) == 0, 1.0, 0.0)
        for j in range(nb):
            vt = v_ref[j * blk:(j + 1) * blk, :].astype(F32).T
            for h in range(nh):
                vt_ref[j, h, 0:hd, :] = vt[h * hd:(h + 1) * hd].astype(BF16)
                vt_ref[j, h, hd:MOBA_VROWS, :] = ones_row.astype(BF16)

    nidx = lax.broadcasted_iota(jnp.int32, (16, blk), 0)
    qoff = lax.broadcasted_iota(jnp.int32, (16, blk), 1)
    dist0 = (qoff + (i - nidx) * blk).astype(F32)
    for hp in range(nh // 2):
        cols = slice(hp * LANES, (hp + 1) * LANES)
        qt = (q_ref[:, cols].astype(F32) * (hd ** -0.5)).T
        gates = jnp.dot(kbp_ref[hp * 96:(hp + 1) * 96, :], qt.astype(BF16),
                        preferred_element_type=F32)
        qs = qt * LOG2E
        for e in range(2):
            h = 2 * hp + e
            g = gates[e * 48:e * 48 + 16] + gates[e * 48 + 16:e * 48 + 32] + gates[e * 48 + 32:e * 48 + 48]
            cnt = jnp.zeros((16, blk), jnp.int32)
            for m in range(nb):
                gm = g[m:m + 1, :]
                ahead = (gm > g) | ((gm == g) & (m < nidx))
                cnt = cnt + jnp.where(ahead, jnp.where(m < i, 1, 0), 0)
            sel = ((nidx < i) & (cnt < MOBA_TOPK)) | (nidx == i)
            sl = slopes_ref[h] * LOG2E
            bias = jnp.where(sel, -sl * dist0, NEG_BIG)
            b_hi, b_mid, b_lo = _split3(bias)
            s_hi, s_mid, s_lo = _split3(jnp.full((16, blk), sl, F32))
            srow = jnp.where(nidx == 0, s_hi, jnp.where(nidx == 1, s_mid, jnp.where(nidx == 2, s_lo, 0.0)))
            qh = qs[0:64] if e == 0 else qs[64:128]
            parts = [qh, b_hi, b_mid, b_lo, srow] if e == 0 else [b_hi, b_mid, b_lo, srow, qh]
            qa_ref[h] = jnp.concatenate(parts, axis=0).astype(BF16)

    m_ref[...] = jnp.full(m_ref.shape, -jnp.inf, F32)
    acc_ref[...] = jnp.zeros(acc_ref.shape, F32)
    p_ref[...] = jnp.zeros(p_ref.shape, BF16)
    al_ref[...] = jnp.ones(al_ref.shape, F32)

    def step(j, masked):
        jm = jnp.maximum(j - 1, 0)
        rd = (j + 1) % 2
        wr = j % 2
        scores = [jnp.dot(ka_ref[h, j], qa_ref[h], preferred_element_type=F32)
                  for h in range(nh)]
        pv = [jnp.dot(vt_ref[jm, h], p_ref[rd, h], preferred_element_type=F32) for h in range(nh)]
        probs, alphas = [], []
        for h in range(nh):
            sT = scores[h]
            if masked:
                sT = jnp.where(causal_ref[...] > 0.5, sT, -jnp.inf)
            m_old = m_ref[h]
            m_new = jnp.maximum(m_old, jnp.max(sT, axis=0, keepdims=True))
            alphas.append(jnp.exp2(m_old - m_new))
            probs.append(jnp.exp2(sT - m_new).astype(BF16))
            m_ref[h] = m_new
        for h in range(nh):
            acc_ref[h] = al_ref[rd, h] * acc_ref[h] + pv[h]
        for h in range(nh):
            p_ref[wr, h] = probs[h]
            al_ref[wr, h] = alphas[h]

    def past(j, carry):
        step(j, False)
        return carry

    lax.fori_loop(0, i, past, 0)
    step(i, True)
    last = i % 2
    pv = [jnp.dot(vt_ref[i, h], p_ref[last, h], preferred_element_type=F32) for h in range(nh)]
    outs = []
    for h in range(nh):
        acc = al_ref[last, h] * acc_ref[h] + pv[h]
        outs.append(acc[0:hd] / acc[hd:hd + 1])
    ot = jnp.concatenate(outs, axis=0)
    o_ref[...] = (ot.T * _silu(z_ref[...].astype(F32))).astype(o_ref.dtype)


def _moba(proj, slopes, bsz, s):
    nb = s // MOBA_BLOCK
    assert nb <= 16
    blk = MOBA_BLOCK
    aug = _moba_aug_keys(s)
    avg = ((jnp.arange(16)[:, None] == (jnp.arange(s) // blk)[None, :]).astype(F32) / blk).astype(BF16)
    causal = (jnp.arange(blk)[:, None] <= jnp.arange(blk)[None, :]).astype(F32)
    qblock = lambda c0: (lambda b, i, *_: (b * nb + i, c0 // BRANCH_W))
    whole = lambda c0: (lambda b, i, *_: (b, c0 // BRANCH_W))
    return pl.pallas_call(
        _moba_body,
        grid_spec=pltpu.PrefetchScalarGridSpec(
            num_scalar_prefetch=1,
            grid=(bsz, nb),
            in_specs=[
                pl.BlockSpec((blk, BRANCH_W), qblock(COL_QA)),
                pl.BlockSpec((s, BRANCH_W), whole(COL_KA)),
                pl.BlockSpec((s, BRANCH_W), whole(COL_VA)),
                pl.BlockSpec((blk, BRANCH_W), qblock(COL_ZA)),
                pl.BlockSpec((2, s, LANES), lambda b, i, *_: (0, 0, 0)),
                pl.BlockSpec((16, s), lambda b, i, *_: (0, 0)),
                pl.BlockSpec((blk, blk), lambda b, i, *_: (0, 0)),
            ],
            out_specs=pl.BlockSpec((blk, BRANCH_W), lambda b, i, *_: (b * nb + i, 0)),
            scratch_shapes=[
                pltpu.VMEM((MOBA_HEADS, nb, blk, LANES), BF16),
                pltpu.VMEM((nb, MOBA_HEADS, MOBA_VROWS, blk), BF16),
                pltpu.VMEM((MOBA_HEADS * 48, LANES), BF16),
                pltpu.VMEM((MOBA_HEADS, LANES, blk), BF16),
                pltpu.VMEM((MOBA_HEADS, 1, blk), F32),
                pltpu.VMEM((MOBA_HEADS, MOBA_VROWS, blk), F32),
                pltpu.VMEM((2, MOBA_HEADS, blk, blk), BF16),
                pltpu.VMEM((2, MOBA_HEADS, 1, blk), F32),
            ],
        ),
        out_shape=jax.ShapeDtypeStruct((bsz * s, BRANCH_W), BF16),
        compiler_params=pltpu.CompilerParams(
            dimension_semantics=("parallel", "arbitrary"), vmem_limit_bytes=VMEM_LIMIT),
        name="moba",
    )(slopes, proj, proj, proj, proj, aug, avg, causal)


HGRN_TILE = 256


def _hgrn_body(f_ref, i_ref, q_ref, g_ref, lbl_ref, nw_ref, lj_ref, tril_ref, o_ref, st_ref):
    r = HGRN_TILE
    c = HGRN_CHUNK
    n_chunks = r // c

    @pl.when(pl.program_id(1) == 0)
    def _():
        st_ref[...] = jnp.zeros_like(st_ref)

    logits = lbl_ref[...]
    ex = jnp.exp(logits - jnp.max(logits, axis=0, keepdims=True))
    lb = ex[0:1, :] / jnp.sum(ex, axis=0, keepdims=True)
    rowc = lax.broadcasted_iota(jnp.int32, (r, 1), 0) // c

    fl = f_ref[...].astype(F32)
    log_f = jnp.log(lb + (1.0 - lb) * _sigmoid(fl))
    kk = (1.0 - lb) * _sigmoid(-fl)
    hi, mid, lo = _split3(log_f)
    lj = lj_ref[...]
    bb = (jnp.dot(lj, hi.astype(BF16), preferred_element_type=F32)
          + jnp.dot(lj, mid.astype(BF16), preferred_element_type=F32)
          + jnp.dot(lj, lo.astype(BF16), preferred_element_type=F32))
    b = bb[0:r]
    b_last = bb[r:2 * r]
    q_t_all = (q_ref[...].astype(F32) * jnp.exp(b)).astype(BF16)
    k_t_all = (kk * jnp.exp(-b)).astype(BF16)
    kd_all = kk * jnp.exp(b_last - b)
    dec_all = jnp.exp(b_last)

    nt_dims = (((1,), (1,)), ((), ()))
    heads = range(HGRN_HEADS)
    cols = [slice(h * HGRN_D, (h + 1) * HGRN_D) for h in heads]
    q_t = [q_t_all[:, cols[h]] for h in heads]
    a = [lax.dot_general(q_t[h], k_t_all[:, cols[h]], nt_dims, preferred_element_type=F32) for h in heads]
    u_t = []
    for h in heads:
        vt = i_ref[:, cols[h]].astype(F32).T.astype(BF16)
        kd = kd_all[:, cols[h]]
        u_t.append([jnp.dot(vt, jnp.where(rowc == ci, kd, 0.0).astype(BF16), preferred_element_type=F32)
                    for ci in range(n_chunks)])
    o_intra = [jnp.dot((a[h] * tril_ref[...]).astype(BF16), i_ref[:, cols[h]], preferred_element_type=F32)
               for h in heads]
    states = []
    for h in heads:
        st = st_ref[h]
        dec = dec_all[:, cols[h]]
        before = []
        for ci in range(n_chunks):
            before.append(st.astype(BF16))
            st = st * dec[ci * c:ci * c + 1, :] + u_t[h][ci]
        st_ref[h] = st
        states.append(before)
    for h in heads:
        o_inter = [lax.dot_general(q_t[h][ci * c:(ci + 1) * c], states[h][ci], nt_dims,
                                   preferred_element_type=F32) for ci in range(n_chunks)]
        o = o_intra[h] + jnp.concatenate(o_inter, axis=0)
        on = o * lax.rsqrt(jnp.mean(o * o, axis=-1, keepdims=True) + EPS) * nw_ref[...]
        o_ref[:, cols[h]] = (on * _silu(g_ref[:, cols[h]].astype(F32))).astype(o_ref.dtype)


def _hgrn(proj, lb_logits, norm_w, bsz, s):
    r, c = HGRN_TILE, HGRN_CHUNK
    nt = s // r
    ri = jnp.arange(r)
    same = (ri[:, None] // c) == (ri[None, :] // c)
    tril = (same & (ri[None, :] <= ri[:, None])).astype(F32)
    lj = jnp.concatenate([tril, same.astype(F32)], axis=0).astype(BF16)
    col = lambda c0: (lambda b, t: (b * nt + t, c0 // BRANCH_W))
    const = lambda shape: pl.BlockSpec(shape, lambda b, t: (0,) * len(shape))
    return pl.pallas_call(
        _hgrn_body,
        grid=(bsz, nt),
        in_specs=[
            pl.BlockSpec((r, BRANCH_W), col(COL_FB)),
            pl.BlockSpec((r, BRANCH_W), col(COL_IB)),
            pl.BlockSpec((r, BRANCH_W), col(COL_QB)),
            pl.BlockSpec((r, BRANCH_W), col(COL_GB)),
            const((lb_logits.shape[0], BRANCH_W)),
            const((1, HGRN_D)),
            const((2 * r, r)),
            const((r, r)),
        ],
        out_specs=pl.BlockSpec((r, BRANCH_W), lambda b, t: (b * nt + t, 0)),
        out_shape=jax.ShapeDtypeStruct((bsz * s, BRANCH_W), BF16),
        scratch_shapes=[pltpu.VMEM((HGRN_HEADS, HGRN_D, HGRN_D), F32)],
        compiler_params=pltpu.CompilerParams(
            dimension_semantics=("parallel", "arbitrary"), vmem_limit_bytes=VMEM_LIMIT),
        name="hgrn",
    )(proj, proj, proj, proj, lb_logits, norm_w.reshape(1, HGRN_D), lj, tril)


def _xattn_body(q_ref, z_ref, mem_ref, mw_ref, wkv_ref, o_ref, km_ref, vm_ref):
    @pl.when(pl.program_id(1) == 0)
    def _():
        m = mem_ref[0]
        mn = m * lax.rsqrt(jnp.mean(m * m, axis=-1, keepdims=True) + EPS) * mw_ref[...]
        kv = jnp.dot(mn.astype(BF16), wkv_ref[...], preferred_element_type=F32)
        km_ref[...] = kv[:, :BRANCH_W].astype(BF16)
        vm_ref[...] = kv[:, BRANCH_W:].astype(BF16)

    for h in range(XA_HEADS):
        cols = slice(h * XA_HD, (h + 1) * XA_HD)
        logits = lax.dot_general(q_ref[:, cols], km_ref[:, cols], (((1,), (1,)), ((), ())),
                                 preferred_element_type=F32) * (XA_HD ** -0.5)
        mx = jnp.max(logits, axis=-1, keepdims=True)
        p = jnp.exp(logits - mx)
        den = jnp.sum(p, axis=-1, keepdims=True)
        o = jnp.dot(p.astype(BF16), vm_ref[:, cols], preferred_element_type=F32) / den
        o_ref[:, cols] = (o * _silu(z_ref[:, cols].astype(F32))).astype(o_ref.dtype)


def _xattn(proj, mem, mem_norm_w, wkv_bf16, bsz, s, tq=1024):
    n_mem, d = mem.shape[1], mem.shape[2]
    nq = s // tq
    return pl.pallas_call(
        _xattn_body,
        grid=(bsz, nq),
        in_specs=[
            pl.BlockSpec((tq, BRANCH_W), lambda b, i: (b * nq + i, COL_QC // BRANCH_W)),
            pl.BlockSpec((tq, BRANCH_W), lambda b, i: (b * nq + i, COL_ZC // BRANCH_W)),
            pl.BlockSpec((1, n_mem, d), lambda b, i: (b, 0, 0)),
            pl.BlockSpec((1, d), lambda b, i: (0, 0)),
            pl.BlockSpec((d, 2 * BRANCH_W), lambda b, i: (0, 0)),
        ],
        out_specs=pl.BlockSpec((tq, BRANCH_W), lambda b, i: (b * nq + i, 0)),
        out_shape=jax.ShapeDtypeStruct((bsz * s, BRANCH_W), BF16),
        scratch_shapes=[pltpu.VMEM((n_mem, BRANCH_W), BF16), pltpu.VMEM((n_mem, BRANCH_W), BF16)],
        compiler_params=pltpu.CompilerParams(
            dimension_semantics=("parallel", "arbitrary"), vmem_limit_bytes=VMEM_LIMIT),
        name="xattn",
    )(proj, proj, mem, mem_norm_w.reshape(1, d), wkv_bf16)


def _merge_body(x_ref, ga_ref, gb_ref, gc_ref, ya_ref, yb_ref, yc_ref,
                wa_ref, wb_ref, wc_ref, wo_ref, pw_ref, o_ref):
    def branch(g_ref, y_ref, w_ref):
        return _sigmoid(g_ref[...].astype(F32)) * jnp.dot(y_ref[...], w_ref[...],
                                                          preferred_element_type=F32)

    merged = branch(ga_ref, ya_ref, wa_ref) + branch(gb_ref, yb_ref, wb_ref) + branch(gc_ref, yc_ref, wc_ref)
    y = jnp.dot(merged.astype(BF16), wo_ref[...], preferred_element_type=F32)
    yn = y * lax.rsqrt(jnp.mean(y * y, axis=-1, keepdims=True) + EPS) * pw_ref[...]
    o_ref[...] = x_ref[...] + yn


def _merge(x2, proj, ya, yb, yc, wa, wb, wc, wo, post_w, tm=1024):
    t, d = x2.shape
    row = lambda c: (lambda i: (i, c))
    const = lambda shape: pl.BlockSpec(shape, lambda i: (0, 0))
    return pl.pallas_call(
        _merge_body,
        grid=(t // tm,),
        in_specs=[
            pl.BlockSpec((tm, d), row(0)),
            pl.BlockSpec((tm, d), row(COL_GATE_A // d)),
            pl.BlockSpec((tm, d), row(COL_GATE_B // d)),
            pl.BlockSpec((tm, d), row(COL_GATE_C // d)),
            pl.BlockSpec((tm, BRANCH_W), row(0)),
            pl.BlockSpec((tm, BRANCH_W), row(0)),
            pl.BlockSpec((tm, BRANCH_W), row(0)),
            const((BRANCH_W, d)), const((BRANCH_W, d)), const((BRANCH_W, d)),
            const((d, d)), const((1, d)),
        ],
        out_specs=pl.BlockSpec((tm, d), row(0)),
        out_shape=jax.ShapeDtypeStruct((t, d), F32),
        compiler_params=pltpu.CompilerParams(
            dimension_semantics=("parallel",), vmem_limit_bytes=VMEM_LIMIT),
        name="merge",
    )(x2, proj, proj, proj, ya, yb, yc, wa, wb, wc, wo, post_w.reshape(1, d))


def kernel(x, mem, pre_norm_w, w_in, hgrn_lb_logits, hgrn_norm_w, mem_norm_w, w_mem_kv,
           w_branch_a, w_branch_b, w_branch_c, w_out, post_norm_w):
    bsz, s, d = x.shape
    assert w_in.shape[0] == 1 and w_in.shape[2] == PROJ_TOTAL and d == 1024
    assert s % MOBA_BLOCK == 0 and s % HGRN_TILE == 0
    x2 = x.reshape(bsz * s, d)
    proj = _in_proj(x2, pre_norm_w[0], w_in[0].astype(BF16))
    slopes = jnp.exp2(-8.0 * jnp.arange(1, MOBA_HEADS + 1, dtype=F32) / MOBA_HEADS)
    ya = _moba(proj, slopes, bsz, s)
    yb = _hgrn(proj, hgrn_lb_logits, hgrn_norm_w[0], bsz, s)
    yc = _xattn(proj, mem, mem_norm_w[0], w_mem_kv[0].astype(BF16), bsz, s)
    out = _merge(x2, proj, ya, yb, yc, w_branch_a[0].astype(BF16), w_branch_b[0].astype(BF16),
                 w_branch_c[0].astype(BF16), w_out[0].astype(BF16), post_norm_w[0])
    return out.reshape(bsz, s, d)
```

```python
import jax
import jax.numpy as jnp
from jax import lax
from jax.experimental import pallas as pl
from jax.experimental.pallas import tpu as pltpu

F32 = jnp.float32
BF16 = jnp.bfloat16
EPS = 1e-6

MOBA_HEADS, MOBA_HD, MOBA_BLOCK, MOBA_TOPK = 8, 64, 256, 3
HGRN_HEADS, HGRN_D, HGRN_CHUNK = 4, 128, 32
XA_HEADS, XA_HD = 4, 128
BRANCH_W = 512
LANES = 128
COL_QA, COL_KA, COL_VA, COL_ZA = 0, 512, 1024, 1536
COL_FB, COL_IB, COL_QB, COL_GB = 2048, 2560, 3072, 3584
COL_QC, COL_ZC = 4096, 4608
COL_GATE_A, COL_GATE_B, COL_GATE_C = 5120, 6144, 7168
PROJ_TOTAL = COL_GATE_C + 1024

NEG_BIG = -1e30
LOG2E = 1.4426950408889634
MOBA_VROWS = MOBA_HD + 16
VMEM_LIMIT = 56 * 1024 * 1024


def _sigmoid(z):
    return 1.0 / (1.0 + jnp.exp(-z))


def _silu(z):
    return z * _sigmoid(z)


def _split3(a):
    hi = a.astype(BF16).astype(F32)
    r = a - hi
    mid = r.astype(BF16).astype(F32)
    lo = (r - mid).astype(BF16).astype(F32)
    return hi, mid, lo


def _in_proj_body(x_ref, nw_ref, w_ref, o_ref, h_ref):
    @pl.when(pl.program_id(1) == 0)
    def _():
        x = x_ref[...]
        ms = jnp.mean(x * x, axis=-1, keepdims=True)
        h_ref[...] = (x * lax.rsqrt(ms + EPS) * nw_ref[...]).astype(BF16)

    o_ref[...] = jnp.dot(h_ref[...], w_ref[...], preferred_element_type=F32).astype(o_ref.dtype)


def _in_proj(x2, norm_w, w_bf16, tm=1024, tn=2048):
    t, d = x2.shape
    n = w_bf16.shape[1]
    return pl.pallas_call(
        _in_proj_body,
        grid=(t // tm, n // tn),
        in_specs=[
            pl.BlockSpec((tm, d), lambda i, j: (i, 0)),
            pl.BlockSpec((1, d), lambda i, j: (0, 0)),
            pl.BlockSpec((d, tn), lambda i, j: (0, j)),
        ],
        out_specs=pl.BlockSpec((tm, tn), lambda i, j: (i, j)),
        out_shape=jax.ShapeDtypeStruct((t, n), BF16),
        scratch_shapes=[pltpu.VMEM((tm, d), BF16)],
        compiler_params=pltpu.CompilerParams(
            dimension_semantics=("parallel", "arbitrary"), vmem_limit_bytes=VMEM_LIMIT),
        name="in_proj",
    )(x2, norm_w.reshape(1, d), w_bf16)


def _moba_aug_keys(s, nb_lanes=16):
    row = jnp.arange(s)
    blk = row // MOBA_BLOCK
    off = (row % MOBA_BLOCK).astype(F32)
    onehot = (blk[:, None] == jnp.arange(nb_lanes)[None, :]).astype(F32)
    bias_lanes = jnp.concatenate([onehot, onehot, onehot, off[:, None], off[:, None], off[:, None],
                                  jnp.zeros((s, 64 - 3 * nb_lanes - 3), F32)], axis=1)
    zeros = jnp.zeros((s, 64), F32)
    a0 = jnp.concatenate([zeros, bias_lanes], axis=1)
    a1 = jnp.concatenate([bias_lanes, zeros], axis=1)
    return jnp.stack([a0, a1]).astype(BF16)


def _moba_body(slopes_ref, q_ref, k_ref, v_ref, z_ref, aug_ref, avg_ref, causal_ref, o_ref,
               ka_ref, vt_ref, kbp_ref, qa_ref, m_ref, acc_ref, p_ref, al_ref):
    s = k_ref.shape[0]
    nb = s // MOBA_BLOCK
    blk = MOBA_BLOCK
    nh = MOBA_HEADS
    hd = MOBA_HD
    i = pl.program_id(1)

    @pl.when(i == 0)
    def _():
        lane = lax.broadcasted_iota(jnp.int32, (1, LANES), 1)
        first_half = lane < 64
        kbar = jnp.dot(avg_ref[...], k_ref[...], preferred_element_type=F32)
        for h in range(nh):
            hp, e = h // 2, h % 2
            cols = slice(hp * LANES, (hp + 1) * LANES)
            keep = first_half if e == 0 else jnp.logical_not(first_half)
            for j in range(nb):
                rows = slice(j * blk, (j + 1) * blk)
                ka_ref[h, j] = jnp.where(keep, k_ref[rows, cols], aug_ref[e, rows, :])
            pieces = _split3(jnp.where(keep, kbar[:, cols], 0.0))
            for p in range(3):
                kbp_ref[(h * 3 + p) * 16:(h * 3 + p + 1) * 16, :] = pieces[p].astype(BF16)
        ones_row = jnp.where(lax.broadcasted_iota(jnp.int32, (MOBA_VROWS - hd, blk), 0) == 0, 1.0, 0.0)
        for j in range(nb):
            vt = v_ref[j * blk:(j + 1) * blk, :].astype(F32).T
            for h in range(nh):
                vt_ref[j, h, 0:hd, :] = vt[h * hd:(h + 1) * hd].astype(BF16)
                vt_ref[j, h, hd:MOBA_VROWS, :] = ones_row.astype(BF16)

    nidx = lax.broadcasted_iota(jnp.int32, (16, blk), 0)
    qoff = lax.broadcasted_iota(jnp.int32, (16, blk), 1)
    dist0 = (qoff + (i - nidx) * blk).astype(F32)
    for hp in range(nh // 2):
        cols = slice(hp * LANES, (hp + 1) * LANES)
        qt = (q_ref[:, cols].astype(F32) * (hd ** -0.5)).T
        gates = jnp.dot(kbp_ref[hp * 96:(hp + 1) * 96, :], qt.astype(BF16),
                        preferred_element_type=F32)
        qs = qt * LOG2E
        for e in range(2):
            h = 2 * hp + e
            g = gates[e * 48:e * 48 + 16] + gates[e * 48 + 16:e * 48 + 32] + gates[e * 48 + 32:e * 48 + 48]
            cnt = jnp.zeros((16, blk), jnp.int32)
            for m in range(nb):
                gm = g[m:m + 1, :]
                ahead = (gm > g) | ((gm == g) & (m < nidx))
                cnt = cnt + jnp.where(ahead, jnp.where(m < i, 1, 0), 0)
            sel = ((nidx < i) & (cnt < MOBA_TOPK)) | (nidx == i)
            sl = slopes_ref[h] * LOG2E
            bias = jnp.where(sel, -sl * dist0, NEG_BIG)
            b_hi, b_mid, b_lo = _split3(bias)
            s_hi, s_mid, s_lo = _split3(jnp.full((16, blk), sl, F32))
            srow = jnp.where(nidx == 0, s_hi, jnp.where(nidx == 1, s_mid, jnp.where(nidx == 2, s_lo, 0.0)))
            qh = qs[0:64] if e == 0 else qs[64:128]
            parts = [qh, b_hi, b_mid, b_lo, srow] if e == 0 else [b_hi, b_mid, b_lo, srow, qh]
            qa_ref[h] = jnp.concatenate(parts, axis=0).astype(BF16)

    m_ref[...] = jnp.full(m_ref.shape, -jnp.inf, F32)
    acc_ref[...] = jnp.zeros(acc_ref.shape, F32)
    p_ref[...] = jnp.zeros(p_ref.shape, BF16)
    al_ref[...] = jnp.ones(al_ref.shape, F32)

    def step(j, masked):
        jm = jnp.maximum(j - 1, 0)
        scores = [jnp.dot(ka_ref[h, j], qa_ref[h], preferred_element_type=F32)
                  for h in range(nh)]
        pv = [jnp.dot(vt_ref[jm, h], p_ref[h], preferred_element_type=F32) for h in range(nh)]
        probs, alphas = [], []
        for h in range(nh):
            sT = scores[h]
            if masked:
                sT = jnp.where(causal_ref[...] > 0.5, sT, -jnp.inf)
            m_old = m_ref[h]
            m_new = jnp.maximum(m_old, jnp.max(sT, axis=0, keepdims=True))
            alphas.append(jnp.exp2(m_old - m_new))
            probs.append(jnp.exp2(sT - m_new).astype(BF16))
            m_ref[h] = m_new
        for h in range(nh):
            acc_ref[h] = al_ref[h] * acc_ref[h] + pv[h]
        return probs, alphas

    def past(j, carry):
        probs, alphas = step(j, False)
        for h in range(nh):
            p_ref[h] = probs[h]
            al_ref[h] = alphas[h]
        return carry

    lax.fori_loop(0, i, past, 0)
    probs, alphas = step(i, True)
    pv = [jnp.dot(vt_ref[i, h], probs[h], preferred_element_type=F32) for h in range(nh)]
    outs = []
    for h in range(nh):
        acc = alphas[h] * acc_ref[h] + pv[h]
        outs.append(acc[0:hd] / acc[hd:hd + 1])
    ot = jnp.concatenate(outs, axis=0)
    o_ref[...] = (ot.T * _silu(z_ref[...].astype(F32))).astype(o_ref.dtype)


def _moba(proj, slopes, bsz, s):
    nb = s // MOBA_BLOCK
    assert nb <= 16
    blk = MOBA_BLOCK
    aug = _moba_aug_keys(s)
    avg = ((jnp.arange(16)[:, None] == (jnp.arange(s) // blk)[None, :]).astype(F32) / blk).astype(BF16)
    causal = (jnp.arange(blk)[:, None] <= jnp.arange(blk)[None, :]).astype(F32)
    qblock = lambda c0: (lambda b, i, *_: (b * nb + i, c0 // BRANCH_W))
    whole = lambda c0: (lambda b, i, *_: (b, c0 // BRANCH_W))
    return pl.pallas_call(
        _moba_body,
        grid_spec=pltpu.PrefetchScalarGridSpec(
            num_scalar_prefetch=1,
            grid=(bsz, nb),
            in_specs=[
                pl.BlockSpec((blk, BRANCH_W), qblock(COL_QA)),
                pl.BlockSpec((s, BRANCH_W), whole(COL_KA)),
                pl.BlockSpec((s, BRANCH_W), whole(COL_VA)),
                pl.BlockSpec((blk, BRANCH_W), qblock(COL_ZA)),
                pl.BlockSpec((2, s, LANES), lambda b, i, *_: (0, 0, 0)),
                pl.BlockSpec((16, s), lambda b, i, *_: (0, 0)),
                pl.BlockSpec((blk, blk), lambda b, i, *_: (0, 0)),
            ],
            out_specs=pl.BlockSpec((blk, BRANCH_W), lambda b, i, *_: (b * nb + i, 0)),
            scratch_shapes=[
                pltpu.VMEM((MOBA_HEADS, nb, blk, LANES), BF16),
                pltpu.VMEM((nb, MOBA_HEADS, MOBA_VROWS, blk), BF16),
                pltpu.VMEM((MOBA_HEADS * 48, LANES), BF16),
                pltpu.VMEM((MOBA_HEADS, LANES, blk), BF16),
                pltpu.VMEM((MOBA_HEADS, 1, blk), F32),
                pltpu.VMEM((MOBA_HEADS, MOBA_VROWS, blk), F32),
                pltpu.VMEM((MOBA_HEADS, blk, blk), BF16),
                pltpu.VMEM((MOBA_HEADS, 1, blk), F32),
            ],
        ),
        out_shape=jax.ShapeDtypeStruct((bsz * s, BRANCH_W), BF16),
        compiler_params=pltpu.CompilerParams(
            dimension_semantics=("parallel", "arbitrary"), vmem_limit_bytes=VMEM_LIMIT),
        name="moba",
    )(slopes, proj, proj, proj, proj, aug, avg, causal)


HGRN_TILE = 256


def _hgrn_body(f_ref, i_ref, q_ref, g_ref, lbl_ref, nw_ref, lj_ref, tril_ref, o_ref, st_ref):
    r = HGRN_TILE
    c = HGRN_CHUNK
    n_chunks = r // c

    @pl.when(pl.program_id(1) == 0)
    def _():
        st_ref[...] = jnp.zeros_like(st_ref)

    logits = lbl_ref[...]
    ex = jnp.exp(logits - jnp.max(logits, axis=0, keepdims=True))
    lb = ex[0:1, :] / jnp.sum(ex, axis=0, keepdims=True)
    rowc = lax.broadcasted_iota(jnp.int32, (r, 1), 0) // c

    fl = f_ref[...].astype(F32)
    log_f = jnp.log(lb + (1.0 - lb) * _sigmoid(fl))
    kk = (1.0 - lb) * _sigmoid(-fl)
    hi, mid, lo = _split3(log_f)
    lj = lj_ref[...]
    bb = (jnp.dot(lj, hi.astype(BF16), preferred_element_type=F32)
          + jnp.dot(lj, mid.astype(BF16), preferred_element_type=F32)
          + jnp.dot(lj, lo.astype(BF16), preferred_element_type=F32))
    b = bb[0:r]
    b_last = bb[r:2 * r]
    q_t_all = (q_ref[...].astype(F32) * jnp.exp(b)).astype(BF16)
    k_t_all = (kk * jnp.exp(-b)).astype(BF16)
    kd_all = kk * jnp.exp(b_last - b)
    dec_all = jnp.exp(b_last)

    nt_dims = (((1,), (1,)), ((), ()))
    heads = range(HGRN_HEADS)
    cols = [slice(h * HGRN_D, (h + 1) * HGRN_D) for h in heads]
    q_t = [q_t_all[:, cols[h]] for h in heads]
    a = [lax.dot_general(q_t[h], k_t_all[:, cols[h]], nt_dims, preferred_element_type=F32) for h in heads]
    u_t = []
    for h in heads:
        vt = i_ref[:, cols[h]].astype(F32).T.astype(BF16)
        kd = kd_all[:, cols[h]]
        u_t.append([jnp.dot(vt, jnp.where(rowc == ci, kd, 0.0).astype(BF16), preferred_element_type=F32)
                    for ci in range(n_chunks)])
    o_intra = [jnp.dot((a[h] * tril_ref[...]).astype(BF16), i_ref[:, cols[h]], preferred_element_type=F32)
               for h in heads]
    states = []
    for h in heads:
        st = st_ref[h]
        dec = dec_all[:, cols[h]]
        before = []
        for ci in range(n_chunks):
            before.append(st.astype(BF16))
            st = st * dec[ci * c:ci * c + 1, :] + u_t[h][ci]
        st_ref[h] = st
        states.append(before)
    for h in heads:
        o_inter = [lax.dot_general(q_t[h][ci * c:(ci + 1) * c], states[h][ci], nt_dims,
                                   preferred_element_type=F32) for ci in range(n_chunks)]
        o = o_intra[h] + jnp.concatenate(o_inter, axis=0)
        on = o * lax.rsqrt(jnp.mean(o * o, axis=-1, keepdims=True) + EPS) * nw_ref[...]
        o_ref[:, cols[h]] = (on * _silu(g_ref[:, cols[h]].astype(F32))).astype(o_ref.dtype)


def _hgrn(proj, lb_logits, norm_w, bsz, s):
    r, c = HGRN_TILE, HGRN_CHUNK
    nt = s // r
    ri = jnp.arange(r)
    same = (ri[:, None] // c) == (ri[None, :] // c)
    tril = (same & (ri[None, :] <= ri[:, None])).astype(F32)
    lj = jnp.concatenate([tril, same.astype(F32)], axis=0).astype(BF16)
    col = lambda c0: (lambda b, t: (b * nt + t, c0 // BRANCH_W))
    const = lambda shape: pl.BlockSpec(shape, lambda b, t: (0,) * len(shape))
    return pl.pallas_call(
        _hgrn_body,
        grid=(bsz, nt),
        in_specs=[
            pl.BlockSpec((r, BRANCH_W), col(COL_FB)),
            pl.BlockSpec((r, BRANCH_W), col(COL_IB)),
            pl.BlockSpec((r, BRANCH_W), col(COL_QB)),
            pl.BlockSpec((r, BRANCH_W), col(COL_GB)),
            const((lb_logits.shape[0], BRANCH_W)),
            const((1, HGRN_D)),
            const((2 * r, r)),
            const((r, r)),
        ],
        out_specs=pl.BlockSpec((r, BRANCH_W), lambda b, t: (b * nt + t, 0)),
        out_shape=jax.ShapeDtypeStruct((bsz * s, BRANCH_W), BF16),
        scratch_shapes=[pltpu.VMEM((HGRN_HEADS, HGRN_D, HGRN_D), F32)],
        compiler_params=pltpu.CompilerParams(
            dimension_semantics=("parallel", "arbitrary"), vmem_limit_bytes=VMEM_LIMIT),
        name="hgrn",
    )(proj, proj, proj, proj, lb_logits, norm_w.reshape(1, HGRN_D), lj, tril)


def _xattn_body(q_ref, z_ref, mem_ref, mw_ref, wkv_ref, o_ref, km_ref, vm_ref):
    @pl.when(pl.program_id(1) == 0)
    def _():
        m = mem_ref[0]
        mn = m * lax.rsqrt(jnp.mean(m * m, axis=-1, keepdims=True) + EPS) * mw_ref[...]
        kv = jnp.dot(mn.astype(BF16), wkv_ref[...], preferred_element_type=F32)
        km_ref[...] = kv[:, :BRANCH_W].astype(BF16)
        vm_ref[...] = kv[:, BRANCH_W:].astype(BF16)

    for h in range(XA_HEADS):
        cols = slice(h * XA_HD, (h + 1) * XA_HD)
        logits = lax.dot_general(q_ref[:, cols], km_ref[:, cols], (((1,), (1,)), ((), ())),
                                 preferred_element_type=F32) * (XA_HD ** -0.5)
        mx = jnp.max(logits, axis=-1, keepdims=True)
        p = jnp.exp(logits - mx)
        den = jnp.sum(p, axis=-1, keepdims=True)
        o = jnp.dot(p.astype(BF16), vm_ref[:, cols], preferred_element_type=F32) / den
        o_ref[:, cols] = (o * _silu(z_ref[:, cols].astype(F32))).astype(o_ref.dtype)


def _xattn(proj, mem, mem_norm_w, wkv_bf16, bsz, s, tq=1024):
    n_mem, d = mem.shape[1], mem.shape[2]
    nq = s // tq
    return pl.pallas_call(
        _xattn_body,
        grid=(bsz, nq),
        in_specs=[
            pl.BlockSpec((tq, BRANCH_W), lambda b, i: (b * nq + i, COL_QC // BRANCH_W)),
            pl.BlockSpec((tq, BRANCH_W), lambda b, i: (b * nq + i, COL_ZC // BRANCH_W)),
            pl.BlockSpec((1, n_mem, d), lambda b, i: (b, 0, 0)),
            pl.BlockSpec((1, d), lambda b, i: (0, 0)),
            pl.BlockSpec((d, 2 * BRANCH_W), lambda b, i: (0, 0)),
        ],
        out_specs=pl.BlockSpec((tq, BRANCH_W), lambda b, i: (b * nq + i, 0)),
        out_shape=jax.ShapeDtypeStruct((bsz * s, BRANCH_W), BF16),
        scratch_shapes=[pltpu.VMEM((n_mem, BRANCH_W), BF16), pltpu.VMEM((n_mem, BRANCH_W), BF16)],
        compiler_params=pltpu.CompilerParams(
            dimension_semantics=("parallel", "arbitrary"), vmem_limit_bytes=VMEM_LIMIT),
        name="xattn",
    )(proj, proj, mem, mem_norm_w.reshape(1, d), wkv_bf16)


def _merge_body(x_ref, ga_ref, gb_ref, gc_ref, ya_ref, yb_ref, yc_ref,
                wa_ref, wb_ref, wc_ref, wo_ref, pw_ref, o_ref):
    def branch(g_ref, y_ref, w_ref):
        return _sigmoid(g_ref[...].astype(F32)) * jnp.dot(y_ref[...], w_ref[...],
                                                          preferred_element_type=F32)

    merged = branch(ga_ref, ya_ref, wa_ref) + branch(gb_ref, yb_ref, wb_ref) + branch(gc_ref, yc_ref, wc_ref)
    y = jnp.dot(merged.astype(BF16), wo_ref[...], preferred_element_type=F32)
    yn = y * lax.rsqrt(jnp.mean(y * y, axis=-1, keepdims=True) + EPS) * pw_ref[...]
    o_ref[...] = x_ref[...] + yn


def _merge(x2, proj, ya, yb, yc, wa, wb, wc, wo, post_w, tm=1024):
    t, d = x2.shape
    row = lambda c: (lambda i: (i, c))
    const = lambda shape: pl.BlockSpec(shape, lambda i: (0, 0))
    return pl.pallas_call(
        _merge_body,
        grid=(t // tm,),
        in_specs=[
            pl.BlockSpec((tm, d), row(0)),
            pl.BlockSpec((tm, d), row(COL_GATE_A // d)),
            pl.BlockSpec((tm, d), row(COL_GATE_B // d)),
            pl.BlockSpec((tm, d), row(COL_GATE_C // d)),
            pl.BlockSpec((tm, BRANCH_W), row(0)),
            pl.BlockSpec((tm, BRANCH_W), row(0)),
            pl.BlockSpec((tm, BRANCH_W), row(0)),
            const((BRANCH_W, d)), const((BRANCH_W, d)), const((BRANCH_W, d)),
            const((d, d)), const((1, d)),
        ],
        out_specs=pl.BlockSpec((tm, d), row(0)),
        out_shape=jax.ShapeDtypeStruct((t, d), F32),
        compiler_params=pltpu.CompilerParams(
            dimension_semantics=("parallel",), vmem_limit_bytes=VMEM_LIMIT),
        name="merge",
    )(x2, proj, proj, proj, ya, yb, yc, wa, wb, wc, wo, post_w.reshape(1, d))


def kernel(x, mem, pre_norm_w, w_in, hgrn_lb_logits, hgrn_norm_w, mem_norm_w, w_mem_kv,
           w_branch_a, w_branch_b, w_branch_c, w_out, post_norm_w):
    bsz, s, d = x.shape
    assert w_in.shape[0] == 1 and w_in.shape[2] == PROJ_TOTAL and d == 1024
    assert s % MOBA_BLOCK == 0 and s % HGRN_TILE == 0
    x2 = x.reshape(bsz * s, d)
    proj = _in_proj(x2, pre_norm_w[0], w_in[0].astype(BF16))
    slopes = jnp.exp2(-8.0 * jnp.arange(1, MOBA_HEADS + 1, dtype=F32) / MOBA_HEADS)
    ya = _moba(proj, slopes, bsz, s)
    yb = _hgrn(proj, hgrn_lb_logits, hgrn_norm_w[0], bsz, s)
    yc = _xattn(proj, mem, mem_norm_w[0], w_mem_kv[0].astype(BF16), bsz, s)
    out = _merge(x2, proj, ya, yb, yc, w_branch_a[0].astype(BF16), w_branch_b[0].astype(BF16),
                 w_branch_c[0].astype(BF16), w_out[0].astype(BF16), post_norm_w[0])
    return out.reshape(bsz, s, d)
```

```python
import jax
import jax.numpy as jnp
from jax import lax
from jax.experimental import pallas as pl
from jax.experimental.pallas import tpu as pltpu

F32 = jnp.float32
BF16 = jnp.bfloat16
EPS = 1e-6

MOBA_HEADS, MOBA_HD, MOBA_BLOCK, MOBA_TOPK = 8, 64, 256, 3
HGRN_HEADS, HGRN_D, HGRN_CHUNK = 4, 128, 32
XA_HEADS, XA_HD = 4, 128
BRANCH_W = 512
LANES = 128
COL_QA, COL_KA, COL_VA, COL_ZA = 0, 512, 1024, 1536
COL_FB, COL_IB, COL_QB, COL_GB = 2048, 2560, 3072, 3584
COL_QC, COL_ZC = 4096, 4608
COL_GATE_A, COL_GATE_B, COL_GATE_C = 5120, 6144, 7168
PROJ_TOTAL = COL_GATE_C + 1024

NEG_BIG = -1e30
LOG2E = 1.4426950408889634
MOBA_VROWS = MOBA_HD + 16
VMEM_LIMIT = 56 * 1024 * 1024


def _sigmoid(z):
    return 1.0 / (1.0 + jnp.exp2(z * (-LOG2E)))


def _silu(z):
    return z * _sigmoid(z)


def _split3(a):
    hi = a.astype(BF16).astype(F32)
    r = a - hi
    mid = r.astype(BF16).astype(F32)
    lo = (r - mid).astype(BF16).astype(F32)
    return hi, mid, lo


def _in_proj_body(x_ref, nw_ref, w_ref, o_ref, h_ref):
    @pl.when(pl.program_id(1) == 0)
    def _():
        x = x_ref[...]
        ms = jnp.mean(x * x, axis=-1, keepdims=True)
        h_ref[...] = (x * lax.rsqrt(ms + EPS) * nw_ref[...]).astype(BF16)

    o_ref[...] = jnp.dot(h_ref[...], w_ref[...], preferred_element_type=F32).astype(o_ref.dtype)


def _in_proj(x2, norm_w, w_bf16, tm=1024, tn=2048):
    t, d = x2.shape
    n = w_bf16.shape[1]
    return pl.pallas_call(
        _in_proj_body,
        grid=(t // tm, n // tn),
        in_specs=[
            pl.BlockSpec((tm, d), lambda i, j: (i, 0)),
            pl.BlockSpec((1, d), lambda i, j: (0, 0)),
            pl.BlockSpec((d, tn), lambda i, j: (0, j)),
        ],
        out_specs=pl.BlockSpec((tm, tn), lambda i, j: (i, j)),
        out_shape=jax.ShapeDtypeStruct((t, n), BF16),
        scratch_shapes=[pltpu.VMEM((tm, d), BF16)],
        compiler_params=pltpu.CompilerParams(
            dimension_semantics=("parallel", "arbitrary"), vmem_limit_bytes=VMEM_LIMIT),
        name="in_proj",
    )(x2, norm_w.reshape(1, d), w_bf16)


def _moba_aug_keys(s, nb_lanes=16):
    row = jnp.arange(s)
    blk = row // MOBA_BLOCK
    off = (row % MOBA_BLOCK).astype(F32)
    onehot = (blk[:, None] == jnp.arange(nb_lanes)[None, :]).astype(F32)
    bias_lanes = jnp.concatenate([onehot, onehot, onehot, off[:, None], off[:, None], off[:, None],
                                  jnp.zeros((s, 64 - 3 * nb_lanes - 3), F32)], axis=1)
    zeros = jnp.zeros((s, 64), F32)
    a0 = jnp.concatenate([zeros, bias_lanes], axis=1)
    a1 = jnp.concatenate([bias_lanes, zeros], axis=1)
    return jnp.stack([a0, a1]).astype(BF16)


def _moba_body(slopes_ref, q_ref, k_ref, v_ref, z_ref, aug_ref, avg_ref, causal_ref, o_ref,
               ka_ref, vt_ref, kbp_ref, qa_ref, m_ref, acc_ref, p_ref, al_ref):
    s = k_ref.shape[0]
    nb = s // MOBA_BLOCK
    blk = MOBA_BLOCK
    nh = MOBA_HEADS
    hd = MOBA_HD
    i = pl.program_id(1)

    @pl.when(i == 0)
    def _():
        lane = lax.broadcasted_iota(jnp.int32, (1, LANES), 1)
        first_half = lane < 64
        kbar = jnp.dot(avg_ref[...], k_ref[...], preferred_element_type=F32)
        for h in range(nh):
            hp, e = h // 2, h % 2
            cols = slice(hp * LANES, (hp + 1) * LANES)
            keep = first_half if e == 0 else jnp.logical_not(first_half)
            for j in range(nb):
                rows = slice(j * blk, (j + 1) * blk)
                ka_ref[h, j] = jnp.where(keep, k_ref[rows, cols], aug_ref[e, rows, :])
            pieces = _split3(jnp.where(keep, kbar[:, cols], 0.0))
            for p in range(3):
                kbp_ref[(h * 3 + p) * 16:(h * 3 + p + 1) * 16, :] = pieces[p].astype(BF16)
        ones_row = jnp.where(lax.broadcasted_iota(jnp.int32, (MOBA_VROWS - hd, blk), 0) == 0, 1.0, 0.0)
        for j in range(nb):
            vt = v_ref[j * blk:(j + 1) * blk, :].astype(F32).T
            for h in range(nh):
                vt_ref[j, h, 0:hd, :] = vt[h * hd:(h + 1) * hd].astype(BF16)
                vt_ref[j, h, hd:MOBA_VROWS, :] = ones_row.astype(BF16)

    nidx = lax.broadcasted_iota(jnp.int32, (16, blk), 0)
    qoff = lax.broadcasted_iota(jnp.int32, (16, blk), 1)
    dist0 = (qoff + (i - nidx) * blk).astype(F32)
    for hp in range(nh // 2):
        cols = slice(hp * LANES, (hp + 1) * LANES)
        qt = (q_ref[:, cols].astype(F32) * (hd ** -0.5)).T
        gates = jnp.dot(kbp_ref[hp * 96:(hp + 1) * 96, :], qt.astype(BF16),
                        preferred_element_type=F32)
        qs = qt * LOG2E
        for e in range(2):
            h = 2 * hp + e
            g = gates[e * 48:e * 48 + 16] + gates[e * 48 + 16:e * 48 + 32] + gates[e * 48 + 32:e * 48 + 48]
            cnt = jnp.zeros((16, blk), jnp.int32)
            for m in range(nb):
                gm = g[m:m + 1, :]
                ahead = (gm > g) | ((gm == g) & (m < nidx))
                cnt = cnt + jnp.where(ahead, jnp.where(m < i, 1, 0), 0)
            sel = ((nidx < i) & (cnt < MOBA_TOPK)) | (nidx == i)
            sl = slopes_ref[h] * LOG2E
            bias = jnp.where(sel, -sl * dist0, NEG_BIG)
            b_hi, b_mid, b_lo = _split3(bias)
            s_hi, s_mid, s_lo = _split3(jnp.full((16, blk), sl, F32))
            srow = jnp.where(nidx == 0, s_hi, jnp.where(nidx == 1, s_mid, jnp.where(nidx == 2, s_lo, 0.0)))
            qh = qs[0:64] if e == 0 else qs[64:128]
            parts = [qh, b_hi, b_mid, b_lo, srow] if e == 0 else [b_hi, b_mid, b_lo, srow, qh]
            qa_ref[h] = jnp.concatenate(parts, axis=0).astype(BF16)

    m_ref[...] = jnp.full(m_ref.shape, -jnp.inf, F32)
    acc_ref[...] = jnp.zeros(acc_ref.shape, F32)
    p_ref[...] = jnp.zeros(p_ref.shape, BF16)
    al_ref[...] = jnp.ones(al_ref.shape, F32)

    def step(j, masked):
        jm = jnp.maximum(j - 1, 0)
        scores = [jnp.dot(ka_ref[h, j], qa_ref[h], preferred_element_type=F32)
                  for h in range(nh)]
        pv = [jnp.dot(vt_ref[jm, h], p_ref[h], preferred_element_type=F32) for h in range(nh)]
        probs, alphas = [], []
        for h in range(nh):
            sT = scores[h]
            if masked:
                sT = jnp.where(causal_ref[...] > 0.5, sT, -jnp.inf)
            m_old = m_ref[h]
            m_new = jnp.maximum(m_old, jnp.max(sT, axis=0, keepdims=True))
            alphas.append(jnp.exp2(m_old - m_new))
            probs.append(jnp.exp2(sT - m_new).astype(BF16))
            m_ref[h] = m_new
        for h in range(nh):
            acc_ref[h] = al_ref[h] * acc_ref[h] + pv[h]
        return probs, alphas

    def past(j, carry):
        probs, alphas = step(j, False)
        for h in range(nh):
            p_ref[h] = probs[h]
            al_ref[h] = alphas[h]
        return carry

    lax.fori_loop(0, i, past, 0)
    probs, alphas = step(i, True)
    pv = [jnp.dot(vt_ref[i, h], probs[h], preferred_element_type=F32) for h in range(nh)]
    outs = []
    for h in range(nh):
        acc = alphas[h] * acc_ref[h] + pv[h]
        outs.append(acc[0:hd] / acc[hd:hd + 1])
    ot = jnp.concatenate(outs, axis=0)
    o_ref[...] = (ot.T * _silu(z_ref[...].astype(F32))).astype(o_ref.dtype)


def _moba(proj, slopes, bsz, s):
    nb = s // MOBA_BLOCK
    assert nb <= 16
    blk = MOBA_BLOCK
    aug = _moba_aug_keys(s)
    avg = ((jnp.arange(16)[:, None] == (jnp.arange(s) // blk)[None, :]).astype(F32) / blk).astype(BF16)
    causal = (jnp.arange(blk)[:, None] <= jnp.arange(blk)[None, :]).astype(F32)
    qblock = lambda c0: (lambda b, i, *_: (b * nb + i, c0 // BRANCH_W))
    whole = lambda c0: (lambda b, i, *_: (b, c0 // BRANCH_W))
    return pl.pallas_call(
        _moba_body,
        grid_spec=pltpu.PrefetchScalarGridSpec(
            num_scalar_prefetch=1,
            grid=(bsz, nb),
            in_specs=[
                pl.BlockSpec((blk, BRANCH_W), qblock(COL_QA)),
                pl.BlockSpec((s, BRANCH_W), whole(COL_KA)),
                pl.BlockSpec((s, BRANCH_W), whole(COL_VA)),
                pl.BlockSpec((blk, BRANCH_W), qblock(COL_ZA)),
                pl.BlockSpec((2, s, LANES), lambda b, i, *_: (0, 0, 0)),
                pl.BlockSpec((16, s), lambda b, i, *_: (0, 0)),
                pl.BlockSpec((blk, blk), lambda b, i, *_: (0, 0)),
            ],
            out_specs=pl.BlockSpec((blk, BRANCH_W), lambda b, i, *_: (b * nb + i, 0)),
            scratch_shapes=[
                pltpu.VMEM((MOBA_HEADS, nb, blk, LANES), BF16),
                pltpu.VMEM((nb, MOBA_HEADS, MOBA_VROWS, blk), BF16),
                pltpu.VMEM((MOBA_HEADS * 48, LANES), BF16),
                pltpu.VMEM((MOBA_HEADS, LANES, blk), BF16),
                pltpu.VMEM((MOBA_HEADS, 1, blk), F32),
                pltpu.VMEM((MOBA_HEADS, MOBA_VROWS, blk), F32),
                pltpu.VMEM((MOBA_HEADS, blk, blk), BF16),
                pltpu.VMEM((MOBA_HEADS, 1, blk), F32),
            ],
        ),
        out_shape=jax.ShapeDtypeStruct((bsz * s, BRANCH_W), BF16),
        compiler_params=pltpu.CompilerParams(
            dimension_semantics=("parallel", "arbitrary"), vmem_limit_bytes=VMEM_LIMIT),
        name="moba",
    )(slopes, proj, proj, proj, proj, aug, avg, causal)


HGRN_TILE = 256
HGRN_SUB = 2


def _hgrn_body(f_ref, i_ref, q_ref, g_ref, lbl_ref, nw_ref, lmat_ref, tril_ref, o_ref, st_ref):
    r, c, d = HGRN_TILE, HGRN_CHUNK, HGRN_D
    n_chunks = r // c
    heads = range(HGRN_HEADS)
    tiles = range(HGRN_SUB)
    chunks = range(n_chunks)
    cols = [slice(h * d, (h + 1) * d) for h in heads]
    rows = [slice(t * r, (t + 1) * r) for t in tiles]
    nt_dims = (((1,), (1,)), ((), ()))

    @pl.when(pl.program_id(1) == 0)
    def _():
        st_ref[...] = jnp.zeros_like(st_ref)

    logits = lbl_ref[...]
    ex = jnp.exp(logits - jnp.max(logits, axis=0, keepdims=True))
    lb = ex[0:1, :] / jnp.sum(ex, axis=0, keepdims=True)

    kk, b = [], []
    for t in tiles:
        fl = f_ref[rows[t], :].astype(F32)
        log_f = jnp.log(lb + (1.0 - lb) * _sigmoid(fl))
        kk.append((1.0 - lb) * _sigmoid(-fl))
        lmat = lmat_ref[...]
        hi, mid, lo = [jnp.dot(lmat, piece.astype(BF16), preferred_element_type=F32)
                       for piece in _split3(log_f)]
        b.append(hi + mid + lo)

    q_t, k_t32, k_t, dec = [], [], [], []
    for t in tiles:
        q_t.append((q_ref[rows[t], :].astype(F32) * jnp.exp(b[t])).astype(BF16))
        kt = kk[t] * jnp.exp(-b[t])
        k_t32.append(kt)
        k_t.append(kt.astype(BF16))
        dec.append([jnp.exp(b[t][ci * c + c - 1:ci * c + c, :]) for ci in chunks])

    a = [[lax.dot_general(q_t[t][:, cols[h]], k_t[t][:, cols[h]], nt_dims, preferred_element_type=F32)
          for h in heads] for t in tiles]
    u_t = []
    for t in tiles:
        per_head = []
        for h in heads:
            vt = i_ref[rows[t], cols[h]].astype(F32).T.astype(BF16)
            blocks = []
            for ci in chunks:
                slab = (k_t32[t][ci * c:(ci + 1) * c, cols[h]] * dec[t][ci][:, cols[h]]).astype(BF16)
                pieces = ([jnp.zeros((c, ci * d), BF16)] if ci else []) + [slab]
                if ci < n_chunks - 1:
                    pieces.append(jnp.zeros((c, (n_chunks - 1 - ci) * d), BF16))
                blocks.append(jnp.concatenate(pieces, axis=1))
            kd_blk = jnp.concatenate(blocks, axis=0)
            per_head.append(jnp.dot(vt, kd_blk, preferred_element_type=F32))
        u_t.append(per_head)
    o_intra = [[jnp.dot((a[t][h] * tril_ref[...]).astype(BF16), i_ref[rows[t], cols[h]],
                        preferred_element_type=F32) for h in heads] for t in tiles]

    states = [[None] * HGRN_HEADS for _ in tiles]
    for h in heads:
        st = st_ref[h]
        for t in tiles:
            before = []
            for ci in chunks:
                before.append(st.astype(BF16))
                st = st * dec[t][ci][:, cols[h]] + u_t[t][h][:, ci * d:(ci + 1) * d]
            states[t][h] = before
        st_ref[h] = st

    for t in tiles:
        for h in heads:
            o_inter = [lax.dot_general(q_t[t][ci * c:(ci + 1) * c, cols[h]], states[t][h][ci], nt_dims,
                                       preferred_element_type=F32) for ci in chunks]
            o = o_intra[t][h] + jnp.concatenate(o_inter, axis=0)
            on = o * lax.rsqrt(jnp.mean(o * o, axis=-1, keepdims=True) + EPS) * nw_ref[...]
            o_ref[rows[t], cols[h]] = (on * _silu(g_ref[rows[t], cols[h]].astype(F32))).astype(o_ref.dtype)


def _hgrn(proj, lb_logits, norm_w, bsz, s):
    r, c = HGRN_TILE, HGRN_CHUNK
    rs = r * HGRN_SUB
    nt = s // rs
    ri = jnp.arange(r)
    same = (ri[:, None] // c) == (ri[None, :] // c)
    tril = (same & (ri[None, :] <= ri[:, None])).astype(F32)
    col = lambda c0: (lambda b, t: (b * nt + t, c0 // BRANCH_W))
    const = lambda shape: pl.BlockSpec(shape, lambda b, t: (0,) * len(shape))
    return pl.pallas_call(
        _hgrn_body,
        grid=(bsz, nt),
        in_specs=[
            pl.BlockSpec((rs, BRANCH_W), col(COL_FB)),
            pl.BlockSpec((rs, BRANCH_W), col(COL_IB)),
            pl.BlockSpec((rs, BRANCH_W), col(COL_QB)),
            pl.BlockSpec((rs, BRANCH_W), col(COL_GB)),
            const((lb_logits.shape[0], BRANCH_W)),
            const((1, HGRN_D)),
            const((r, r)),
            const((r, r)),
        ],
        out_specs=pl.BlockSpec((rs, BRANCH_W), lambda b, t: (b * nt + t, 0)),
        out_shape=jax.ShapeDtypeStruct((bsz * s, BRANCH_W), BF16),
        scratch_shapes=[pltpu.VMEM((HGRN_HEADS, HGRN_D, HGRN_D), F32)],
        compiler_params=pltpu.CompilerParams(
            dimension_semantics=("parallel", "arbitrary"), vmem_limit_bytes=VMEM_LIMIT),
        name="hgrn",
    )(proj, proj, proj, proj, lb_logits, norm_w.reshape(1, HGRN_D), tril.astype(BF16), tril)


def _xattn_body(q_ref, z_ref, mem_ref, mw_ref, wkv_ref, o_ref, km_ref, vm_ref):
    @pl.when(pl.program_id(1) == 0)
    def _():
        m = mem_ref[0]
        mn = m * lax.rsqrt(jnp.mean(m * m, axis=-1, keepdims=True) + EPS) * mw_ref[...]
        kv = jnp.dot(mn.astype(BF16), wkv_ref[...], preferred_element_type=F32)
        km_ref[...] = kv[:, :BRANCH_W].astype(BF16)
        vm_ref[...] = kv[:, BRANCH_W:].astype(BF16)

    for h in range(XA_HEADS):
        cols = slice(h * XA_HD, (h + 1) * XA_HD)
        logits = lax.dot_general(q_ref[:, cols], km_ref[:, cols], (((1,), (1,)), ((), ())),
                                 preferred_element_type=F32) * (XA_HD ** -0.5)
        mx = jnp.max(logits, axis=-1, keepdims=True)
        p = jnp.exp(logits - mx)
        den = jnp.sum(p, axis=-1, keepdims=True)
        o = jnp.dot(p.astype(BF16), vm_ref[:, cols], preferred_element_type=F32) / den
        o_ref[:, cols] = (o * _silu(z_ref[:, cols].astype(F32))).astype(o_ref.dtype)


def _xattn(proj, mem, mem_norm_w, wkv_bf16, bsz, s, tq=1024):
    n_mem, d = mem.shape[1], mem.shape[2]
    nq = s // tq
    return pl.pallas_call(
        _xattn_body,
        grid=(bsz, nq),
        in_specs=[
            pl.BlockSpec((tq, BRANCH_W), lambda b, i: (b * nq + i, COL_QC // BRANCH_W)),
            pl.BlockSpec((tq, BRANCH_W), lambda b, i: (b * nq + i, COL_ZC // BRANCH_W)),
            pl.BlockSpec((1, n_mem, d), lambda b, i: (b, 0, 0)),
            pl.BlockSpec((1, d), lambda b, i: (0, 0)),
            pl.BlockSpec((d, 2 * BRANCH_W), lambda b, i: (0, 0)),
        ],
        out_specs=pl.BlockSpec((tq, BRANCH_W), lambda b, i: (b * nq + i, 0)),
        out_shape=jax.ShapeDtypeStruct((bsz * s, BRANCH_W), BF16),
        scratch_shapes=[pltpu.VMEM((n_mem, BRANCH_W), BF16), pltpu.VMEM((n_mem, BRANCH_W), BF16)],
        compiler_params=pltpu.CompilerParams(
            dimension_semantics=("parallel", "arbitrary"), vmem_limit_bytes=VMEM_LIMIT),
        name="xattn",
    )(proj, proj, mem, mem_norm_w.reshape(1, d), wkv_bf16)


def _merge_body(x_ref, ga_ref, gb_ref, gc_ref, ya_ref, yb_ref, yc_ref,
                wa_ref, wb_ref, wc_ref, wo_ref, pw_ref, o_ref):
    tm = x_ref.shape[0]
    halves = [slice(0, tm // 2), slice(tm // 2, tm)]
    branches = ((ga_ref, ya_ref, wa_ref), (gb_ref, yb_ref, wb_ref), (gc_ref, yc_ref, wc_ref))
    proj = [[jnp.dot(y_ref[rows, :], w_ref[...], preferred_element_type=F32) for _, y_ref, w_ref in branches]
            for rows in halves]
    gated = [[_sigmoid(g_ref[rows, :].astype(F32)) * proj[k][n] for n, (g_ref, _, _) in enumerate(branches)]
             for k, rows in enumerate(halves)]
    merged = [gated[k][0] + gated[k][1] + gated[k][2] for k in range(2)]
    y = [jnp.dot(merged[k].astype(BF16), wo_ref[...], preferred_element_type=F32) for k in range(2)]
    for k, rows in enumerate(halves):
        yn = y[k] * lax.rsqrt(jnp.mean(y[k] * y[k], axis=-1, keepdims=True) + EPS) * pw_ref[...]
        o_ref[rows, :] = x_ref[rows, :] + yn


def _merge(x2, proj, ya, yb, yc, wa, wb, wc, wo, post_w, tm=1024):
    t, d = x2.shape
    row = lambda c: (lambda i: (i, c))
    const = lambda shape: pl.BlockSpec(shape, lambda i: (0, 0))
    return pl.pallas_call(
        _merge_body,
        grid=(t // tm,),
        in_specs=[
            pl.BlockSpec((tm, d), row(0)),
            pl.BlockSpec((tm, d), row(COL_GATE_A // d)),
            pl.BlockSpec((tm, d), row(COL_GATE_B // d)),
            pl.BlockSpec((tm, d), row(COL_GATE_C // d)),
            pl.BlockSpec((tm, BRANCH_W), row(0)),
            pl.BlockSpec((tm, BRANCH_W), row(0)),
            pl.BlockSpec((tm, BRANCH_W), row(0)),
            const((BRANCH_W, d)), const((BRANCH_W, d)), const((BRANCH_W, d)),
            const((d, d)), const((1, d)),
        ],
        out_specs=pl.BlockSpec((tm, d), row(0)),
        out_shape=jax.ShapeDtypeStruct((t, d), F32),
        compiler_params=pltpu.CompilerParams(
            dimension_semantics=("parallel",), vmem_limit_bytes=VMEM_LIMIT),
        name="merge",
    )(x2, proj, proj, proj, ya, yb, yc, wa, wb, wc, wo, post_w.reshape(1, d))


def kernel(x, mem, pre_norm_w, w_in, hgrn_lb_logits, hgrn_norm_w, mem_norm_w, w_mem_kv,
           w_branch_a, w_branch_b, w_branch_c, w_out, post_norm_w):
    bsz, s, d = x.shape
    assert w_in.shape[0] == 1 and w_in.shape[2] == PROJ_TOTAL and d == 1024
    assert s % MOBA_BLOCK == 0 and s % (HGRN_TILE * HGRN_SUB) == 0
    x2 = x.reshape(bsz * s, d)
    proj = _in_proj(x2, pre_norm_w[0], w_in[0].astype(BF16))
    slopes = jnp.exp2(-8.0 * jnp.arange(1, MOBA_HEADS + 1, dtype=F32) / MOBA_HEADS)
    ya = _moba(proj, slopes, bsz, s)
    yb = _hgrn(proj, hgrn_lb_logits, hgrn_norm_w[0], bsz, s)
    yc = _xattn(proj, mem, mem_norm_w[0], w_mem_kv[0].astype(BF16), bsz, s)
    out = _merge(x2, proj, ya, yb, yc, w_branch_a[0].astype(BF16), w_branch_b[0].astype(BF16),
                 w_branch_c[0].astype(BF16), w_out[0].astype(BF16), post_norm_w[0])
    return out.reshape(bsz, s, d)
```

```python
import jax
import jax.numpy as jnp
import numpy as np
from jax import lax
from jax.experimental import pallas as pl
from jax.experimental.pallas import tpu as pltpu

F32 = jnp.float32
BF16 = jnp.bfloat16
EPS = 1e-6

MOBA_HEADS, MOBA_HD, MOBA_BLOCK, MOBA_TOPK = 8, 64, 256, 3
HGRN_HEADS, HGRN_D, HGRN_CHUNK = 4, 128, 32
XA_HEADS, XA_HD = 4, 128
BRANCH_W = 512
LANES = 128
COL_QA, COL_KA, COL_VA, COL_ZA = 0, 512, 1024, 1536
COL_FB, COL_IB, COL_QB, COL_GB = 2048, 2560, 3072, 3584
COL_QC, COL_ZC = 4096, 4608
COL_GATE_A, COL_GATE_B, COL_GATE_C = 5120, 6144, 7168
PROJ_TOTAL = COL_GATE_C + 1024

NEG_BIG = -1e30
LOG2E = 1.4426950408889634
MOBA_VROWS = MOBA_HD + 16
VMEM_LIMIT = 56 * 1024 * 1024


def _sigmoid(z):
    return 1.0 / (1.0 + jnp.exp2(z * (-LOG2E)))


def _silu(z):
    return z * _sigmoid(z)


def _split3(a):
    hi = a.astype(BF16).astype(F32)
    r = a - hi
    mid = r.astype(BF16).astype(F32)
    lo = (r - mid).astype(BF16).astype(F32)
    return hi, mid, lo


def _in_proj_body(x_ref, nw_ref, w_ref, o_ref, h_ref):
    @pl.when(pl.program_id(1) == 0)
    def _():
        x = x_ref[...]
        ms = jnp.mean(x * x, axis=-1, keepdims=True)
        h_ref[...] = (x * lax.rsqrt(ms + EPS) * nw_ref[...]).astype(BF16)

    o_ref[...] = jnp.dot(h_ref[...], w_ref[...], preferred_element_type=F32).astype(o_ref.dtype)


def _in_proj(x2, norm_w, w_bf16, tm=1024, tn=2048):
    t, d = x2.shape
    n = w_bf16.shape[1]
    return pl.pallas_call(
        _in_proj_body,
        grid=(t // tm, n // tn),
        in_specs=[
            pl.BlockSpec((tm, d), lambda i, j: (i, 0)),
            pl.BlockSpec((1, d), lambda i, j: (0, 0)),
            pl.BlockSpec((d, tn), lambda i, j: (0, j)),
        ],
        out_specs=pl.BlockSpec((tm, tn), lambda i, j: (i, j)),
        out_shape=jax.ShapeDtypeStruct((t, n), BF16),
        scratch_shapes=[pltpu.VMEM((tm, d), BF16)],
        compiler_params=pltpu.CompilerParams(
            dimension_semantics=("parallel", "arbitrary"), vmem_limit_bytes=VMEM_LIMIT),
        name="in_proj",
    )(x2, norm_w.reshape(1, d), w_bf16)


def _moba_constants(s, nb_lanes=16):
    blk = MOBA_BLOCK
    row = np.arange(s)
    off = (row % blk).astype(np.float32)
    onehot = (row[:, None] // blk == np.arange(nb_lanes)[None, :]).astype(np.float32)
    bias_lanes = np.concatenate([onehot, onehot, onehot, off[:, None], off[:, None], off[:, None],
                                 np.zeros((s, 64 - 3 * nb_lanes - 3), np.float32)], axis=1)
    zeros = np.zeros((s, 64), np.float32)
    a0 = np.concatenate([zeros, bias_lanes], axis=1)
    a1 = np.concatenate([bias_lanes, zeros], axis=1)
    aug = np.stack([a0, a1]).astype(BF16)
    avg = ((np.arange(nb_lanes)[:, None] == (row // blk)[None, :]).astype(np.float32) / blk).astype(BF16)
    causal = np.where(np.arange(blk)[:, None] <= np.arange(blk)[None, :], 0.0, -np.inf).astype(np.float32)
    return aug, avg, causal


def _moba_body(slopes_ref, q_ref, qn_ref, k_ref, v_ref, z_ref, aug_ref, avg_ref, causal_ref, o_ref,
               ka_ref, vt_ref, kbp_ref, qa_ref, qan_ref, m_ref, acc_ref, p_ref, al_ref):
    s = k_ref.shape[0]
    nb = s // MOBA_BLOCK
    blk = MOBA_BLOCK
    nh = MOBA_HEADS
    hd = MOBA_HD
    i = pl.program_id(1)

    def build_queries(src_ref, iq, dst_ref):
        nidx = lax.broadcasted_iota(jnp.int32, (16, blk), 0)
        qoff = lax.broadcasted_iota(jnp.int32, (16, blk), 1)
        dist0 = (qoff + (iq - nidx) * blk).astype(F32)
        qts = [(src_ref[:, hp * LANES:(hp + 1) * LANES].astype(F32) * (hd ** -0.5)).T
               for hp in range(nh // 2)]
        gates = [jnp.dot(kbp_ref[hp * 96:(hp + 1) * 96, :], qts[hp].astype(BF16),
                         preferred_element_type=F32) for hp in range(nh // 2)]
        for hp in range(nh // 2):
            qs = qts[hp] * LOG2E
            for e in range(2):
                h = 2 * hp + e
                gg = gates[hp]
                g = gg[e * 48:e * 48 + 16] + gg[e * 48 + 16:e * 48 + 32] + gg[e * 48 + 32:e * 48 + 48]
                cnt = jnp.zeros((16, blk), jnp.int32)
                for m in range(nb):
                    gm = g[m:m + 1, :]
                    ahead = (gm > g) | ((gm == g) & (m < nidx))
                    cnt = cnt + jnp.where(ahead, jnp.where(m < iq, 1, 0), 0)
                sel = ((nidx < iq) & (cnt < MOBA_TOPK)) | (nidx == iq)
                sl = slopes_ref[h] * LOG2E
                bias = jnp.where(sel, -sl * dist0, NEG_BIG)
                b_hi, b_mid, b_lo = _split3(bias)
                s_hi, s_mid, s_lo = _split3(jnp.full((16, blk), sl, F32))
                srow = jnp.where(nidx == 0, s_hi, jnp.where(nidx == 1, s_mid, jnp.where(nidx == 2, s_lo, 0.0)))
                qh = qs[0:64] if e == 0 else qs[64:128]
                parts = [qh, b_hi, b_mid, b_lo, srow] if e == 0 else [b_hi, b_mid, b_lo, srow, qh]
                dst_ref[h] = jnp.concatenate(parts, axis=0).astype(BF16)

    @pl.when(i == 0)
    def _():
        lane = lax.broadcasted_iota(jnp.int32, (1, LANES), 1)
        first_half = lane < 64
        kbar = jnp.dot(avg_ref[...], k_ref[...], preferred_element_type=F32)
        for h in range(nh):
            hp, e = h // 2, h % 2
            cols = slice(hp * LANES, (hp + 1) * LANES)
            keep = first_half if e == 0 else jnp.logical_not(first_half)
            for j in range(nb):
                rows = slice(j * blk, (j + 1) * blk)
                ka_ref[h, j] = jnp.where(keep, k_ref[rows, cols], aug_ref[e, rows, :])
            pieces = _split3(jnp.where(keep, kbar[:, cols], 0.0))
            for p in range(3):
                kbp_ref[(h * 3 + p) * 16:(h * 3 + p + 1) * 16, :] = pieces[p].astype(BF16)
        ones_row = jnp.where(lax.broadcasted_iota(jnp.int32, (MOBA_VROWS - hd, blk), 0) == 0, 1.0, 0.0)
        for j in range(nb):
            vt = v_ref[j * blk:(j + 1) * blk, :].astype(F32).T
            for h in range(nh):
                vt_ref[j, h, 0:hd, :] = vt[h * hd:(h + 1) * hd].astype(BF16)
                vt_ref[j, h, hd:MOBA_VROWS, :] = ones_row.astype(BF16)
        build_queries(q_ref, 0, qa_ref)

    @pl.when(i > 0)
    def _():
        qa_ref[...] = qan_ref[...]

    m_ref[...] = jnp.full(m_ref.shape, -jnp.inf, F32)
    acc_ref[...] = jnp.zeros(acc_ref.shape, F32)
    p_ref[...] = jnp.zeros(p_ref.shape, BF16)
    al_ref[...] = jnp.ones(al_ref.shape, F32)

    def step(j, masked):
        jm = jnp.maximum(j - 1, 0)
        scores = [jnp.dot(ka_ref[h, j], qa_ref[h], preferred_element_type=F32)
                  for h in range(nh)]
        pv = [jnp.dot(vt_ref[jm, h], p_ref[h], preferred_element_type=F32) for h in range(nh)]
        probs, alphas = [], []
        for h in range(nh):
            sT = scores[h]
            if masked:
                sT = sT + causal_ref[...]
            m_old = m_ref[h]
            m_new = jnp.maximum(m_old, jnp.max(sT, axis=0, keepdims=True))
            alphas.append(jnp.exp2(m_old - m_new))
            probs.append(jnp.exp2(sT - m_new).astype(BF16))
            m_ref[h] = m_new
        for h in range(nh):
            acc_ref[h] = al_ref[h] * acc_ref[h] + pv[h]
        return probs, alphas

    def past(j, carry):
        probs, alphas = step(j, False)
        for h in range(nh):
            p_ref[h] = probs[h]
            al_ref[h] = alphas[h]
        return carry

    lax.fori_loop(0, i, past, 0)
    build_queries(qn_ref, i + 1, qan_ref)
    probs, alphas = step(i, True)
    pv = [jnp.dot(vt_ref[i, h], probs[h], preferred_element_type=F32) for h in range(nh)]
    outs = []
    for h in range(nh):
        acc = alphas[h] * acc_ref[h] + pv[h]
        outs.append(acc[0:hd] / acc[hd:hd + 1])
    ot = jnp.concatenate(outs, axis=0)
    o_ref[...] = (ot.T * _silu(z_ref[...].astype(F32))).astype(o_ref.dtype)


def _moba(proj, slopes, bsz, s):
    nb = s // MOBA_BLOCK
    assert nb <= 16
    blk = MOBA_BLOCK
    aug, avg, causal = _moba_constants(s)
    qblock = lambda c0: (lambda b, i, *_: (b * nb + i, c0 // BRANCH_W))
    qnext = lambda b, i, *_: (b * nb + jnp.minimum(i + 1, nb - 1), COL_QA // BRANCH_W)
    whole = lambda c0: (lambda b, i, *_: (b, c0 // BRANCH_W))
    return pl.pallas_call(
        _moba_body,
        grid_spec=pltpu.PrefetchScalarGridSpec(
            num_scalar_prefetch=1,
            grid=(bsz, nb),
            in_specs=[
                pl.BlockSpec((blk, BRANCH_W), qblock(COL_QA)),
                pl.BlockSpec((blk, BRANCH_W), qnext),
                pl.BlockSpec((s, BRANCH_W), whole(COL_KA)),
                pl.BlockSpec((s, BRANCH_W), whole(COL_VA)),
                pl.BlockSpec((blk, BRANCH_W), qblock(COL_ZA)),
                pl.BlockSpec((2, s, LANES), lambda b, i, *_: (0, 0, 0)),
                pl.BlockSpec((16, s), lambda b, i, *_: (0, 0)),
                pl.BlockSpec((blk, blk), lambda b, i, *_: (0, 0)),
            ],
            out_specs=pl.BlockSpec((blk, BRANCH_W), lambda b, i, *_: (b * nb + i, 0)),
            scratch_shapes=[
                pltpu.VMEM((MOBA_HEADS, nb, blk, LANES), BF16),
                pltpu.VMEM((nb, MOBA_HEADS, MOBA_VROWS, blk), BF16),
                pltpu.VMEM((MOBA_HEADS * 48, LANES), BF16),
                pltpu.VMEM((MOBA_HEADS, LANES, blk), BF16),
                pltpu.VMEM((MOBA_HEADS, LANES, blk), BF16),
                pltpu.VMEM((MOBA_HEADS, 1, blk), F32),
                pltpu.VMEM((MOBA_HEADS, MOBA_VROWS, blk), F32),
                pltpu.VMEM((MOBA_HEADS, blk, blk), BF16),
                pltpu.VMEM((MOBA_HEADS, 1, blk), F32),
            ],
        ),
        out_shape=jax.ShapeDtypeStruct((bsz * s, BRANCH_W), BF16),
        compiler_params=pltpu.CompilerParams(
            dimension_semantics=("parallel", "arbitrary"), vmem_limit_bytes=VMEM_LIMIT),
        name="moba",
    )(slopes, proj, proj, proj, proj, proj, aug, avg, causal)


HGRN_TILE = 256
HGRN_SUB = 2


def _hgrn_body(f_ref, i_ref, q_ref, g_ref, lbl_ref, nw_ref, lmat_ref, tril_ref, o_ref, st_ref):
    r, c, d = HGRN_TILE, HGRN_CHUNK, HGRN_D
    n_chunks = r // c
    heads = range(HGRN_HEADS)
    tiles = range(HGRN_SUB)
    chunks = range(n_chunks)
    cols = [slice(h * d, (h + 1) * d) for h in heads]
    rows = [slice(t * r, (t + 1) * r) for t in tiles]
    nt_dims = (((1,), (1,)), ((), ()))

    @pl.when(pl.program_id(1) == 0)
    def _():
        st_ref[...] = jnp.zeros_like(st_ref)

    logits = lbl_ref[...]
    ex = jnp.exp(logits - jnp.max(logits, axis=0, keepdims=True))
    lb = ex[0:1, :] / jnp.sum(ex, axis=0, keepdims=True)

    kk, b = [], []
    for t in tiles:
        fl = f_ref[rows[t], :].astype(F32)
        log_f = jnp.log(lb + (1.0 - lb) * _sigmoid(fl))
        kk.append((1.0 - lb) * _sigmoid(-fl))
        lmat = lmat_ref[...]
        hi, mid, lo = [jnp.dot(lmat, piece.astype(BF16), preferred_element_type=F32)
                       for piece in _split3(log_f)]
        b.append(hi + mid + lo)

    q_t, k_t32, k_t, dec = [], [], [], []
    for t in tiles:
        q_t.append((q_ref[rows[t], :].astype(F32) * jnp.exp(b[t])).astype(BF16))
        kt = kk[t] * jnp.exp(-b[t])
        k_t32.append(kt)
        k_t.append(kt.astype(BF16))
        dec.append([jnp.exp(b[t][ci * c + c - 1:ci * c + c, :]) for ci in chunks])

    a = [[lax.dot_general(q_t[t][:, cols[h]], k_t[t][:, cols[h]], nt_dims, preferred_element_type=F32)
          for h in heads] for t in tiles]
    u_t = []
    for t in tiles:
        per_head = []
        for h in heads:
            vt = i_ref[rows[t], cols[h]].astype(F32).T.astype(BF16)
            blocks = []
            for ci in chunks:
                slab = (k_t32[t][ci * c:(ci + 1) * c, cols[h]] * dec[t][ci][:, cols[h]]).astype(BF16)
                pieces = ([jnp.zeros((c, ci * d), BF16)] if ci else []) + [slab]
                if ci < n_chunks - 1:
                    pieces.append(jnp.zeros((c, (n_chunks - 1 - ci) * d), BF16))
                blocks.append(jnp.concatenate(pieces, axis=1))
            kd_blk = jnp.concatenate(blocks, axis=0)
            per_head.append(jnp.dot(vt, kd_blk, preferred_element_type=F32))
        u_t.append(per_head)
    o_intra = [[jnp.dot((a[t][h] * tril_ref[...]).astype(BF16), i_ref[rows[t], cols[h]],
                        preferred_element_type=F32) for h in heads] for t in tiles]

    states = [[None] * HGRN_HEADS for _ in tiles]
    for h in heads:
        st = st_ref[h]
        for t in tiles:
            before = []
            for ci in chunks:
                before.append(st.astype(BF16))
                st = st * dec[t][ci][:, cols[h]] + u_t[t][h][:, ci * d:(ci + 1) * d]
            states[t][h] = before
        st_ref[h] = st

    for t in tiles:
        for h in heads:
            o_inter = [lax.dot_general(q_t[t][ci * c:(ci + 1) * c, cols[h]], states[t][h][ci], nt_dims,
                                       preferred_element_type=F32) for ci in chunks]
            o = o_intra[t][h] + jnp.concatenate(o_inter, axis=0)
            on = o * lax.rsqrt(jnp.mean(o * o, axis=-1, keepdims=True) + EPS) * nw_ref[...]
            o_ref[rows[t], cols[h]] = (on * _silu(g_ref[rows[t], cols[h]].astype(F32))).astype(o_ref.dtype)


def _hgrn(proj, lb_logits, norm_w, bsz, s):
    r, c = HGRN_TILE, HGRN_CHUNK
    rs = r * HGRN_SUB
    nt = s // rs
    ri = np.arange(r)
    same = (ri[:, None] // c) == (ri[None, :] // c)
    tril = (same & (ri[None, :] <= ri[:, None])).astype(np.float32)
    col = lambda c0: (lambda b, t: (b * nt + t, c0 // BRANCH_W))
    const = lambda shape: pl.BlockSpec(shape, lambda b, t: (0,) * len(shape))
    return pl.pallas_call(
        _hgrn_body,
        grid=(bsz, nt),
        in_specs=[
            pl.BlockSpec((rs, BRANCH_W), col(COL_FB)),
            pl.BlockSpec((rs, BRANCH_W), col(COL_IB)),
            pl.BlockSpec((rs, BRANCH_W), col(COL_QB)),
            pl.BlockSpec((rs, BRANCH_W), col(COL_GB)),
            const((lb_logits.shape[0], BRANCH_W)),
            const((1, HGRN_D)),
            const((r, r)),
            const((r, r)),
        ],
        out_specs=pl.BlockSpec((rs, BRANCH_W), lambda b, t: (b * nt + t, 0)),
        out_shape=jax.ShapeDtypeStruct((bsz * s, BRANCH_W), BF16),
        scratch_shapes=[pltpu.VMEM((HGRN_HEADS, HGRN_D, HGRN_D), F32)],
        compiler_params=pltpu.CompilerParams(
            dimension_semantics=("parallel", "arbitrary"), vmem_limit_bytes=VMEM_LIMIT),
        name="hgrn",
    )(proj, proj, proj, proj, lb_logits, norm_w.reshape(1, HGRN_D), tril.astype(BF16), tril)


def _xattn_body(q_ref, z_ref, mem_ref, mw_ref, wkv_ref, o_ref, km_ref, vm_ref):
    @pl.when(pl.program_id(1) == 0)
    def _():
        m = mem_ref[0]
        mn = m * lax.rsqrt(jnp.mean(m * m, axis=-1, keepdims=True) + EPS) * mw_ref[...]
        kv = jnp.dot(mn.astype(BF16), wkv_ref[...], preferred_element_type=F32)
        km_ref[...] = kv[:, :BRANCH_W].astype(BF16)
        vm_ref[...] = kv[:, BRANCH_W:].astype(BF16)

    for h in range(XA_HEADS):
        cols = slice(h * XA_HD, (h + 1) * XA_HD)
        logits = lax.dot_general(q_ref[:, cols], km_ref[:, cols], (((1,), (1,)), ((), ())),
                                 preferred_element_type=F32) * (XA_HD ** -0.5)
        mx = jnp.max(logits, axis=-1, keepdims=True)
        p = jnp.exp(logits - mx)
        den = jnp.sum(p, axis=-1, keepdims=True)
        o = jnp.dot(p.astype(BF16), vm_ref[:, cols], preferred_element_type=F32) / den
        o_ref[:, cols] = (o * _silu(z_ref[:, cols].astype(F32))).astype(o_ref.dtype)


def _xattn(proj, mem, mem_norm_w, wkv_bf16, bsz, s, tq=1024):
    n_mem, d = mem.shape[1], mem.shape[2]
    nq = s // tq
    return pl.pallas_call(
        _xattn_body,
        grid=(bsz, nq),
        in_specs=[
            pl.BlockSpec((tq, BRANCH_W), lambda b, i: (b * nq + i, COL_QC // BRANCH_W)),
            pl.BlockSpec((tq, BRANCH_W), lambda b, i: (b * nq + i, COL_ZC // BRANCH_W)),
            pl.BlockSpec((1, n_mem, d), lambda b, i: (b, 0, 0)),
            pl.BlockSpec((1, d), lambda b, i: (0, 0)),
            pl.BlockSpec((d, 2 * BRANCH_W), lambda b, i: (0, 0)),
        ],
        out_specs=pl.BlockSpec((tq, BRANCH_W), lambda b, i: (b * nq + i, 0)),
        out_shape=jax.ShapeDtypeStruct((bsz * s, BRANCH_W), BF16),
        scratch_shapes=[pltpu.VMEM((n_mem, BRANCH_W), BF16), pltpu.VMEM((n_mem, BRANCH_W), BF16)],
        compiler_params=pltpu.CompilerParams(
            dimension_semantics=("parallel", "arbitrary"), vmem_limit_bytes=VMEM_LIMIT),
        name="xattn",
    )(proj, proj, mem, mem_norm_w.reshape(1, d), wkv_bf16)


def _merge_body(x_ref, ga_ref, gb_ref, gc_ref, ya_ref, yb_ref, yc_ref,
                wa_ref, wb_ref, wc_ref, wo_ref, pw_ref, o_ref):
    tm = x_ref.shape[0]
    halves = [slice(0, tm // 2), slice(tm // 2, tm)]
    branches = ((ga_ref, ya_ref, wa_ref), (gb_ref, yb_ref, wb_ref), (gc_ref, yc_ref, wc_ref))
    proj = [[jnp.dot(y_ref[rows, :], w_ref[...], preferred_element_type=F32) for _, y_ref, w_ref in branches]
            for rows in halves]
    gated = [[_sigmoid(g_ref[rows, :].astype(F32)) * proj[k][n] for n, (g_ref, _, _) in enumerate(branches)]
             for k, rows in enumerate(halves)]
    merged = [gated[k][0] + gated[k][1] + gated[k][2] for k in range(2)]
    y = [jnp.dot(merged[k].astype(BF16), wo_ref[...], preferred_element_type=F32) for k in range(2)]
    for k, rows in enumerate(halves):
        yn = y[k] * lax.rsqrt(jnp.mean(y[k] * y[k], axis=-1, keepdims=True) + EPS) * pw_ref[...]
        o_ref[rows, :] = x_ref[rows, :] + yn


def _merge(x2, proj, ya, yb, yc, wa, wb, wc, wo, post_w, tm=1024):
    t, d = x2.shape
    row = lambda c: (lambda i: (i, c))
    const = lambda shape: pl.BlockSpec(shape, lambda i: (0, 0))
    return pl.pallas_call(
        _merge_body,
        grid=(t // tm,),
        in_specs=[
            pl.BlockSpec((tm, d), row(0)),
            pl.BlockSpec((tm, d), row(COL_GATE_A // d)),
            pl.BlockSpec((tm, d), row(COL_GATE_B // d)),
            pl.BlockSpec((tm, d), row(COL_GATE_C // d)),
            pl.BlockSpec((tm, BRANCH_W), row(0)),
            pl.BlockSpec((tm, BRANCH_W), row(0)),
            pl.BlockSpec((tm, BRANCH_W), row(0)),
            const((BRANCH_W, d)), const((BRANCH_W, d)), const((BRANCH_W, d)),
            const((d, d)), const((1, d)),
        ],
        out_specs=pl.BlockSpec((tm, d), row(0)),
        out_shape=jax.ShapeDtypeStruct((t, d), F32),
        compiler_params=pltpu.CompilerParams(
            dimension_semantics=("parallel",), vmem_limit_bytes=VMEM_LIMIT),
        name="merge",
    )(x2, proj, proj, proj, ya, yb, yc, wa, wb, wc, wo, post_w.reshape(1, d))


def kernel(x, mem, pre_norm_w, w_in, hgrn_lb_logits, hgrn_norm_w, mem_norm_w, w_mem_kv,
           w_branch_a, w_branch_b, w_branch_c, w_out, post_norm_w):
    bsz, s, d = x.shape
    assert w_in.shape[0] == 1 and w_in.shape[2] == PROJ_TOTAL and d == 1024
    assert s % MOBA_BLOCK == 0 and s % (HGRN_TILE * HGRN_SUB) == 0
    x2 = x.reshape(bsz * s, d)
    proj = _in_proj(x2, pre_norm_w[0], w_in[0].astype(BF16))
    slopes = jnp.exp2(-8.0 * jnp.arange(1, MOBA_HEADS + 1, dtype=F32) / MOBA_HEADS)
    ya = _moba(proj, slopes, bsz, s)
    yb = _hgrn(proj, hgrn_lb_logits, hgrn_norm_w[0], bsz, s)
    yc = _xattn(proj, mem, mem_norm_w[0], w_mem_kv[0].astype(BF16), bsz, s)
    out = _merge(x2, proj, ya, yb, yc, w_branch_a[0].astype(BF16), w_branch_b[0].astype(BF16),
                 w_branch_c[0].astype(BF16), w_out[0].astype(BF16), post_norm_w[0])
    return out.reshape(bsz, s, d)
```

```python
import jax
import jax.numpy as jnp
import numpy as np
from jax import lax
from jax.experimental import pallas as pl
from jax.experimental.pallas import tpu as pltpu

F32 = jnp.float32
BF16 = jnp.bfloat16
EPS = 1e-6

MOBA_HEADS, MOBA_HD, MOBA_BLOCK, MOBA_TOPK = 8, 64, 256, 3
HGRN_HEADS, HGRN_D, HGRN_CHUNK = 4, 128, 32
XA_HEADS, XA_HD = 4, 128
BRANCH_W = 512
LANES = 128
COL_QA, COL_KA, COL_VA, COL_ZA = 0, 512, 1024, 1536
COL_FB, COL_IB, COL_QB, COL_GB = 2048, 2560, 3072, 3584
COL_QC, COL_ZC = 4096, 4608
COL_GATE_A, COL_GATE_B, COL_GATE_C = 5120, 6144, 7168
PROJ_TOTAL = COL_GATE_C + 1024

NEG_BIG = -1e30
LOG2E = 1.4426950408889634
MOBA_VROWS = MOBA_HD + 16
VMEM_LIMIT = 56 * 1024 * 1024


def _sigmoid(z):
    return 1.0 / (1.0 + jnp.exp2(z * (-LOG2E)))


def _silu(z):
    return z * _sigmoid(z)


def _split3(a):
    hi = a.astype(BF16).astype(F32)
    r = a - hi
    mid = r.astype(BF16).astype(F32)
    lo = (r - mid).astype(BF16).astype(F32)
    return hi, mid, lo


IN_PROJ_CHUNK = 1024


def _in_proj_body(x_ref, nw_ref, w_ref, o_ref):
    x = x_ref[...]
    ms = jnp.mean(x * x, axis=-1, keepdims=True)
    h = (x * lax.rsqrt(ms + EPS) * nw_ref[...]).astype(BF16)
    for c0 in range(0, o_ref.shape[1], IN_PROJ_CHUNK):
        cols = slice(c0, c0 + IN_PROJ_CHUNK)
        o_ref[:, cols] = jnp.dot(h, w_ref[:, cols], preferred_element_type=F32).astype(o_ref.dtype)


def _in_proj(x2, norm_w, w_bf16, tm=512):
    t, d = x2.shape
    n = w_bf16.shape[1]
    return pl.pallas_call(
        _in_proj_body,
        grid=(t // tm,),
        in_specs=[
            pl.BlockSpec((tm, d), lambda i: (i, 0)),
            pl.BlockSpec((1, d), lambda i: (0, 0)),
            pl.BlockSpec((d, n), lambda i: (0, 0), pipeline_mode=pl.Buffered(1)),
        ],
        out_specs=pl.BlockSpec((tm, n), lambda i: (i, 0)),
        out_shape=jax.ShapeDtypeStruct((t, n), BF16),
        compiler_params=pltpu.CompilerParams(
            dimension_semantics=("parallel",), vmem_limit_bytes=VMEM_LIMIT),
        name="in_proj",
    )(x2, norm_w.reshape(1, d), w_bf16)


def _moba_constants(s, nb_lanes=16):
    blk = MOBA_BLOCK
    row = np.arange(s)
    off = (row % blk).astype(np.float32)
    onehot = (row[:, None] // blk == np.arange(nb_lanes)[None, :]).astype(np.float32)
    bias_lanes = np.concatenate([onehot, onehot, onehot, off[:, None], off[:, None], off[:, None],
                                 np.zeros((s, 64 - 3 * nb_lanes - 3), np.float32)], axis=1)
    zeros = np.zeros((s, 64), np.float32)
    a0 = np.concatenate([zeros, bias_lanes], axis=1)
    a1 = np.concatenate([bias_lanes, zeros], axis=1)
    aug = np.stack([a0, a1]).astype(BF16)
    avg = ((np.arange(nb_lanes)[:, None] == (row // blk)[None, :]).astype(np.float32) / blk).astype(BF16)
    causal = np.where(np.arange(blk)[:, None] <= np.arange(blk)[None, :], 0.0, -np.inf).astype(np.float32)
    return aug, avg, causal


def _moba_body(slopes_ref, q_ref, qn_ref, k_ref, v_ref, z_ref, aug_ref, avg_ref, causal_ref, o_ref,
               ka_ref, vt_ref, kbp_ref, qa_ref, qan_ref, m_ref, acc_ref, p_ref, al_ref):
    s = k_ref.shape[0]
    nb = s // MOBA_BLOCK
    blk = MOBA_BLOCK
    nh = MOBA_HEADS
    hd = MOBA_HD
    i = pl.program_id(1)

    def build_queries(src_ref, iq, dst_ref):
        nidx = lax.broadcasted_iota(jnp.int32, (16, blk), 0)
        qoff = lax.broadcasted_iota(jnp.int32, (16, blk), 1)
        dist0 = (qoff + (iq - nidx) * blk).astype(F32)
        qts = [(src_ref[:, hp * LANES:(hp + 1) * LANES].astype(F32) * (hd ** -0.5)).T
               for hp in range(nh // 2)]
        gates = [jnp.dot(kbp_ref[hp * 96:(hp + 1) * 96, :], qts[hp].astype(BF16),
                         preferred_element_type=F32) for hp in range(nh // 2)]
        for hp in range(nh // 2):
            qs = qts[hp] * LOG2E
            for e in range(2):
                h = 2 * hp + e
                gg = gates[hp]
                g = gg[e * 48:e * 48 + 16] + gg[e * 48 + 16:e * 48 + 32] + gg[e * 48 + 32:e * 48 + 48]
                cnt = jnp.zeros((16, blk), jnp.int32)
                for m in range(nb):
                    gm = g[m:m + 1, :]
                    ahead = (gm > g) | ((gm == g) & (m < nidx))
                    cnt = cnt + jnp.where(ahead, jnp.where(m < iq, 1, 0), 0)
                sel = ((nidx < iq) & (cnt < MOBA_TOPK)) | (nidx == iq)
                sl = slopes_ref[h] * LOG2E
                bias = jnp.where(sel, -sl * dist0, NEG_BIG)
                b_hi, b_mid, b_lo = _split3(bias)
                s_hi, s_mid, s_lo = _split3(jnp.full((16, blk), sl, F32))
                srow = jnp.where(nidx == 0, s_hi, jnp.where(nidx == 1, s_mid, jnp.where(nidx == 2, s_lo, 0.0)))
                qh = qs[0:64] if e == 0 else qs[64:128]
                parts = [qh, b_hi, b_mid, b_lo, srow] if e == 0 else [b_hi, b_mid, b_lo, srow, qh]
                dst_ref[h] = jnp.concatenate(parts, axis=0).astype(BF16)

    @pl.when(i == 0)
    def _():
        lane = lax.broadcasted_iota(jnp.int32, (1, LANES), 1)
        first_half = lane < 64
        kbar = jnp.dot(avg_ref[...], k_ref[...], preferred_element_type=F32)
        for h in range(nh):
            hp, e = h // 2, h % 2
            cols = slice(hp * LANES, (hp + 1) * LANES)
            keep = first_half if e == 0 else jnp.logical_not(first_half)
            for j in range(nb):
                rows = slice(j * blk, (j + 1) * blk)
                ka_ref[h, j] = jnp.where(keep, k_ref[rows, cols], aug_ref[e, rows, :])
            pieces = _split3(jnp.where(keep, kbar[:, cols], 0.0))
            for p in range(3):
                kbp_ref[(h * 3 + p) * 16:(h * 3 + p + 1) * 16, :] = pieces[p].astype(BF16)
        ones_row = jnp.where(lax.broadcasted_iota(jnp.int32, (MOBA_VROWS - hd, blk), 0) == 0, 1.0, 0.0)
        for j in range(nb):
            vt = v_ref[j * blk:(j + 1) * blk, :].astype(F32).T
            for h in range(nh):
                vt_ref[j, h, 0:hd, :] = vt[h * hd:(h + 1) * hd].astype(BF16)
                vt_ref[j, h, hd:MOBA_VROWS, :] = ones_row.astype(BF16)
        build_queries(q_ref, 0, qa_ref)

    @pl.when(i > 0)
    def _():
        qa_ref[...] = qan_ref[...]

    m_ref[...] = jnp.full(m_ref.shape, -jnp.inf, F32)
    acc_ref[...] = jnp.zeros(acc_ref.shape, F32)
    p_ref[...] = jnp.zeros(p_ref.shape, BF16)
    al_ref[...] = jnp.ones(al_ref.shape, F32)

    def step(j, masked):
        jm = jnp.maximum(j - 1, 0)
        scores = [jnp.dot(ka_ref[h, j], qa_ref[h], preferred_element_type=F32)
                  for h in range(nh)]
        pv = [jnp.dot(vt_ref[jm, h], p_ref[h], preferred_element_type=F32) for h in range(nh)]
        probs, alphas = [], []
        for h in range(nh):
            sT = scores[h]
            if masked:
                sT = sT + causal_ref[...]
            m_old = m_ref[h]
            m_new = jnp.maximum(m_old, jnp.max(sT, axis=0, keepdims=True))
            alphas.append(jnp.exp2(m_old - m_new))
            probs.append(jnp.exp2(sT - m_new).astype(BF16))
            m_ref[h] = m_new
        for h in range(nh):
            acc_ref[h] = al_ref[h] * acc_ref[h] + pv[h]
        return probs, alphas

    def past(j, carry):
        probs, alphas = step(j, False)
        for h in range(nh):
            p_ref[h] = probs[h]
            al_ref[h] = alphas[h]
        return carry

    lax.fori_loop(0, i, past, 0)
    build_queries(qn_ref, i + 1, qan_ref)
    probs, alphas = step(i, True)
    pv = [jnp.dot(vt_ref[i, h], probs[h], preferred_element_type=F32) for h in range(nh)]
    outs = []
    for h in range(nh):
        acc = alphas[h] * acc_ref[h] + pv[h]
        outs.append(acc[0:hd] / acc[hd:hd + 1])
    ot = jnp.concatenate(outs, axis=0)
    o_ref[...] = (ot.T * _silu(z_ref[...].astype(F32))).astype(o_ref.dtype)


def _moba(proj, slopes, bsz, s):
    nb = s // MOBA_BLOCK
    assert nb <= 16
    blk = MOBA_BLOCK
    aug, avg, causal = _moba_constants(s)
    qblock = lambda c0: (lambda b, i, *_: (b * nb + i, c0 // BRANCH_W))
    qnext = lambda b, i, *_: (b * nb + jnp.minimum(i + 1, nb - 1), COL_QA // BRANCH_W)
    whole = lambda c0: (lambda b, i, *_: (b, c0 // BRANCH_W))
    return pl.pallas_call(
        _moba_body,
        grid_spec=pltpu.PrefetchScalarGridSpec(
            num_scalar_prefetch=1,
            grid=(bsz, nb),
            in_specs=[
                pl.BlockSpec((blk, BRANCH_W), qblock(COL_QA)),
                pl.BlockSpec((blk, BRANCH_W), qnext),
                pl.BlockSpec((s, BRANCH_W), whole(COL_KA)),
                pl.BlockSpec((s, BRANCH_W), whole(COL_VA)),
                pl.BlockSpec((blk, BRANCH_W), qblock(COL_ZA)),
                pl.BlockSpec((2, s, LANES), lambda b, i, *_: (0, 0, 0)),
                pl.BlockSpec((16, s), lambda b, i, *_: (0, 0)),
                pl.BlockSpec((blk, blk), lambda b, i, *_: (0, 0)),
            ],
            out_specs=pl.BlockSpec((blk, BRANCH_W), lambda b, i, *_: (b * nb + i, 0)),
            scratch_shapes=[
                pltpu.VMEM((MOBA_HEADS, nb, blk, LANES), BF16),
                pltpu.VMEM((nb, MOBA_HEADS, MOBA_VROWS, blk), BF16),
                pltpu.VMEM((MOBA_HEADS * 48, LANES), BF16),
                pltpu.VMEM((MOBA_HEADS, LANES, blk), BF16),
                pltpu.VMEM((MOBA_HEADS, LANES, blk), BF16),
                pltpu.VMEM((MOBA_HEADS, 1, blk), F32),
                pltpu.VMEM((MOBA_HEADS, MOBA_VROWS, blk), F32),
                pltpu.VMEM((MOBA_HEADS, blk, blk), BF16),
                pltpu.VMEM((MOBA_HEADS, 1, blk), F32),
            ],
        ),
        out_shape=jax.ShapeDtypeStruct((bsz * s, BRANCH_W), BF16),
        compiler_params=pltpu.CompilerParams(
            dimension_semantics=("parallel", "arbitrary"), vmem_limit_bytes=VMEM_LIMIT),
        name="moba",
    )(slopes, proj, proj, proj, proj, proj, aug, avg, causal)


HGRN_TILE = 256
HGRN_SUB = 2


def _hgrn_body(f_ref, i_ref, q_ref, g_ref, lbl_ref, nw_ref, lmat_ref, tril_ref, o_ref, st_ref):
    r, c, d = HGRN_TILE, HGRN_CHUNK, HGRN_D
    n_chunks = r // c
    heads = range(HGRN_HEADS)
    tiles = range(HGRN_SUB)
    chunks = range(n_chunks)
    cols = [slice(h * d, (h + 1) * d) for h in heads]
    rows = [slice(t * r, (t + 1) * r) for t in tiles]
    nt_dims = (((1,), (1,)), ((), ()))

    @pl.when(pl.program_id(1) == 0)
    def _():
        st_ref[...] = jnp.zeros_like(st_ref)

    logits = lbl_ref[...]
    ex = jnp.exp(logits - jnp.max(logits, axis=0, keepdims=True))
    lb = ex[0:1, :] / jnp.sum(ex, axis=0, keepdims=True)

    kk, b = [], []
    for t in tiles:
        fl = f_ref[rows[t], :].astype(F32)
        log_f = jnp.log(lb + (1.0 - lb) * _sigmoid(fl))
        kk.append((1.0 - lb) * _sigmoid(-fl))
        lmat = lmat_ref[...]
        hi, mid, lo = [jnp.dot(lmat, piece.astype(BF16), preferred_element_type=F32)
                       for piece in _split3(log_f)]
        b.append(hi + mid + lo)

    q_t, k_t32, k_t, dec = [], [], [], []
    for t in tiles:
        q_t.append((q_ref[rows[t], :].astype(F32) * jnp.exp(b[t])).astype(BF16))
        kt = kk[t] * jnp.exp(-b[t])
        k_t32.append(kt)
        k_t.append(kt.astype(BF16))
        dec.append([jnp.exp(b[t][ci * c + c - 1:ci * c + c, :]) for ci in chunks])

    a = [[lax.dot_general(q_t[t][:, cols[h]], k_t[t][:, cols[h]], nt_dims, preferred_element_type=F32)
          for h in heads] for t in tiles]
    u_t = []
    for t in tiles:
        per_head = []
        for h in heads:
            vt = i_ref[rows[t], cols[h]].astype(F32).T.astype(BF16)
            blocks = []
            for ci in chunks:
                slab = (k_t32[t][ci * c:(ci + 1) * c, cols[h]] * dec[t][ci][:, cols[h]]).astype(BF16)
                pieces = ([jnp.zeros((c, ci * d), BF16)] if ci else []) + [slab]
                if ci < n_chunks - 1:
                    pieces.append(jnp.zeros((c, (n_chunks - 1 - ci) * d), BF16))
                blocks.append(jnp.concatenate(pieces, axis=1))
            kd_blk = jnp.concatenate(blocks, axis=0)
            per_head.append(jnp.dot(vt, kd_blk, preferred_element_type=F32))
        u_t.append(per_head)
    o_intra = [[jnp.dot((a[t][h] * tril_ref[...]).astype(BF16), i_ref[rows[t], cols[h]],
                        preferred_element_type=F32) for h in heads] for t in tiles]

    states = [[None] * HGRN_HEADS for _ in tiles]
    for h in heads:
        st = st_ref[h]
        for t in tiles:
            before = []
            for ci in chunks:
                before.append(st.astype(BF16))
                st = st * dec[t][ci][:, cols[h]] + u_t[t][h][:, ci * d:(ci + 1) * d]
            states[t][h] = before
        st_ref[h] = st

    for t in tiles:
        for h in heads:
            o_inter = [lax.dot_general(q_t[t][ci * c:(ci + 1) * c, cols[h]], states[t][h][ci], nt_dims,
                                       preferred_element_type=F32) for ci in chunks]
            o = o_intra[t][h] + jnp.concatenate(o_inter, axis=0)
            on = o * lax.rsqrt(jnp.mean(o * o, axis=-1, keepdims=True) + EPS) * nw_ref[...]
            o_ref[rows[t], cols[h]] = (on * _silu(g_ref[rows[t], cols[h]].astype(F32))).astype(o_ref.dtype)


def _hgrn(proj, lb_logits, norm_w, bsz, s):
    r, c = HGRN_TILE, HGRN_CHUNK
    rs = r * HGRN_SUB
    nt = s // rs
    ri = np.arange(r)
    same = (ri[:, None] // c) == (ri[None, :] // c)
    tril = (same & (ri[None, :] <= ri[:, None])).astype(np.float32)
    col = lambda c0: (lambda b, t: (b * nt + t, c0 // BRANCH_W))
    const = lambda shape: pl.BlockSpec(shape, lambda b, t: (0,) * len(shape))
    return pl.pallas_call(
        _hgrn_body,
        grid=(bsz, nt),
        in_specs=[
            pl.BlockSpec((rs, BRANCH_W), col(COL_FB)),
            pl.BlockSpec((rs, BRANCH_W), col(COL_IB)),
            pl.BlockSpec((rs, BRANCH_W), col(COL_QB)),
            pl.BlockSpec((rs, BRANCH_W), col(COL_GB)),
            const((lb_logits.shape[0], BRANCH_W)),
            const((1, HGRN_D)),
            const((r, r)),
            const((r, r)),
        ],
        out_specs=pl.BlockSpec((rs, BRANCH_W), lambda b, t: (b * nt + t, 0)),
        out_shape=jax.ShapeDtypeStruct((bsz * s, BRANCH_W), BF16),
        scratch_shapes=[pltpu.VMEM((HGRN_HEADS, HGRN_D, HGRN_D), F32)],
        compiler_params=pltpu.CompilerParams(
            dimension_semantics=("parallel", "arbitrary"), vmem_limit_bytes=VMEM_LIMIT),
        name="hgrn",
    )(proj, proj, proj, proj, lb_logits, norm_w.reshape(1, HGRN_D), tril.astype(BF16), tril)


def _xattn_body(q_ref, z_ref, mem_ref, mw_ref, wkv_ref, o_ref, km_ref, vm_ref):
    @pl.when(pl.program_id(1) == 0)
    def _():
        m = mem_ref[0]
        mn = m * lax.rsqrt(jnp.mean(m * m, axis=-1, keepdims=True) + EPS) * mw_ref[...]
        kv = jnp.dot(mn.astype(BF16), wkv_ref[...], preferred_element_type=F32)
        km_ref[...] = kv[:, :BRANCH_W].astype(BF16)
        vm_ref[...] = kv[:, BRANCH_W:].astype(BF16)

    for h in range(XA_HEADS):
        cols = slice(h * XA_HD, (h + 1) * XA_HD)
        logits = lax.dot_general(q_ref[:, cols], km_ref[:, cols], (((1,), (1,)), ((), ())),
                                 preferred_element_type=F32) * (XA_HD ** -0.5)
        mx = jnp.max(logits, axis=-1, keepdims=True)
        p = jnp.exp(logits - mx)
        den = jnp.sum(p, axis=-1, keepdims=True)
        o = jnp.dot(p.astype(BF16), vm_ref[:, cols], preferred_element_type=F32) / den
        o_ref[:, cols] = (o * _silu(z_ref[:, cols].astype(F32))).astype(o_ref.dtype)


def _xattn(proj, mem, mem_norm_w, wkv_bf16, bsz, s, tq=1024):
    n_mem, d = mem.shape[1], mem.shape[2]
    nq = s // tq
    return pl.pallas_call(
        _xattn_body,
        grid=(bsz, nq),
        in_specs=[
            pl.BlockSpec((tq, BRANCH_W), lambda b, i: (b * nq + i, COL_QC // BRANCH_W)),
            pl.BlockSpec((tq, BRANCH_W), lambda b, i: (b * nq + i, COL_ZC // BRANCH_W)),
            pl.BlockSpec((1, n_mem, d), lambda b, i: (b, 0, 0)),
            pl.BlockSpec((1, d), lambda b, i: (0, 0)),
            pl.BlockSpec((d, 2 * BRANCH_W), lambda b, i: (0, 0)),
        ],
        out_specs=pl.BlockSpec((tq, BRANCH_W), lambda b, i: (b * nq + i, 0)),
        out_shape=jax.ShapeDtypeStruct((bsz * s, BRANCH_W), BF16),
        scratch_shapes=[pltpu.VMEM((n_mem, BRANCH_W), BF16), pltpu.VMEM((n_mem, BRANCH_W), BF16)],
        compiler_params=pltpu.CompilerParams(
            dimension_semantics=("parallel", "arbitrary"), vmem_limit_bytes=VMEM_LIMIT),
        name="xattn",
    )(proj, proj, mem, mem_norm_w.reshape(1, d), wkv_bf16)


def _merge_body(x_ref, ga_ref, gb_ref, gc_ref, ya_ref, yb_ref, yc_ref,
                wa_ref, wb_ref, wc_ref, wo_ref, pw_ref, o_ref):
    tm = x_ref.shape[0]
    halves = [slice(0, tm // 2), slice(tm // 2, tm)]
    branches = ((ga_ref, ya_ref, wa_ref), (gb_ref, yb_ref, wb_ref), (gc_ref, yc_ref, wc_ref))
    proj = [[jnp.dot(y_ref[rows, :], w_ref[...], preferred_element_type=F32) for _, y_ref, w_ref in branches]
            for rows in halves]
    gated = [[_sigmoid(g_ref[rows, :].astype(F32)) * proj[k][n] for n, (g_ref, _, _) in enumerate(branches)]
             for k, rows in enumerate(halves)]
    merged = [gated[k][0] + gated[k][1] + gated[k][2] for k in range(2)]
    y = [jnp.dot(merged[k].astype(BF16), wo_ref[...], preferred_element_type=F32) for k in range(2)]
    for k, rows in enumerate(halves):
        yn = y[k] * lax.rsqrt(jnp.mean(y[k] * y[k], axis=-1, keepdims=True) + EPS) * pw_ref[...]
        o_ref[rows, :] = x_ref[rows, :] + yn


def _merge(x2, proj, ya, yb, yc, wa, wb, wc, wo, post_w, tm=1024):
    t, d = x2.shape
    row = lambda c: (lambda i: (i, c))
    const = lambda shape: pl.BlockSpec(shape, lambda i: (0, 0))
    return pl.pallas_call(
        _merge_body,
        grid=(t // tm,),
        in_specs=[
            pl.BlockSpec((tm, d), row(0)),
            pl.BlockSpec((tm, d), row(COL_GATE_A // d)),
            pl.BlockSpec((tm, d), row(COL_GATE_B // d)),
            pl.BlockSpec((tm, d), row(COL_GATE_C // d)),
            pl.BlockSpec((tm, BRANCH_W), row(0)),
            pl.BlockSpec((tm, BRANCH_W), row(0)),
            pl.BlockSpec((tm, BRANCH_W), row(0)),
            const((BRANCH_W, d)), const((BRANCH_W, d)), const((BRANCH_W, d)),
            const((d, d)), const((1, d)),
        ],
        out_specs=pl.BlockSpec((tm, d), row(0)),
        out_shape=jax.ShapeDtypeStruct((t, d), F32),
        compiler_params=pltpu.CompilerParams(
            dimension_semantics=("parallel",), vmem_limit_bytes=VMEM_LIMIT),
        name="merge",
    )(x2, proj, proj, proj, ya, yb, yc, wa, wb, wc, wo, post_w.reshape(1, d))


def kernel(x, mem, pre_norm_w, w_in, hgrn_lb_logits, hgrn_norm_w, mem_norm_w, w_mem_kv,
           w_branch_a, w_branch_b, w_branch_c, w_out, post_norm_w):
    bsz, s, d = x.shape
    assert w_in.shape[0] == 1 and w_in.shape[2] == PROJ_TOTAL and d == 1024
    assert s % MOBA_BLOCK == 0 and s % (HGRN_TILE * HGRN_SUB) == 0
    x2 = x.reshape(bsz * s, d)
    proj = _in_proj(x2, pre_norm_w[0], w_in[0].astype(BF16))
    slopes = jnp.exp2(-8.0 * jnp.arange(1, MOBA_HEADS + 1, dtype=F32) / MOBA_HEADS)
    ya = _moba(proj, slopes, bsz, s)
    yb = _hgrn(proj, hgrn_lb_logits, hgrn_norm_w[0], bsz, s)
    yc = _xattn(proj, mem, mem_norm_w[0], w_mem_kv[0].astype(BF16), bsz, s)
    out = _merge(x2, proj, ya, yb, yc, w_branch_a[0].astype(BF16), w_branch_b[0].astype(BF16),
                 w_branch_c[0].astype(BF16), w_out[0].astype(BF16), post_norm_w[0])
    return out.reshape(bsz, s, d)
```

```python
import jax
import jax.numpy as jnp
import numpy as np
from jax import lax
from jax.experimental import pallas as pl
from jax.experimental.pallas import tpu as pltpu

F32 = jnp.float32
BF16 = jnp.bfloat16
EPS = 1e-6

MOBA_HEADS, MOBA_HD, MOBA_BLOCK, MOBA_TOPK = 8, 64, 256, 3
HGRN_HEADS, HGRN_D, HGRN_CHUNK = 4, 128, 32
XA_HEADS, XA_HD = 4, 128
BRANCH_W = 512
LANES = 128
COL_QA, COL_KA, COL_VA, COL_ZA = 0, 512, 1024, 1536
COL_FB, COL_IB, COL_QB, COL_GB = 2048, 2560, 3072, 3584
COL_QC, COL_ZC = 4096, 4608
COL_GATE_A, COL_GATE_B, COL_GATE_C = 5120, 6144, 7168
PROJ_TOTAL = COL_GATE_C + 1024

NEG_BIG = -1e30
LOG2E = 1.4426950408889634
MOBA_VROWS = MOBA_HD + 16
VMEM_LIMIT = 56 * 1024 * 1024


def _sigmoid(z):
    return 1.0 / (1.0 + jnp.exp2(z * (-LOG2E)))


def _silu(z):
    return z * _sigmoid(z)


def _split3(a):
    hi = a.astype(BF16).astype(F32)
    r = a - hi
    mid = r.astype(BF16).astype(F32)
    lo = (r - mid).astype(BF16).astype(F32)
    return hi, mid, lo


IN_PROJ_CHUNK = 1024


def _in_proj_body(x_ref, nw_ref, w_ref, o_ref):
    x = x_ref[...]
    ms = jnp.mean(x * x, axis=-1, keepdims=True)
    h = (x * lax.rsqrt(ms + EPS) * nw_ref[...]).astype(BF16)
    for c0 in range(0, o_ref.shape[1], IN_PROJ_CHUNK):
        cols = slice(c0, c0 + IN_PROJ_CHUNK)
        o_ref[:, cols] = jnp.dot(h, w_ref[:, cols], preferred_element_type=F32).astype(o_ref.dtype)


def _in_proj(x2, norm_w, w_bf16, tm=512):
    t, d = x2.shape
    n = w_bf16.shape[1]
    return pl.pallas_call(
        _in_proj_body,
        grid=(t // tm,),
        in_specs=[
            pl.BlockSpec((tm, d), lambda i: (i, 0)),
            pl.BlockSpec((1, d), lambda i: (0, 0)),
            pl.BlockSpec((d, n), lambda i: (0, 0), pipeline_mode=pl.Buffered(1)),
        ],
        out_specs=pl.BlockSpec((tm, n), lambda i: (i, 0)),
        out_shape=jax.ShapeDtypeStruct((t, n), BF16),
        compiler_params=pltpu.CompilerParams(
            dimension_semantics=("parallel",), vmem_limit_bytes=VMEM_LIMIT),
        name="in_proj",
    )(x2, norm_w.reshape(1, d), w_bf16)


def _moba_constants(s, nb_lanes=16):
    blk = MOBA_BLOCK
    row = np.arange(s)
    off = (row % blk).astype(np.float32)
    onehot = (row[:, None] // blk == np.arange(nb_lanes)[None, :]).astype(np.float32)
    bias_lanes = np.concatenate([onehot, onehot, onehot, off[:, None], off[:, None], off[:, None],
                                 np.zeros((s, 64 - 3 * nb_lanes - 3), np.float32)], axis=1)
    zeros = np.zeros((s, 64), np.float32)
    a0 = np.concatenate([zeros, bias_lanes], axis=1)
    a1 = np.concatenate([bias_lanes, zeros], axis=1)
    aug = np.stack([a0, a1]).astype(BF16)
    avg = ((np.arange(nb_lanes)[:, None] == (row // blk)[None, :]).astype(np.float32) / blk).astype(BF16)
    causal = np.where(np.arange(blk)[:, None] <= np.arange(blk)[None, :], 0.0, -np.inf).astype(np.float32)
    mask = np.concatenate([np.zeros((blk, blk), np.float32), causal])
    return aug, avg, mask


def _moba_body(slopes_ref, q_ref, qn_ref, k_ref, v_ref, z_ref, aug_ref, avg_ref, mask_ref, o_ref,
               ka_ref, vt_ref, kbp_ref, qa_ref, qan_ref, m_ref, acc_ref, p_ref, al_ref):
    s = k_ref.shape[0]
    nb = s // MOBA_BLOCK
    blk = MOBA_BLOCK
    nh = MOBA_HEADS
    hd = MOBA_HD
    i = pl.program_id(1)

    def build_queries(src_ref, iq, dst_ref):
        nidx = lax.broadcasted_iota(jnp.int32, (16, blk), 0)
        qoff = lax.broadcasted_iota(jnp.int32, (16, blk), 1)
        dist0 = (qoff + (iq - nidx) * blk).astype(F32)
        qts = [(src_ref[:, hp * LANES:(hp + 1) * LANES].astype(F32) * (hd ** -0.5)).T
               for hp in range(nh // 2)]
        gates = [jnp.dot(kbp_ref[hp * 96:(hp + 1) * 96, :], qts[hp].astype(BF16),
                         preferred_element_type=F32) for hp in range(nh // 2)]
        for hp in range(nh // 2):
            qs = qts[hp] * LOG2E
            for e in range(2):
                h = 2 * hp + e
                gg = gates[hp]
                g = gg[e * 48:e * 48 + 16] + gg[e * 48 + 16:e * 48 + 32] + gg[e * 48 + 32:e * 48 + 48]
                cnt = jnp.zeros((16, blk), jnp.int32)
                for m in range(nb):
                    gm = g[m:m + 1, :]
                    ahead = (gm > g) | ((gm == g) & (m < nidx))
                    cnt = cnt + jnp.where(ahead, jnp.where(m < iq, 1, 0), 0)
                sel = ((nidx < iq) & (cnt < MOBA_TOPK)) | (nidx == iq)
                sl = slopes_ref[h] * LOG2E
                bias = jnp.where(sel, -sl * dist0, NEG_BIG)
                b_hi, b_mid, b_lo = _split3(bias)
                s_hi, s_mid, s_lo = _split3(jnp.full((16, blk), sl, F32))
                srow = jnp.where(nidx == 0, s_hi, jnp.where(nidx == 1, s_mid, jnp.where(nidx == 2, s_lo, 0.0)))
                qh = qs[0:64] if e == 0 else qs[64:128]
                parts = [qh, b_hi, b_mid, b_lo, srow] if e == 0 else [b_hi, b_mid, b_lo, srow, qh]
                dst_ref[h] = jnp.concatenate(parts, axis=0).astype(BF16)

    @pl.when(i == 0)
    def _():
        lane = lax.broadcasted_iota(jnp.int32, (1, LANES), 1)
        first_half = lane < 64
        kbar = jnp.dot(avg_ref[...], k_ref[...], preferred_element_type=F32)
        for h in range(nh):
            hp, e = h // 2, h % 2
            cols = slice(hp * LANES, (hp + 1) * LANES)
            keep = first_half if e == 0 else jnp.logical_not(first_half)
            for j in range(nb):
                rows = slice(j * blk, (j + 1) * blk)
                ka_ref[h, j] = jnp.where(keep, k_ref[rows, cols], aug_ref[e, rows, :])
            pieces = _split3(jnp.where(keep, kbar[:, cols], 0.0))
            for p in range(3):
                kbp_ref[(h * 3 + p) * 16:(h * 3 + p + 1) * 16, :] = pieces[p].astype(BF16)
        ones_row = jnp.where(lax.broadcasted_iota(jnp.int32, (MOBA_VROWS - hd, blk), 0) == 0, 1.0, 0.0)
        for j in range(nb):
            vt = v_ref[j * blk:(j + 1) * blk, :].astype(F32).T
            for h in range(nh):
                vt_ref[j, h, 0:hd, :] = vt[h * hd:(h + 1) * hd].astype(BF16)
                vt_ref[j, h, hd:MOBA_VROWS, :] = ones_row.astype(BF16)
        build_queries(q_ref, 0, qa_ref)
        p_ref[...] = jnp.zeros(p_ref.shape, BF16)
        al_ref[...] = jnp.ones(al_ref.shape, F32)

    @pl.when(i > 0)
    def _():
        qa_ref[...] = qan_ref[...]

    m_ref[...] = jnp.full(m_ref.shape, -jnp.inf, F32)
    acc_ref[...] = jnp.zeros(acc_ref.shape, F32)

    def softmax(h, sT):
        m_old = m_ref[h]
        m_new = jnp.maximum(m_old, jnp.max(sT, axis=0, keepdims=True))
        m_ref[h] = m_new
        return jnp.exp2(sT - m_new).astype(BF16), jnp.exp2(m_old - m_new)

    def pair_scores(j0):
        return [jnp.dot(ka_ref[h, pl.ds(j0, 2)].reshape(2 * blk, LANES), qa_ref[h],
                        preferred_element_type=F32) for h in range(nh)]

    def pair_values(ja, jb, h):
        return jnp.concatenate([vt_ref[ja, h], vt_ref[jb, h]], axis=1)

    def pending_pv(j0):
        pa, pb = jnp.maximum(j0 - 2, 0), jnp.maximum(j0 - 1, 0)
        return [jnp.dot(pair_values(pa, pb, h), p_ref[h], preferred_element_type=F32) for h in range(nh)]

    def past(k, carry):
        scores = pair_scores(2 * k)
        pv = pending_pv(2 * k)
        sm = [softmax(h, scores[h]) for h in range(nh)]
        for h in range(nh):
            acc_ref[h] = al_ref[h] * acc_ref[h] + pv[h]
        for h in range(nh):
            p_ref[h], al_ref[h] = sm[h]
        return carry

    npairs = i // 2
    lax.fori_loop(0, npairs, past, 0)
    build_queries(qn_ref, i + 1, qan_ref)

    def finish(scores, last_values):
        pv = pending_pv(2 * npairs)
        sm = [softmax(h, scores[h]) for h in range(nh)]
        last = [jnp.dot(last_values(h), sm[h][0], preferred_element_type=F32) for h in range(nh)]
        outs = []
        for h in range(nh):
            acc = sm[h][1] * (al_ref[h] * acc_ref[h] + pv[h]) + last[h]
            outs.append(acc[0:hd] / acc[hd:hd + 1])
        ot = jnp.concatenate(outs, axis=0)
        o_ref[...] = (ot.T * _silu(z_ref[...].astype(F32))).astype(o_ref.dtype)

    @pl.when(i % 2 == 0)
    def _():
        finish([jnp.dot(ka_ref[h, i], qa_ref[h], preferred_element_type=F32) + mask_ref[blk:2 * blk, :]
                for h in range(nh)], lambda h: vt_ref[i, h])

    @pl.when(i % 2 == 1)
    def _():
        scores = pair_scores(i - 1)
        finish([scores[h] + mask_ref[...] for h in range(nh)], lambda h: pair_values(i - 1, i, h))


def _moba(proj, slopes, bsz, s):
    nb = s // MOBA_BLOCK
    assert nb <= 16
    blk = MOBA_BLOCK
    aug, avg, mask = _moba_constants(s)
    qblock = lambda c0: (lambda b, i, *_: (b * nb + i, c0 // BRANCH_W))
    qnext = lambda b, i, *_: (b * nb + jnp.minimum(i + 1, nb - 1), COL_QA // BRANCH_W)
    whole = lambda c0: (lambda b, i, *_: (b, c0 // BRANCH_W))
    return pl.pallas_call(
        _moba_body,
        grid_spec=pltpu.PrefetchScalarGridSpec(
            num_scalar_prefetch=1,
            grid=(bsz, nb),
            in_specs=[
                pl.BlockSpec((blk, BRANCH_W), qblock(COL_QA)),
                pl.BlockSpec((blk, BRANCH_W), qnext),
                pl.BlockSpec((s, BRANCH_W), whole(COL_KA)),
                pl.BlockSpec((s, BRANCH_W), whole(COL_VA)),
                pl.BlockSpec((blk, BRANCH_W), qblock(COL_ZA)),
                pl.BlockSpec((2, s, LANES), lambda b, i, *_: (0, 0, 0)),
                pl.BlockSpec((16, s), lambda b, i, *_: (0, 0)),
                pl.BlockSpec((2 * blk, blk), lambda b, i, *_: (0, 0)),
            ],
            out_specs=pl.BlockSpec((blk, BRANCH_W), lambda b, i, *_: (b * nb + i, 0)),
            scratch_shapes=[
                pltpu.VMEM((MOBA_HEADS, nb, blk, LANES), BF16),
                pltpu.VMEM((nb, MOBA_HEADS, MOBA_VROWS, blk), BF16),
                pltpu.VMEM((MOBA_HEADS * 48, LANES), BF16),
                pltpu.VMEM((MOBA_HEADS, LANES, blk), BF16),
                pltpu.VMEM((MOBA_HEADS, LANES, blk), BF16),
                pltpu.VMEM((MOBA_HEADS, 1, blk), F32),
                pltpu.VMEM((MOBA_HEADS, MOBA_VROWS, blk), F32),
                pltpu.VMEM((MOBA_HEADS, 2 * blk, blk), BF16),
                pltpu.VMEM((MOBA_HEADS, 1, blk), F32),
            ],
        ),
        out_shape=jax.ShapeDtypeStruct((bsz * s, BRANCH_W), BF16),
        compiler_params=pltpu.CompilerParams(
            dimension_semantics=("parallel", "arbitrary"), vmem_limit_bytes=VMEM_LIMIT),
        name="moba",
    )(slopes, proj, proj, proj, proj, proj, aug, avg, mask)


HGRN_TILE = 256
HGRN_SUB = 2


def _hgrn_body(f_ref, i_ref, q_ref, g_ref, lbl_ref, nw_ref, lmat_ref, tril_ref, o_ref, st_ref):
    r, c, d = HGRN_TILE, HGRN_CHUNK, HGRN_D
    n_chunks = r // c
    heads = range(HGRN_HEADS)
    tiles = range(HGRN_SUB)
    chunks = range(n_chunks)
    cols = [slice(h * d, (h + 1) * d) for h in heads]
    rows = [slice(t * r, (t + 1) * r) for t in tiles]
    nt_dims = (((1,), (1,)), ((), ()))

    @pl.when(pl.program_id(1) == 0)
    def _():
        st_ref[...] = jnp.zeros_like(st_ref)

    logits = lbl_ref[...]
    ex = jnp.exp(logits - jnp.max(logits, axis=0, keepdims=True))
    lb = ex[0:1, :] / jnp.sum(ex, axis=0, keepdims=True)

    kk, b = [], []
    for t in tiles:
        fl = f_ref[rows[t], :].astype(F32)
        log_f = jnp.log(lb + (1.0 - lb) * _sigmoid(fl))
        kk.append((1.0 - lb) * _sigmoid(-fl))
        lmat = lmat_ref[...]
        hi, mid, lo = [jnp.dot(lmat, piece.astype(BF16), preferred_element_type=F32)
                       for piece in _split3(log_f)]
        b.append(hi + mid + lo)

    q_t, k_t32, k_t, dec = [], [], [], []
    for t in tiles:
        q_t.append((q_ref[rows[t], :].astype(F32) * jnp.exp(b[t])).astype(BF16))
        kt = kk[t] * jnp.exp(-b[t])
        k_t32.append(kt)
        k_t.append(kt.astype(BF16))
        dec.append([jnp.exp(b[t][ci * c + c - 1:ci * c + c, :]) for ci in chunks])

    a = [[lax.dot_general(q_t[t][:, cols[h]], k_t[t][:, cols[h]], nt_dims, preferred_element_type=F32)
          for h in heads] for t in tiles]
    u_t = []
    for t in tiles:
        per_head = []
        for h in heads:
            vt = i_ref[rows[t], cols[h]].astype(F32).T.astype(BF16)
            blocks = []
            for ci in chunks:
                slab = (k_t32[t][ci * c:(ci + 1) * c, cols[h]] * dec[t][ci][:, cols[h]]).astype(BF16)
                pieces = ([jnp.zeros((c, ci * d), BF16)] if ci else []) + [slab]
                if ci < n_chunks - 1:
                    pieces.append(jnp.zeros((c, (n_chunks - 1 - ci) * d), BF16))
                blocks.append(jnp.concatenate(pieces, axis=1))
            kd_blk = jnp.concatenate(blocks, axis=0)
            per_head.append(jnp.dot(vt, kd_blk, preferred_element_type=F32))
        u_t.append(per_head)
    o_intra = [[jnp.dot((a[t][h] * tril_ref[...]).astype(BF16), i_ref[rows[t], cols[h]],
                        preferred_element_type=F32) for h in heads] for t in tiles]

    states = [[None] * HGRN_HEADS for _ in tiles]
    for h in heads:
        st = st_ref[h]
        for t in tiles:
            before = []
            for ci in chunks:
                before.append(st.astype(BF16))
                st = st * dec[t][ci][:, cols[h]] + u_t[t][h][:, ci * d:(ci + 1) * d]
            states[t][h] = before
        st_ref[h] = st

    for t in tiles:
        for h in heads:
            o_inter = [lax.dot_general(q_t[t][ci * c:(ci + 1) * c, cols[h]], states[t][h][ci], nt_dims,
                                       preferred_element_type=F32) for ci in chunks]
            o = o_intra[t][h] + jnp.concatenate(o_inter, axis=0)
            on = o * lax.rsqrt(jnp.mean(o * o, axis=-1, keepdims=True) + EPS) * nw_ref[...]
            o_ref[rows[t], cols[h]] = (on * _silu(g_ref[rows[t], cols[h]].astype(F32))).astype(o_ref.dtype)


def _hgrn(proj, lb_logits, norm_w, bsz, s):
    r, c = HGRN_TILE, HGRN_CHUNK
    rs = r * HGRN_SUB
    nt = s // rs
    ri = np.arange(r)
    same = (ri[:, None] // c) == (ri[None, :] // c)
    tril = (same & (ri[None, :] <= ri[:, None])).astype(np.float32)
    col = lambda c0: (lambda b, t: (b * nt + t, c0 // BRANCH_W))
    const = lambda shape: pl.BlockSpec(shape, lambda b, t: (0,) * len(shape))
    return pl.pallas_call(
        _hgrn_body,
        grid=(bsz, nt),
        in_specs=[
            pl.BlockSpec((rs, BRANCH_W), col(COL_FB)),
            pl.BlockSpec((rs, BRANCH_W), col(COL_IB)),
            pl.BlockSpec((rs, BRANCH_W), col(COL_QB)),
            pl.BlockSpec((rs, BRANCH_W), col(COL_GB)),
            const((lb_logits.shape[0], BRANCH_W)),
            const((1, HGRN_D)),
            const((r, r)),
            const((r, r)),
        ],
        out_specs=pl.BlockSpec((rs, BRANCH_W), lambda b, t: (b * nt + t, 0)),
        out_shape=jax.ShapeDtypeStruct((bsz * s, BRANCH_W), BF16),
        scratch_shapes=[pltpu.VMEM((HGRN_HEADS, HGRN_D, HGRN_D), F32)],
        compiler_params=pltpu.CompilerParams(
            dimension_semantics=("parallel", "arbitrary"), vmem_limit_bytes=VMEM_LIMIT),
        name="hgrn",
    )(proj, proj, proj, proj, lb_logits, norm_w.reshape(1, HGRN_D), tril.astype(BF16), tril)


def _merge_body(x_ref, ga_ref, gb_ref, gc_ref, ya_ref, yb_ref, qc_ref, zc_ref, mem_ref, mw_ref, wkv_ref,
                wa_ref, wb_ref, wc_ref, wo_ref, pw_ref, o_ref, km_ref, vm_ref):
    @pl.when(pl.program_id(1) == 0)
    def _():
        m = mem_ref[0]
        mn = m * lax.rsqrt(jnp.mean(m * m, axis=-1, keepdims=True) + EPS) * mw_ref[...]
        kv = jnp.dot(mn.astype(BF16), wkv_ref[...], preferred_element_type=F32)
        km_ref[...] = kv[:, :BRANCH_W].astype(BF16)
        vm_ref[...] = kv[:, BRANCH_W:].astype(BF16)

    tm = x_ref.shape[0]
    halves = [slice(0, tm // 2), slice(tm // 2, tm)]
    heads = range(XA_HEADS)
    hcols = [slice(h * XA_HD, (h + 1) * XA_HD) for h in heads]
    nt_dims = (((1,), (1,)), ((), ()))
    logits = [[lax.dot_general(qc_ref[rows, hcols[h]], km_ref[:, hcols[h]], nt_dims,
                               preferred_element_type=F32) * (XA_HD ** -0.5) for h in heads] for rows in halves]
    proj_a = [jnp.dot(ya_ref[rows, :], wa_ref[...], preferred_element_type=F32) for rows in halves]
    proj_b = [jnp.dot(yb_ref[rows, :], wb_ref[...], preferred_element_type=F32) for rows in halves]
    probs, dens = [], []
    for k in range(2):
        pk, dk = [], []
        for h in heads:
            lg = logits[k][h]
            p = jnp.exp(lg - jnp.max(lg, axis=-1, keepdims=True))
            dk.append(jnp.sum(p, axis=-1, keepdims=True))
            pk.append(p.astype(BF16))
        probs.append(pk)
        dens.append(dk)
    pv = [[jnp.dot(probs[k][h], vm_ref[:, hcols[h]], preferred_element_type=F32) for h in heads]
          for k in range(2)]
    yc = [jnp.concatenate([(pv[k][h] / dens[k][h]) * _silu(zc_ref[rows, hcols[h]].astype(F32)) for h in heads],
                          axis=1).astype(BF16) for k, rows in enumerate(halves)]
    proj_c = [jnp.dot(yc[k], wc_ref[...], preferred_element_type=F32) for k in range(2)]
    merged = [_sigmoid(ga_ref[rows, :].astype(F32)) * proj_a[k]
              + _sigmoid(gb_ref[rows, :].astype(F32)) * proj_b[k]
              + _sigmoid(gc_ref[rows, :].astype(F32)) * proj_c[k] for k, rows in enumerate(halves)]
    y = [jnp.dot(merged[k].astype(BF16), wo_ref[...], preferred_element_type=F32) for k in range(2)]
    for k, rows in enumerate(halves):
        yn = y[k] * lax.rsqrt(jnp.mean(y[k] * y[k], axis=-1, keepdims=True) + EPS) * pw_ref[...]
        o_ref[rows, :] = x_ref[rows, :] + yn


def _merge(x2, proj, ya, yb, mem, mem_norm_w, wkv, wa, wb, wc, wo, post_w, bsz, s, tm=512):
    t, d = x2.shape
    n_mem = mem.shape[1]
    nt = s // tm
    row = lambda c: (lambda b, i: (b * nt + i, c))
    const = lambda shape: pl.BlockSpec(shape, lambda b, i: (0,) * len(shape), pipeline_mode=pl.Buffered(1))
    return pl.pallas_call(
        _merge_body,
        grid=(bsz, nt),
        in_specs=[
            pl.BlockSpec((tm, d), row(0)),
            pl.BlockSpec((tm, d), row(COL_GATE_A // d)),
            pl.BlockSpec((tm, d), row(COL_GATE_B // d)),
            pl.BlockSpec((tm, d), row(COL_GATE_C // d)),
            pl.BlockSpec((tm, BRANCH_W), row(0)),
            pl.BlockSpec((tm, BRANCH_W), row(0)),
            pl.BlockSpec((tm, BRANCH_W), row(COL_QC // BRANCH_W)),
            pl.BlockSpec((tm, BRANCH_W), row(COL_ZC // BRANCH_W)),
            pl.BlockSpec((1, n_mem, d), lambda b, i: (b, 0, 0)),
            const((1, d)),
            const((d, 2 * BRANCH_W)),
            const((BRANCH_W, d)), const((BRANCH_W, d)), const((BRANCH_W, d)),
            const((d, d)), const((1, d)),
        ],
        out_specs=pl.BlockSpec((tm, d), row(0)),
        out_shape=jax.ShapeDtypeStruct((t, d), F32),
        scratch_shapes=[pltpu.VMEM((n_mem, BRANCH_W), BF16), pltpu.VMEM((n_mem, BRANCH_W), BF16)],
        compiler_params=pltpu.CompilerParams(
            dimension_semantics=("parallel", "arbitrary"), vmem_limit_bytes=VMEM_LIMIT),
        name="merge",
    )(x2, proj, proj, proj, ya, yb, proj, proj, mem, mem_norm_w.reshape(1, d), wkv, wa, wb, wc, wo,
      post_w.reshape(1, d))


def kernel(x, mem, pre_norm_w, w_in, hgrn_lb_logits, hgrn_norm_w, mem_norm_w, w_mem_kv,
           w_branch_a, w_branch_b, w_branch_c, w_out, post_norm_w):
    bsz, s, d = x.shape
    assert w_in.shape[0] == 1 and w_in.shape[2] == PROJ_TOTAL and d == 1024
    assert s % MOBA_BLOCK == 0 and s % (HGRN_TILE * HGRN_SUB) == 0
    x2 = x.reshape(bsz * s, d)
    proj = _in_proj(x2, pre_norm_w[0], w_in[0].astype(BF16))
    slopes = jnp.exp2(-8.0 * jnp.arange(1, MOBA_HEADS + 1, dtype=F32) / MOBA_HEADS)
    ya = _moba(proj, slopes, bsz, s)
    yb = _hgrn(proj, hgrn_lb_logits, hgrn_norm_w[0], bsz, s)
    out = _merge(x2, proj, ya, yb, mem, mem_norm_w[0], w_mem_kv[0].astype(BF16),
                 w_branch_a[0].astype(BF16), w_branch_b[0].astype(BF16), w_branch_c[0].astype(BF16),
                 w_out[0].astype(BF16), post_norm_w[0], bsz, s)
    return out.reshape(bsz, s, d)
```

```python
import jax
import jax.numpy as jnp
import numpy as np
from jax import lax
from jax.experimental import pallas as pl
from jax.experimental.pallas import tpu as pltpu

F32 = jnp.float32
BF16 = jnp.bfloat16
EPS = 1e-6

MOBA_HEADS, MOBA_HD, MOBA_BLOCK, MOBA_TOPK = 8, 64, 256, 3
HGRN_HEADS, HGRN_D, HGRN_CHUNK = 4, 128, 32
XA_HEADS, XA_HD = 4, 128
BRANCH_W = 512
LANES = 128
COL_QA, COL_KA, COL_VA, COL_ZA = 0, 512, 1024, 1536
COL_FB, COL_IB, COL_QB, COL_GB = 2048, 2560, 3072, 3584
COL_QC, COL_ZC = 4096, 4608
COL_GATE_A, COL_GATE_B, COL_GATE_C = 5120, 6144, 7168
PROJ_TOTAL = COL_GATE_C + 1024

NEG_BIG = -1e30
LOG2E = 1.4426950408889634
MOBA_VROWS = MOBA_HD + 16
VMEM_LIMIT = 56 * 1024 * 1024


def _sigmoid(z):
    return 1.0 / (1.0 + jnp.exp2(z * (-LOG2E)))


def _silu(z):
    return z * _sigmoid(z)


def _split3(a):
    hi = a.astype(BF16).astype(F32)
    r = a - hi
    mid = r.astype(BF16).astype(F32)
    lo = (r - mid).astype(BF16).astype(F32)
    return hi, mid, lo


IN_PROJ_CHUNK = 1024


def _in_proj_body(x_ref, nw_ref, w_ref, o_ref):
    x = x_ref[...]
    ms = jnp.mean(x * x, axis=-1, keepdims=True)
    h = (x * lax.rsqrt(ms + EPS) * nw_ref[...]).astype(BF16)
    for c0 in range(0, o_ref.shape[1], IN_PROJ_CHUNK):
        cols = slice(c0, c0 + IN_PROJ_CHUNK)
        o_ref[:, cols] = jnp.dot(h, w_ref[:, cols], preferred_element_type=F32).astype(o_ref.dtype)


def _in_proj(x2, norm_w, w_bf16, tm=512):
    t, d = x2.shape
    n = w_bf16.shape[1]
    return pl.pallas_call(
        _in_proj_body,
        grid=(t // tm,),
        in_specs=[
            pl.BlockSpec((tm, d), lambda i: (i, 0)),
            pl.BlockSpec((1, d), lambda i: (0, 0)),
            pl.BlockSpec((d, n), lambda i: (0, 0), pipeline_mode=pl.Buffered(1)),
        ],
        out_specs=pl.BlockSpec((tm, n), lambda i: (i, 0)),
        out_shape=jax.ShapeDtypeStruct((t, n), BF16),
        compiler_params=pltpu.CompilerParams(
            dimension_semantics=("parallel",), vmem_limit_bytes=VMEM_LIMIT),
        name="in_proj",
    )(x2, norm_w.reshape(1, d), w_bf16)


def _moba_constants(s, nb_lanes=16):
    blk = MOBA_BLOCK
    row = np.arange(s)
    off = (row % blk).astype(np.float32)
    onehot = (row[:, None] // blk == np.arange(nb_lanes)[None, :]).astype(np.float32)
    bias_lanes = np.concatenate([onehot, onehot, onehot, off[:, None], off[:, None], off[:, None],
                                 np.zeros((s, 64 - 3 * nb_lanes - 3), np.float32)], axis=1)
    zeros = np.zeros((s, 64), np.float32)
    a0 = np.concatenate([zeros, bias_lanes], axis=1)
    a1 = np.concatenate([bias_lanes, zeros], axis=1)
    aug = np.stack([a0, a1]).astype(BF16)
    avg = ((np.arange(nb_lanes)[:, None] == (row // blk)[None, :]).astype(np.float32) / blk).astype(BF16)
    causal = np.where(np.arange(blk)[:, None] <= np.arange(blk)[None, :], 0.0, -np.inf).astype(np.float32)
    mask = np.concatenate([np.zeros((blk, blk), np.float32), causal])
    return aug, avg, mask


def _moba_body(slopes_ref, q_ref, qn_ref, k_ref, v_ref, z_ref, aug_ref, avg_ref, mask_ref, o_ref,
               ka_ref, vt_ref, kbp_ref, qa_ref, qan_ref, m_ref, acc_ref, p_ref, al_ref):
    s = k_ref.shape[0]
    nb = s // MOBA_BLOCK
    blk = MOBA_BLOCK
    nh = MOBA_HEADS
    hd = MOBA_HD
    i = pl.program_id(1)

    def build_queries(src_ref, iq, dst_ref):
        nidx = lax.broadcasted_iota(jnp.int32, (16, blk), 0)
        nidx_f = nidx.astype(F32)
        qoff = lax.broadcasted_iota(jnp.int32, (16, blk), 1)
        dist0 = (qoff + (iq - nidx) * blk).astype(F32)
        qts = [(src_ref[:, hp * LANES:(hp + 1) * LANES].astype(F32) * (hd ** -0.5)).T
               for hp in range(nh // 2)]
        gates = [jnp.dot(kbp_ref[hp * 96:(hp + 1) * 96, :], qts[hp].astype(BF16),
                         preferred_element_type=F32) for hp in range(nh // 2)]
        for hp in range(nh // 2):
            qs = qts[hp] * LOG2E
            for e in range(2):
                h = 2 * hp + e
                gg = gates[hp]
                g = gg[e * 48:e * 48 + 16] + gg[e * 48 + 16:e * 48 + 32] + gg[e * 48 + 32:e * 48 + 48]
                gv = jnp.where(nidx < iq, g, -jnp.inf)
                sel = nidx == iq
                for _ in range(MOBA_TOPK):
                    mx = jnp.max(gv, axis=0, keepdims=True)
                    first = jnp.min(jnp.where(gv == mx, nidx_f, 16.0), axis=0, keepdims=True)
                    pick = nidx_f == jnp.where(mx > -jnp.inf, first, 16.0)
                    sel = sel | pick
                    gv = jnp.where(pick, -jnp.inf, gv)
                sl = slopes_ref[h] * LOG2E
                bias = jnp.where(sel, -sl * dist0, NEG_BIG)
                b_hi, b_mid, b_lo = _split3(bias)
                s_hi, s_mid, s_lo = _split3(jnp.full((16, blk), sl, F32))
                srow = jnp.where(nidx == 0, s_hi, jnp.where(nidx == 1, s_mid, jnp.where(nidx == 2, s_lo, 0.0)))
                qh = qs[0:64] if e == 0 else qs[64:128]
                parts = [qh, b_hi, b_mid, b_lo, srow] if e == 0 else [b_hi, b_mid, b_lo, srow, qh]
                dst_ref[h] = jnp.concatenate(parts, axis=0).astype(BF16)

    @pl.when(i == 0)
    def _():
        lane = lax.broadcasted_iota(jnp.int32, (1, LANES), 1)
        first_half = lane < 64
        kbar = jnp.dot(avg_ref[...], k_ref[...], preferred_element_type=F32)
        for h in range(nh):
            hp, e = h // 2, h % 2
            cols = slice(hp * LANES, (hp + 1) * LANES)
            keep = first_half if e == 0 else jnp.logical_not(first_half)
            for j in range(nb):
                rows = slice(j * blk, (j + 1) * blk)
                ka_ref[h, j] = jnp.where(keep, k_ref[rows, cols], aug_ref[e, rows, :])
            pieces = _split3(jnp.where(keep, kbar[:, cols], 0.0))
            for p in range(3):
                kbp_ref[(h * 3 + p) * 16:(h * 3 + p + 1) * 16, :] = pieces[p].astype(BF16)
        ones_row = jnp.where(lax.broadcasted_iota(jnp.int32, (MOBA_VROWS - hd, blk), 0) == 0, 1.0, 0.0)
        for j in range(nb):
            vt = v_ref[j * blk:(j + 1) * blk, :].astype(F32).T
            for h in range(nh):
                vt_ref[j, h, 0:hd, :] = vt[h * hd:(h + 1) * hd].astype(BF16)
                vt_ref[j, h, hd:MOBA_VROWS, :] = ones_row.astype(BF16)
        build_queries(q_ref, 0, qa_ref)
        p_ref[...] = jnp.zeros(p_ref.shape, BF16)
        al_ref[...] = jnp.ones(al_ref.shape, F32)

    @pl.when(i > 0)
    def _():
        qa_ref[...] = qan_ref[...]

    m_ref[...] = jnp.full(m_ref.shape, -jnp.inf, F32)
    acc_ref[...] = jnp.zeros(acc_ref.shape, F32)

    def softmax(h, sT):
        m_old = m_ref[h]
        m_new = jnp.maximum(m_old, jnp.max(sT, axis=0, keepdims=True))
        m_ref[h] = m_new
        return jnp.exp2(sT - m_new).astype(BF16), jnp.exp2(m_old - m_new)

    def pair_scores(j0):
        return [jnp.dot(ka_ref[h, pl.ds(j0, 2)].reshape(2 * blk, LANES), qa_ref[h],
                        preferred_element_type=F32) for h in range(nh)]

    def pair_values(ja, jb, h):
        return jnp.concatenate([vt_ref[ja, h], vt_ref[jb, h]], axis=1)

    def pending_pv(j0):
        pa, pb = jnp.maximum(j0 - 2, 0), jnp.maximum(j0 - 1, 0)
        return [jnp.dot(pair_values(pa, pb, h), p_ref[h], preferred_element_type=F32) for h in range(nh)]

    def past(k, carry):
        scores = pair_scores(2 * k)
        pv = pending_pv(2 * k)
        sm = [softmax(h, scores[h]) for h in range(nh)]
        for h in range(nh):
            acc_ref[h] = al_ref[h] * acc_ref[h] + pv[h]
        for h in range(nh):
            p_ref[h], al_ref[h] = sm[h]
        return carry

    npairs = i // 2
    lax.fori_loop(0, npairs, past, 0)
    build_queries(qn_ref, i + 1, qan_ref)

    def finish(scores, last_values):
        pv = pending_pv(2 * npairs)
        sm = [softmax(h, scores[h]) for h in range(nh)]
        last = [jnp.dot(last_values(h), sm[h][0], preferred_element_type=F32) for h in range(nh)]
        outs = []
        for h in range(nh):
            acc = sm[h][1] * (al_ref[h] * acc_ref[h] + pv[h]) + last[h]
            outs.append(acc[0:hd] / acc[hd:hd + 1])
        ot = jnp.concatenate(outs, axis=0)
        o_ref[...] = (ot.T * _silu(z_ref[...].astype(F32))).astype(o_ref.dtype)

    @pl.when(i % 2 == 0)
    def _():
        finish([jnp.dot(ka_ref[h, i], qa_ref[h], preferred_element_type=F32) + mask_ref[blk:2 * blk, :]
                for h in range(nh)], lambda h: vt_ref[i, h])

    @pl.when(i % 2 == 1)
    def _():
        scores = pair_scores(i - 1)
        finish([scores[h] + mask_ref[...] for h in range(nh)], lambda h: pair_values(i - 1, i, h))


def _moba(proj, slopes, bsz, s):
    nb = s // MOBA_BLOCK
    assert nb <= 16
    blk = MOBA_BLOCK
    aug, avg, mask = _moba_constants(s)
    qblock = lambda c0: (lambda b, i, *_: (b * nb + i, c0 // BRANCH_W))
    qnext = lambda b, i, *_: (b * nb + jnp.minimum(i + 1, nb - 1), COL_QA // BRANCH_W)
    whole = lambda c0: (lambda b, i, *_: (b, c0 // BRANCH_W))
    return pl.pallas_call(
        _moba_body,
        grid_spec=pltpu.PrefetchScalarGridSpec(
            num_scalar_prefetch=1,
            grid=(bsz, nb),
            in_specs=[
                pl.BlockSpec((blk, BRANCH_W), qblock(COL_QA)),
                pl.BlockSpec((blk, BRANCH_W), qnext),
                pl.BlockSpec((s, BRANCH_W), whole(COL_KA)),
                pl.BlockSpec((s, BRANCH_W), whole(COL_VA)),
                pl.BlockSpec((blk, BRANCH_W), qblock(COL_ZA)),
                pl.BlockSpec((2, s, LANES), lambda b, i, *_: (0, 0, 0)),
                pl.BlockSpec((16, s), lambda b, i, *_: (0, 0)),
                pl.BlockSpec((2 * blk, blk), lambda b, i, *_: (0, 0)),
            ],
            out_specs=pl.BlockSpec((blk, BRANCH_W), lambda b, i, *_: (b * nb + i, 0)),
            scratch_shapes=[
                pltpu.VMEM((MOBA_HEADS, nb, blk, LANES), BF16),
                pltpu.VMEM((nb, MOBA_HEADS, MOBA_VROWS, blk), BF16),
                pltpu.VMEM((MOBA_HEADS * 48, LANES), BF16),
                pltpu.VMEM((MOBA_HEADS, LANES, blk), BF16),
                pltpu.VMEM((MOBA_HEADS, LANES, blk), BF16),
                pltpu.VMEM((MOBA_HEADS, 1, blk), F32),
                pltpu.VMEM((MOBA_HEADS, MOBA_VROWS, blk), F32),
                pltpu.VMEM((MOBA_HEADS, 2 * blk, blk), BF16),
                pltpu.VMEM((MOBA_HEADS, 1, blk), F32),
            ],
        ),
        out_shape=jax.ShapeDtypeStruct((bsz * s, BRANCH_W), BF16),
        compiler_params=pltpu.CompilerParams(
            dimension_semantics=("parallel", "arbitrary"), vmem_limit_bytes=VMEM_LIMIT),
        name="moba",
    )(slopes, proj, proj, proj, proj, proj, aug, avg, mask)


HGRN_TILE = 256
HGRN_SUB = 2


def _hgrn_body(f_ref, i_ref, q_ref, g_ref, lbl_ref, nw_ref, lmat_ref, tril_ref, o_ref, st_ref):
    r, c, d = HGRN_TILE, HGRN_CHUNK, HGRN_D
    n_chunks = r // c
    heads = range(HGRN_HEADS)
    tiles = range(HGRN_SUB)
    chunks = range(n_chunks)
    cols = [slice(h * d, (h + 1) * d) for h in heads]
    rows = [slice(t * r, (t + 1) * r) for t in tiles]
    nt_dims = (((1,), (1,)), ((), ()))

    @pl.when(pl.program_id(1) == 0)
    def _():
        st_ref[...] = jnp.zeros_like(st_ref)

    logits = lbl_ref[...]
    ex = jnp.exp(logits - jnp.max(logits, axis=0, keepdims=True))
    lb = ex[0:1, :] / jnp.sum(ex, axis=0, keepdims=True)

    kk, b = [], []
    for t in tiles:
        fl = f_ref[rows[t], :].astype(F32)
        log_f = jnp.log(lb + (1.0 - lb) * _sigmoid(fl))
        kk.append((1.0 - lb) * _sigmoid(-fl))
        lmat = lmat_ref[...]
        hi, mid, lo = [jnp.dot(lmat, piece.astype(BF16), preferred_element_type=F32)
                       for piece in _split3(log_f)]
        b.append(hi + mid + lo)

    q_t, k_t32, k_t, dec = [], [], [], []
    for t in tiles:
        q_t.append((q_ref[rows[t], :].astype(F32) * jnp.exp(b[t])).astype(BF16))
        kt = kk[t] * jnp.exp(-b[t])
        k_t32.append(kt)
        k_t.append(kt.astype(BF16))
        dec.append([jnp.exp(b[t][ci * c + c - 1:ci * c + c, :]) for ci in chunks])

    a = [[lax.dot_general(q_t[t][:, cols[h]], k_t[t][:, cols[h]], nt_dims, preferred_element_type=F32)
          for h in heads] for t in tiles]
    u_t = []
    for t in tiles:
        per_head = []
        for h in heads:
            vt = i_ref[rows[t], cols[h]].astype(F32).T.astype(BF16)
            blocks = []
            for ci in chunks:
                slab = (k_t32[t][ci * c:(ci + 1) * c, cols[h]] * dec[t][ci][:, cols[h]]).astype(BF16)
                pieces = ([jnp.zeros((c, ci * d), BF16)] if ci else []) + [slab]
                if ci < n_chunks - 1:
                    pieces.append(jnp.zeros((c, (n_chunks - 1 - ci) * d), BF16))
                blocks.append(jnp.concatenate(pieces, axis=1))
            kd_blk = jnp.concatenate(blocks, axis=0)
            per_head.append(jnp.dot(vt, kd_blk, preferred_element_type=F32))
        u_t.append(per_head)
    o_intra = [[jnp.dot((a[t][h] * tril_ref[...]).astype(BF16), i_ref[rows[t], cols[h]],
                        preferred_element_type=F32) for h in heads] for t in tiles]

    states = [[None] * HGRN_HEADS for _ in tiles]
    for h in heads:
        st = st_ref[h]
        for t in tiles:
            before = []
            for ci in chunks:
                before.append(st.astype(BF16))
                st = st * dec[t][ci][:, cols[h]] + u_t[t][h][:, ci * d:(ci + 1) * d]
            states[t][h] = before
        st_ref[h] = st

    for t in tiles:
        for h in heads:
            o_inter = [lax.dot_general(q_t[t][ci * c:(ci + 1) * c, cols[h]], states[t][h][ci], nt_dims,
                                       preferred_element_type=F32) for ci in chunks]
            o = o_intra[t][h] + jnp.concatenate(o_inter, axis=0)
            on = o * lax.rsqrt(jnp.mean(o * o, axis=-1, keepdims=True) + EPS) * nw_ref[...]
            o_ref[rows[t], cols[h]] = (on * _silu(g_ref[rows[t], cols[h]].astype(F32))).astype(o_ref.dtype)


def _hgrn(proj, lb_logits, norm_w, bsz, s):
    r, c = HGRN_TILE, HGRN_CHUNK
    rs = r * HGRN_SUB
    nt = s // rs
    ri = np.arange(r)
    same = (ri[:, None] // c) == (ri[None, :] // c)
    tril = (same & (ri[None, :] <= ri[:, None])).astype(np.float32)
    col = lambda c0: (lambda b, t: (b * nt + t, c0 // BRANCH_W))
    const = lambda shape: pl.BlockSpec(shape, lambda b, t: (0,) * len(shape))
    return pl.pallas_call(
        _hgrn_body,
        grid=(bsz, nt),
        in_specs=[
            pl.BlockSpec((rs, BRANCH_W), col(COL_FB)),
            pl.BlockSpec((rs, BRANCH_W), col(COL_IB)),
            pl.BlockSpec((rs, BRANCH_W), col(COL_QB)),
            pl.BlockSpec((rs, BRANCH_W), col(COL_GB)),
            const((lb_logits.shape[0], BRANCH_W)),
            const((1, HGRN_D)),
            const((r, r)),
            const((r, r)),
        ],
        out_specs=pl.BlockSpec((rs, BRANCH_W), lambda b, t: (b * nt + t, 0)),
        out_shape=jax.ShapeDtypeStruct((bsz * s, BRANCH_W), BF16),
        scratch_shapes=[pltpu.VMEM((HGRN_HEADS, HGRN_D, HGRN_D), F32)],
        compiler_params=pltpu.CompilerParams(
            dimension_semantics=("parallel", "arbitrary"), vmem_limit_bytes=VMEM_LIMIT),
        name="hgrn",
    )(proj, proj, proj, proj, lb_logits, norm_w.reshape(1, HGRN_D), tril.astype(BF16), tril)


def _merge_body(x_ref, ga_ref, gb_ref, gc_ref, ya_ref, yb_ref, qc_ref, zc_ref, mem_ref, mw_ref, wkv_ref,
                wa_ref, wb_ref, wc_ref, wo_ref, pw_ref, o_ref, km_ref, vm_ref):
    @pl.when(pl.program_id(1) == 0)
    def _():
        m = mem_ref[0]
        mn = m * lax.rsqrt(jnp.mean(m * m, axis=-1, keepdims=True) + EPS) * mw_ref[...]
        kv = jnp.dot(mn.astype(BF16), wkv_ref[...], preferred_element_type=F32)
        km_ref[...] = kv[:, :BRANCH_W].astype(BF16)
        vm_ref[...] = kv[:, BRANCH_W:].astype(BF16)

    tm = x_ref.shape[0]
    halves = [slice(0, tm // 2), slice(tm // 2, tm)]
    heads = range(XA_HEADS)
    hcols = [slice(h * XA_HD, (h + 1) * XA_HD) for h in heads]
    nt_dims = (((1,), (1,)), ((), ()))
    logits = [[lax.dot_general(qc_ref[rows, hcols[h]], km_ref[:, hcols[h]], nt_dims,
                               preferred_element_type=F32) * (XA_HD ** -0.5) for h in heads] for rows in halves]
    proj_a = [jnp.dot(ya_ref[rows, :], wa_ref[...], preferred_element_type=F32) for rows in halves]
    proj_b = [jnp.dot(yb_ref[rows, :], wb_ref[...], preferred_element_type=F32) for rows in halves]
    probs, dens = [], []
    for k in range(2):
        pk, dk = [], []
        for h in heads:
            lg = logits[k][h]
            p = jnp.exp(lg - jnp.max(lg, axis=-1, keepdims=True))
            dk.append(jnp.sum(p, axis=-1, keepdims=True))
            pk.append(p.astype(BF16))
        probs.append(pk)
        dens.append(dk)
    pv = [[jnp.dot(probs[k][h], vm_ref[:, hcols[h]], preferred_element_type=F32) for h in heads]
          for k in range(2)]
    yc = [jnp.concatenate([(pv[k][h] / dens[k][h]) * _silu(zc_ref[rows, hcols[h]].astype(F32)) for h in heads],
                          axis=1).astype(BF16) for k, rows in enumerate(halves)]
    proj_c = [jnp.dot(yc[k], wc_ref[...], preferred_element_type=F32) for k in range(2)]
    merged = [_sigmoid(ga_ref[rows, :].astype(F32)) * proj_a[k]
              + _sigmoid(gb_ref[rows, :].astype(F32)) * proj_b[k]
              + _sigmoid(gc_ref[rows, :].astype(F32)) * proj_c[k] for k, rows in enumerate(halves)]
    y = [jnp.dot(merged[k].astype(BF16), wo_ref[...], preferred_element_type=F32) for k in range(2)]
    for k, rows in enumerate(halves):
        yn = y[k] * lax.rsqrt(jnp.mean(y[k] * y[k], axis=-1, keepdims=True) + EPS) * pw_ref[...]
        o_ref[rows, :] = x_ref[rows, :] + yn


def _merge(x2, proj, ya, yb, mem, mem_norm_w, wkv, wa, wb, wc, wo, post_w, bsz, s, tm=512):
    t, d = x2.shape
    n_mem = mem.shape[1]
    nt = s // tm
    row = lambda c: (lambda b, i: (b * nt + i, c))
    const = lambda shape: pl.BlockSpec(shape, lambda b, i: (0,) * len(shape), pipeline_mode=pl.Buffered(1))
    return pl.pallas_call(
        _merge_body,
        grid=(bsz, nt),
        in_specs=[
            pl.BlockSpec((tm, d), row(0)),
            pl.BlockSpec((tm, d), row(COL_GATE_A // d)),
            pl.BlockSpec((tm, d), row(COL_GATE_B // d)),
            pl.BlockSpec((tm, d), row(COL_GATE_C // d)),
            pl.BlockSpec((tm, BRANCH_W), row(0)),
            pl.BlockSpec((tm, BRANCH_W), row(0)),
            pl.BlockSpec((tm, BRANCH_W), row(COL_QC // BRANCH_W)),
            pl.BlockSpec((tm, BRANCH_W), row(COL_ZC // BRANCH_W)),
            pl.BlockSpec((1, n_mem, d), lambda b, i: (b, 0, 0)),
            const((1, d)),
            const((d, 2 * BRANCH_W)),
            const((BRANCH_W, d)), const((BRANCH_W, d)), const((BRANCH_W, d)),
            const((d, d)), const((1, d)),
        ],
        out_specs=pl.BlockSpec((tm, d), row(0)),
        out_shape=jax.ShapeDtypeStruct((t, d), F32),
        scratch_shapes=[pltpu.VMEM((n_mem, BRANCH_W), BF16), pltpu.VMEM((n_mem, BRANCH_W), BF16)],
        compiler_params=pltpu.CompilerParams(
            dimension_semantics=("parallel", "arbitrary"), vmem_limit_bytes=VMEM_LIMIT),
        name="merge",
    )(x2, proj, proj, proj, ya, yb, proj, proj, mem, mem_norm_w.reshape(1, d), wkv, wa, wb, wc, wo,
      post_w.reshape(1, d))


def kernel(x, mem, pre_norm_w, w_in, hgrn_lb_logits, hgrn_norm_w, mem_norm_w, w_mem_kv,
           w_branch_a, w_branch_b, w_branch_c, w_out, post_norm_w):
    bsz, s, d = x.shape
    assert w_in.shape[0] == 1 and w_in.shape[2] == PROJ_TOTAL and d == 1024
    assert s % MOBA_BLOCK == 0 and s % (HGRN_TILE * HGRN_SUB) == 0
    x2 = x.reshape(bsz * s, d)
    proj = _in_proj(x2, pre_norm_w[0], w_in[0].astype(BF16))
    slopes = jnp.exp2(-8.0 * jnp.arange(1, MOBA_HEADS + 1, dtype=F32) / MOBA_HEADS)
    ya = _moba(proj, slopes, bsz, s)
    yb = _hgrn(proj, hgrn_lb_logits, hgrn_norm_w[0], bsz, s)
    out = _merge(x2, proj, ya, yb, mem, mem_norm_w[0], w_mem_kv[0].astype(BF16),
                 w_branch_a[0].astype(BF16), w_branch_b[0].astype(BF16), w_branch_c[0].astype(BF16),
                 w_out[0].astype(BF16), post_norm_w[0], bsz, s)
    return out.reshape(bsz, s, d)
```

```python
import jax
import jax.numpy as jnp
import numpy as np
from jax import lax
from jax.experimental import pallas as pl
from jax.experimental.pallas import tpu as pltpu

F32 = jnp.float32
BF16 = jnp.bfloat16
EPS = 1e-6

MOBA_HEADS, MOBA_HD, MOBA_BLOCK, MOBA_TOPK = 8, 64, 256, 3
HGRN_HEADS, HGRN_D, HGRN_CHUNK = 4, 128, 32
XA_HEADS, XA_HD = 4, 128
BRANCH_W = 512
LANES = 128
COL_QA, COL_KA, COL_VA, COL_ZA = 0, 512, 1024, 1536
COL_FB, COL_IB, COL_QB, COL_GB = 2048, 2560, 3072, 3584
COL_QC, COL_ZC = 4096, 4608
COL_GATE_A, COL_GATE_B, COL_GATE_C = 5120, 6144, 7168
PROJ_TOTAL = COL_GATE_C + 1024

NEG_BIG = -1e30
LOG2E = 1.4426950408889634
MOBA_VROWS = MOBA_HD + 16
MOBA_NBL = 16
MOBA_GATE_ROWS = 3 * MOBA_NBL
VMEM_LIMIT = 56 * 1024 * 1024


def _sigmoid(z):
    return 1.0 / (1.0 + jnp.exp2(z * (-LOG2E)))


def _silu(z):
    return z * _sigmoid(z)


def _split3(a):
    hi = a.astype(BF16).astype(F32)
    r = a - hi
    mid = r.astype(BF16).astype(F32)
    lo = (r - mid).astype(BF16).astype(F32)
    return hi, mid, lo


IN_PROJ_CHUNK = 1024


def _in_proj_body(x_ref, nw_ref, w_ref, o_ref):
    x = x_ref[...]
    ms = jnp.mean(x * x, axis=-1, keepdims=True)
    h = (x * lax.rsqrt(ms + EPS) * nw_ref[...]).astype(BF16)
    for c0 in range(0, o_ref.shape[1], IN_PROJ_CHUNK):
        cols = slice(c0, c0 + IN_PROJ_CHUNK)
        o_ref[:, cols] = jnp.dot(h, w_ref[:, cols], preferred_element_type=F32).astype(o_ref.dtype)


def _in_proj(x2, norm_w, w_bf16, tm=512):
    t, d = x2.shape
    n = w_bf16.shape[1]
    return pl.pallas_call(
        _in_proj_body,
        grid=(t // tm,),
        in_specs=[
            pl.BlockSpec((tm, d), lambda i: (i, 0)),
            pl.BlockSpec((1, d), lambda i: (0, 0)),
            pl.BlockSpec((d, n), lambda i: (0, 0), pipeline_mode=pl.Buffered(1)),
        ],
        out_specs=pl.BlockSpec((tm, n), lambda i: (i, 0)),
        out_shape=jax.ShapeDtypeStruct((t, n), BF16),
        compiler_params=pltpu.CompilerParams(
            dimension_semantics=("parallel",), vmem_limit_bytes=VMEM_LIMIT),
        name="in_proj",
    )(x2, norm_w.reshape(1, d), w_bf16)


def _moba_constants(s):
    blk, hd, nbl = MOBA_BLOCK, MOBA_HD, MOBA_NBL
    row = np.arange(s)
    off = (row % blk).astype(np.float32)
    onehot = (row[:, None] // blk == np.arange(nbl)[None, :]).astype(np.float32)
    bias_lanes = np.concatenate([onehot, onehot, onehot, off[:, None], off[:, None], off[:, None],
                                 np.zeros((s, hd - 3 * nbl - 3), np.float32)], axis=1)
    zeros = np.zeros((s, hd), np.float32)
    a0 = np.concatenate([zeros, bias_lanes], axis=1)
    a1 = np.concatenate([bias_lanes, zeros], axis=1)
    aug = np.stack([a0, a1]).astype(BF16)
    avg = ((np.arange(nbl)[:, None] == (row // blk)[None, :]).astype(np.float32) / blk).astype(BF16)
    causal = np.where(np.arange(blk)[:, None] <= np.arange(blk)[None, :], 0.0, -np.inf).astype(np.float32)
    mask = np.concatenate([np.zeros((blk, blk), np.float32), causal])
    return aug, avg, mask


def _moba_body(slopes_ref, q_ref, qn_ref, k_ref, v_ref, z_ref, aug_ref, avg_ref, mask_ref, o_ref,
               ka_ref, vt_ref, kbp_ref, qa_ref, qan_ref, m_ref, acc_ref, p_ref, al_ref):
    s = k_ref.shape[0]
    nb = s // MOBA_BLOCK
    blk = MOBA_BLOCK
    nh = MOBA_HEADS
    hd = MOBA_HD
    nbl, gr = MOBA_NBL, MOBA_GATE_ROWS
    i = pl.program_id(1)

    def build_queries(src_ref, iq, dst_ref):
        nidx = lax.broadcasted_iota(jnp.int32, (nbl, blk), 0)
        nidx_f = nidx.astype(F32)
        qoff = lax.broadcasted_iota(jnp.int32, (nbl, blk), 1)
        dist0 = (qoff + (iq - nidx) * blk).astype(F32)
        qts = [(src_ref[:, hp * LANES:(hp + 1) * LANES].astype(F32) * (hd ** -0.5)).T
               for hp in range(nh // 2)]
        gates = [jnp.dot(kbp_ref[hp * 2 * gr:(hp + 1) * 2 * gr, :], qts[hp].astype(BF16),
                         preferred_element_type=F32) for hp in range(nh // 2)]
        for hp in range(nh // 2):
            qs = qts[hp] * LOG2E
            for e in range(2):
                h = 2 * hp + e
                gg = gates[hp][e * gr:(e + 1) * gr]
                g = gg[0:nbl] + gg[nbl:2 * nbl] + gg[2 * nbl:3 * nbl]
                gv = jnp.where(nidx < iq, g, -jnp.inf)
                sel = nidx == iq
                for _ in range(MOBA_TOPK):
                    mx = jnp.max(gv, axis=0, keepdims=True)
                    first = jnp.min(jnp.where(gv == mx, nidx_f, float(nbl)), axis=0, keepdims=True)
                    pick = nidx_f == jnp.where(mx > -jnp.inf, first, float(nbl))
                    sel = sel | pick
                    gv = jnp.where(pick, -jnp.inf, gv)
                sl = slopes_ref[h] * LOG2E
                bias = jnp.where(sel, -sl * dist0, NEG_BIG)
                b_hi, b_mid, b_lo = _split3(bias)
                s_hi, s_mid, s_lo = _split3(jnp.full((nbl, blk), sl, F32))
                srow = jnp.where(nidx == 0, s_hi, jnp.where(nidx == 1, s_mid, jnp.where(nidx == 2, s_lo, 0.0)))
                qh = qs[0:hd] if e == 0 else qs[hd:2 * hd]
                parts = [qh, b_hi, b_mid, b_lo, srow] if e == 0 else [b_hi, b_mid, b_lo, srow, qh]
                dst_ref[h] = jnp.concatenate(parts, axis=0).astype(BF16)

    @pl.when(i == 0)
    def _():
        lane = lax.broadcasted_iota(jnp.int32, (1, LANES), 1)
        first_half = lane < hd
        kbar = jnp.dot(avg_ref[...], k_ref[...], preferred_element_type=F32)
        for h in range(nh):
            hp, e = h // 2, h % 2
            cols = slice(hp * LANES, (hp + 1) * LANES)
            keep = first_half if e == 0 else jnp.logical_not(first_half)
            for j in range(nb):
                rows = slice(j * blk, (j + 1) * blk)
                ka_ref[h, j] = jnp.where(keep, k_ref[rows, cols], aug_ref[e, rows, :])
            pieces = _split3(jnp.where(keep, kbar[:, cols], 0.0))
            for p in range(3):
                kbp_ref[h * gr + p * nbl:h * gr + (p + 1) * nbl, :] = pieces[p].astype(BF16)
        ones_row = jnp.where(lax.broadcasted_iota(jnp.int32, (MOBA_VROWS - hd, blk), 0) == 0, 1.0, 0.0)
        for j in range(nb):
            vt = v_ref[j * blk:(j + 1) * blk, :].astype(F32).T
            for h in range(nh):
                vt_ref[j, h, 0:hd, :] = vt[h * hd:(h + 1) * hd].astype(BF16)
                vt_ref[j, h, hd:MOBA_VROWS, :] = ones_row.astype(BF16)
        build_queries(q_ref, 0, qa_ref)
        p_ref[...] = jnp.zeros(p_ref.shape, BF16)
        al_ref[...] = jnp.ones(al_ref.shape, F32)

    @pl.when(i > 0)
    def _():
        qa_ref[...] = qan_ref[...]

    m_ref[...] = jnp.full(m_ref.shape, -jnp.inf, F32)
    acc_ref[...] = jnp.zeros(acc_ref.shape, F32)

    def softmax(h, sT):
        m_old = m_ref[h]
        m_new = jnp.maximum(m_old, jnp.max(sT, axis=0, keepdims=True))
        m_ref[h] = m_new
        return jnp.exp2(sT - m_new).astype(BF16), jnp.exp2(m_old - m_new)

    def pair_scores(j0):
        return [jnp.dot(ka_ref[h, pl.ds(j0, 2)].reshape(2 * blk, LANES), qa_ref[h],
                        preferred_element_type=F32) for h in range(nh)]

    def pair_values(ja, jb, h):
        return jnp.concatenate([vt_ref[ja, h], vt_ref[jb, h]], axis=1)

    def pending_pv(j0):
        pa, pb = jnp.maximum(j0 - 2, 0), jnp.maximum(j0 - 1, 0)
        return [jnp.dot(pair_values(pa, pb, h), p_ref[h], preferred_element_type=F32) for h in range(nh)]

    def past(k, carry):
        scores = pair_scores(2 * k)
        pv = pending_pv(2 * k)
        sm = [softmax(h, scores[h]) for h in range(nh)]
        for h in range(nh):
            acc_ref[h] = al_ref[h] * acc_ref[h] + pv[h]
        for h in range(nh):
            p_ref[h], al_ref[h] = sm[h]
        return carry

    npairs = i // 2
    lax.fori_loop(0, npairs, past, 0)
    build_queries(qn_ref, i + 1, qan_ref)

    def finish(scores, last_values):
        pv = pending_pv(2 * npairs)
        sm = [softmax(h, scores[h]) for h in range(nh)]
        last = [jnp.dot(last_values(h), sm[h][0], preferred_element_type=F32) for h in range(nh)]
        outs = []
        for h in range(nh):
            acc = sm[h][1] * (al_ref[h] * acc_ref[h] + pv[h]) + last[h]
            outs.append(acc[0:hd] / acc[hd:hd + 1])
        ot = jnp.concatenate(outs, axis=0)
        o_ref[...] = (ot.T * _silu(z_ref[...].astype(F32))).astype(o_ref.dtype)

    @pl.when(i % 2 == 0)
    def _():
        finish([jnp.dot(ka_ref[h, i], qa_ref[h], preferred_element_type=F32) + mask_ref[blk:2 * blk, :]
                for h in range(nh)], lambda h: vt_ref[i, h])

    @pl.when(i % 2 == 1)
    def _():
        scores = pair_scores(i - 1)
        finish([scores[h] + mask_ref[...] for h in range(nh)], lambda h: pair_values(i - 1, i, h))


def _moba(proj, slopes, bsz, s):
    nb = s // MOBA_BLOCK
    assert nb <= MOBA_NBL and 2 * MOBA_HD == LANES and 3 * MOBA_NBL + 3 <= MOBA_HD
    blk = MOBA_BLOCK
    aug, avg, mask = _moba_constants(s)
    qblock = lambda c0: (lambda b, i, *_: (b * nb + i, c0 // BRANCH_W))
    qnext = lambda b, i, *_: (b * nb + jnp.minimum(i + 1, nb - 1), COL_QA // BRANCH_W)
    whole = lambda c0: (lambda b, i, *_: (b, c0 // BRANCH_W))
    return pl.pallas_call(
        _moba_body,
        grid_spec=pltpu.PrefetchScalarGridSpec(
            num_scalar_prefetch=1,
            grid=(bsz, nb),
            in_specs=[
                pl.BlockSpec((blk, BRANCH_W), qblock(COL_QA)),
                pl.BlockSpec((blk, BRANCH_W), qnext),
                pl.BlockSpec((s, BRANCH_W), whole(COL_KA)),
                pl.BlockSpec((s, BRANCH_W), whole(COL_VA)),
                pl.BlockSpec((blk, BRANCH_W), qblock(COL_ZA)),
                pl.BlockSpec((2, s, LANES), lambda b, i, *_: (0, 0, 0)),
                pl.BlockSpec((MOBA_NBL, s), lambda b, i, *_: (0, 0)),
                pl.BlockSpec((2 * blk, blk), lambda b, i, *_: (0, 0)),
            ],
            out_specs=pl.BlockSpec((blk, BRANCH_W), lambda b, i, *_: (b * nb + i, 0)),
            scratch_shapes=[
                pltpu.VMEM((MOBA_HEADS, nb, blk, LANES), BF16),
                pltpu.VMEM((nb, MOBA_HEADS, MOBA_VROWS, blk), BF16),
                pltpu.VMEM((MOBA_HEADS * MOBA_GATE_ROWS, LANES), BF16),
                pltpu.VMEM((MOBA_HEADS, LANES, blk), BF16),
                pltpu.VMEM((MOBA_HEADS, LANES, blk), BF16),
                pltpu.VMEM((MOBA_HEADS, 1, blk), F32),
                pltpu.VMEM((MOBA_HEADS, MOBA_VROWS, blk), F32),
                pltpu.VMEM((MOBA_HEADS, 2 * blk, blk), BF16),
                pltpu.VMEM((MOBA_HEADS, 1, blk), F32),
            ],
        ),
        out_shape=jax.ShapeDtypeStruct((bsz * s, BRANCH_W), BF16),
        compiler_params=pltpu.CompilerParams(
            dimension_semantics=("parallel", "arbitrary"), vmem_limit_bytes=VMEM_LIMIT),
        name="moba",
    )(slopes, proj, proj, proj, proj, proj, aug, avg, mask)


HGRN_TILE = 256
HGRN_SUB = 2


def _hgrn_body(f_ref, i_ref, q_ref, g_ref, lbl_ref, nw_ref, lmat_ref, tril_ref, o_ref, st_ref):
    r, c, d = HGRN_TILE, HGRN_CHUNK, HGRN_D
    n_chunks = r // c
    heads = range(HGRN_HEADS)
    tiles = range(HGRN_SUB)
    chunks = range(n_chunks)
    cols = [slice(h * d, (h + 1) * d) for h in heads]
    rows = [slice(t * r, (t + 1) * r) for t in tiles]
    nt_dims = (((1,), (1,)), ((), ()))

    @pl.when(pl.program_id(1) == 0)
    def _():
        st_ref[...] = jnp.zeros_like(st_ref)

    logits = lbl_ref[...]
    ex = jnp.exp(logits - jnp.max(logits, axis=0, keepdims=True))
    lb = ex[0:1, :] / jnp.sum(ex, axis=0, keepdims=True)

    kk, b = [], []
    for t in tiles:
        fl = f_ref[rows[t], :].astype(F32)
        log_f = jnp.log(lb + (1.0 - lb) * _sigmoid(fl))
        kk.append((1.0 - lb) * _sigmoid(-fl))
        lmat = lmat_ref[...]
        hi, mid, lo = [jnp.dot(lmat, piece.astype(BF16), preferred_element_type=F32)
                       for piece in _split3(log_f)]
        b.append(hi + mid + lo)

    q_t, k_t32, k_t, dec = [], [], [], []
    for t in tiles:
        q_t.append((q_ref[rows[t], :].astype(F32) * jnp.exp(b[t])).astype(BF16))
        kt = kk[t] * jnp.exp(-b[t])
        k_t32.append(kt)
        k_t.append(kt.astype(BF16))
        dec.append([jnp.exp(b[t][ci * c + c - 1:ci * c + c, :]) for ci in chunks])

    a = [[lax.dot_general(q_t[t][:, cols[h]], k_t[t][:, cols[h]], nt_dims, preferred_element_type=F32)
          for h in heads] for t in tiles]
    u_t = []
    for t in tiles:
        per_head = []
        for h in heads:
            vt = i_ref[rows[t], cols[h]].astype(F32).T.astype(BF16)
            blocks = []
            for ci in chunks:
                slab = (k_t32[t][ci * c:(ci + 1) * c, cols[h]] * dec[t][ci][:, cols[h]]).astype(BF16)
                pieces = ([jnp.zeros((c, ci * d), BF16)] if ci else []) + [slab]
                if ci < n_chunks - 1:
                    pieces.append(jnp.zeros((c, (n_chunks - 1 - ci) * d), BF16))
                blocks.append(jnp.concatenate(pieces, axis=1))
            kd_blk = jnp.concatenate(blocks, axis=0)
            per_head.append(jnp.dot(vt, kd_blk, preferred_element_type=F32))
        u_t.append(per_head)
    o_intra = [[jnp.dot((a[t][h] * tril_ref[...]).astype(BF16), i_ref[rows[t], cols[h]],
                        preferred_element_type=F32) for h in heads] for t in tiles]

    states = [[None] * HGRN_HEADS for _ in tiles]
    for h in heads:
        st = st_ref[h]
        for t in tiles:
            before = []
            for ci in chunks:
                before.append(st.astype(BF16))
                st = st * dec[t][ci][:, cols[h]] + u_t[t][h][:, ci * d:(ci + 1) * d]
            states[t][h] = before
        st_ref[h] = st

    for t in tiles:
        for h in heads:
            o_inter = [lax.dot_general(q_t[t][ci * c:(ci + 1) * c, cols[h]], states[t][h][ci], nt_dims,
                                       preferred_element_type=F32) for ci in chunks]
            o = o_intra[t][h] + jnp.concatenate(o_inter, axis=0)
            on = o * lax.rsqrt(jnp.mean(o * o, axis=-1, keepdims=True) + EPS) * nw_ref[...]
            o_ref[rows[t], cols[h]] = (on * _silu(g_ref[rows[t], cols[h]].astype(F32))).astype(o_ref.dtype)


def _hgrn(proj, lb_logits, norm_w, bsz, s):
    r, c = HGRN_TILE, HGRN_CHUNK
    rs = r * HGRN_SUB
    nt = s // rs
    ri = np.arange(r)
    same = (ri[:, None] // c) == (ri[None, :] // c)
    tril = (same & (ri[None, :] <= ri[:, None])).astype(np.float32)
    col = lambda c0: (lambda b, t: (b * nt + t, c0 // BRANCH_W))
    const = lambda shape: pl.BlockSpec(shape, lambda b, t: (0,) * len(shape))
    return pl.pallas_call(
        _hgrn_body,
        grid=(bsz, nt),
        in_specs=[
            pl.BlockSpec((rs, BRANCH_W), col(COL_FB)),
            pl.BlockSpec((rs, BRANCH_W), col(COL_IB)),
            pl.BlockSpec((rs, BRANCH_W), col(COL_QB)),
            pl.BlockSpec((rs, BRANCH_W), col(COL_GB)),
            const((lb_logits.shape[0], BRANCH_W)),
            const((1, HGRN_D)),
            const((r, r)),
            const((r, r)),
        ],
        out_specs=pl.BlockSpec((rs, BRANCH_W), lambda b, t: (b * nt + t, 0)),
        out_shape=jax.ShapeDtypeStruct((bsz * s, BRANCH_W), BF16),
        scratch_shapes=[pltpu.VMEM((HGRN_HEADS, HGRN_D, HGRN_D), F32)],
        compiler_params=pltpu.CompilerParams(
            dimension_semantics=("parallel", "arbitrary"), vmem_limit_bytes=VMEM_LIMIT),
        name="hgrn",
    )(proj, proj, proj, proj, lb_logits, norm_w.reshape(1, HGRN_D), tril.astype(BF16), tril)


def _xattn_body(q_ref, z_ref, mem_ref, mw_ref, wkv_ref, o_ref, km_ref, vm_ref):
    @pl.when(pl.program_id(1) == 0)
    def _():
        m = mem_ref[0]
        mn = m * lax.rsqrt(jnp.mean(m * m, axis=-1, keepdims=True) + EPS) * mw_ref[...]
        kv = jnp.dot(mn.astype(BF16), wkv_ref[...], preferred_element_type=F32)
        km_ref[...] = kv[:, :BRANCH_W].astype(BF16)
        vm_ref[...] = kv[:, BRANCH_W:].astype(BF16)

    for h in range(XA_HEADS):
        cols = slice(h * XA_HD, (h + 1) * XA_HD)
        logits = lax.dot_general(q_ref[:, cols], km_ref[:, cols], (((1,), (1,)), ((), ())),
                                 preferred_element_type=F32) * (XA_HD ** -0.5 * LOG2E)
        mx = jnp.max(logits, axis=-1, keepdims=True)
        p = jnp.exp2(logits - mx)
        den = jnp.sum(p, axis=-1, keepdims=True)
        o = jnp.dot(p.astype(BF16), vm_ref[:, cols], preferred_element_type=F32) / den
        o_ref[:, cols] = (o * _silu(z_ref[:, cols].astype(F32))).astype(o_ref.dtype)


def _xattn(proj, mem, mem_norm_w, wkv_bf16, bsz, s, tq=1024):
    n_mem, d = mem.shape[1], mem.shape[2]
    nq = s // tq
    return pl.pallas_call(
        _xattn_body,
        grid=(bsz, nq),
        in_specs=[
            pl.BlockSpec((tq, BRANCH_W), lambda b, i: (b * nq + i, COL_QC // BRANCH_W)),
            pl.BlockSpec((tq, BRANCH_W), lambda b, i: (b * nq + i, COL_ZC // BRANCH_W)),
            pl.BlockSpec((1, n_mem, d), lambda b, i: (b, 0, 0)),
            pl.BlockSpec((1, d), lambda b, i: (0, 0)),
            pl.BlockSpec((d, 2 * BRANCH_W), lambda b, i: (0, 0)),
        ],
        out_specs=pl.BlockSpec((tq, BRANCH_W), lambda b, i: (b * nq + i, 0)),
        out_shape=jax.ShapeDtypeStruct((bsz * s, BRANCH_W), BF16),
        scratch_shapes=[pltpu.VMEM((n_mem, BRANCH_W), BF16), pltpu.VMEM((n_mem, BRANCH_W), BF16)],
        compiler_params=pltpu.CompilerParams(
            dimension_semantics=("parallel", "arbitrary"), vmem_limit_bytes=VMEM_LIMIT),
        name="xattn",
    )(proj, proj, mem, mem_norm_w.reshape(1, d), wkv_bf16)


def _merge_body(x_ref, ga_ref, gb_ref, gc_ref, ya_ref, yb_ref, yc_ref,
                wa_ref, wb_ref, wc_ref, wo_ref, pw_ref, o_ref):
    tm = x_ref.shape[0]
    halves = [slice(0, tm // 2), slice(tm // 2, tm)]
    branches = ((ga_ref, ya_ref, wa_ref), (gb_ref, yb_ref, wb_ref), (gc_ref, yc_ref, wc_ref))
    proj = [[jnp.dot(y_ref[rows, :], w_ref[...], preferred_element_type=F32) for _, y_ref, w_ref in branches]
            for rows in halves]
    gated = [[_sigmoid(g_ref[rows, :].astype(F32)) * proj[k][n] for n, (g_ref, _, _) in enumerate(branches)]
             for k, rows in enumerate(halves)]
    merged = [gated[k][0] + gated[k][1] + gated[k][2] for k in range(2)]
    y = [jnp.dot(merged[k].astype(BF16), wo_ref[...], preferred_element_type=F32) for k in range(2)]
    for k, rows in enumerate(halves):
        yn = y[k] * lax.rsqrt(jnp.mean(y[k] * y[k], axis=-1, keepdims=True) + EPS) * pw_ref[...]
        o_ref[rows, :] = x_ref[rows, :] + yn


def _merge(x2, proj, ya, yb, yc, wa, wb, wc, wo, post_w, tm=1024):
    t, d = x2.shape
    row = lambda c: (lambda i: (i, c))
    const = lambda shape: pl.BlockSpec(shape, lambda i: (0, 0))
    return pl.pallas_call(
        _merge_body,
        grid=(t // tm,),
        in_specs=[
            pl.BlockSpec((tm, d), row(0)),
            pl.BlockSpec((tm, d), row(COL_GATE_A // d)),
            pl.BlockSpec((tm, d), row(COL_GATE_B // d)),
            pl.BlockSpec((tm, d), row(COL_GATE_C // d)),
            pl.BlockSpec((tm, BRANCH_W), row(0)),
            pl.BlockSpec((tm, BRANCH_W), row(0)),
            pl.BlockSpec((tm, BRANCH_W), row(0)),
            const((BRANCH_W, d)), const((BRANCH_W, d)), const((BRANCH_W, d)),
            const((d, d)), const((1, d)),
        ],
        out_specs=pl.BlockSpec((tm, d), row(0)),
        out_shape=jax.ShapeDtypeStruct((t, d), F32),
        compiler_params=pltpu.CompilerParams(
            dimension_semantics=("parallel",), vmem_limit_bytes=VMEM_LIMIT),
        name="merge",
    )(x2, proj, proj, proj, ya, yb, yc, wa, wb, wc, wo, post_w.reshape(1, d))


def kernel(x, mem, pre_norm_w, w_in, hgrn_lb_logits, hgrn_norm_w, mem_norm_w, w_mem_kv,
           w_branch_a, w_branch_b, w_branch_c, w_out, post_norm_w):
    bsz, s, d = x.shape
    assert w_in.shape[0] == 1 and w_in.shape[2] == PROJ_TOTAL and d == 1024
    assert s % MOBA_BLOCK == 0 and s % (HGRN_TILE * HGRN_SUB) == 0
    x2 = x.reshape(bsz * s, d)
    proj = _in_proj(x2, pre_norm_w[0], w_in[0].astype(BF16))
    slopes = jnp.exp2(-8.0 * jnp.arange(1, MOBA_HEADS + 1, dtype=F32) / MOBA_HEADS)
    ya = _moba(proj, slopes, bsz, s)
    yb = _hgrn(proj, hgrn_lb_logits, hgrn_norm_w[0], bsz, s)
    yc = _xattn(proj, mem, mem_norm_w[0], w_mem_kv[0].astype(BF16), bsz, s)
    out = _merge(x2, proj, ya, yb, yc, w_branch_a[0].astype(BF16), w_branch_b[0].astype(BF16),
                 w_branch_c[0].astype(BF16), w_out[0].astype(BF16), post_norm_w[0])
    return out.reshape(bsz, s, d)
```

```python
import jax
import jax.numpy as jnp
import numpy as np
from jax import lax
from jax.experimental import pallas as pl
from jax.experimental.pallas import tpu as pltpu

F32 = jnp.float32
BF16 = jnp.bfloat16
EPS = 1e-6

MOBA_HEADS, MOBA_HD, MOBA_BLOCK, MOBA_TOPK = 8, 64, 256, 3
HGRN_HEADS, HGRN_D, HGRN_CHUNK = 4, 128, 32
XA_HEADS, XA_HD = 4, 128
BRANCH_W = 512
LANES = 128
COL_QA, COL_KA, COL_VA, COL_ZA = 0, 512, 1024, 1536
COL_FB, COL_IB, COL_QB, COL_GB = 2048, 2560, 3072, 3584
COL_QC, COL_ZC = 4096, 4608
COL_GATE_A, COL_GATE_B, COL_GATE_C = 5120, 6144, 7168
PROJ_TOTAL = COL_GATE_C + 1024

NEG_BIG = -1e30
LOG2E = 1.4426950408889634
MOBA_VROWS = MOBA_HD + 16
MOBA_NBL = 16
MOBA_GATE_ROWS = 3 * MOBA_NBL
VMEM_LIMIT = 56 * 1024 * 1024


def _sigmoid(z):
    return 1.0 / (1.0 + jnp.exp2(z * (-LOG2E)))


def _silu(z):
    return z * _sigmoid(z)


def _split3(a):
    hi = a.astype(BF16).astype(F32)
    r = a - hi
    mid = r.astype(BF16).astype(F32)
    lo = (r - mid).astype(BF16).astype(F32)
    return hi, mid, lo


IN_PROJ_CHUNK = 1024


def _in_proj_body(x_ref, nw_ref, w_ref, o_ref):
    x = x_ref[...]
    ms = jnp.mean(x * x, axis=-1, keepdims=True)
    h = (x * lax.rsqrt(ms + EPS) * nw_ref[...]).astype(BF16)
    for c0 in range(0, o_ref.shape[1], IN_PROJ_CHUNK):
        cols = slice(c0, c0 + IN_PROJ_CHUNK)
        o_ref[:, cols] = jnp.dot(h, w_ref[:, cols], preferred_element_type=F32).astype(o_ref.dtype)


def _in_proj(x2, norm_w, w_bf16, tm=512):
    t, d = x2.shape
    n = w_bf16.shape[1]
    return pl.pallas_call(
        _in_proj_body,
        grid=(t // tm,),
        in_specs=[
            pl.BlockSpec((tm, d), lambda i: (i, 0)),
            pl.BlockSpec((1, d), lambda i: (0, 0)),
            pl.BlockSpec((d, n), lambda i: (0, 0), pipeline_mode=pl.Buffered(1)),
        ],
        out_specs=pl.BlockSpec((tm, n), lambda i: (i, 0)),
        out_shape=jax.ShapeDtypeStruct((t, n), BF16),
        compiler_params=pltpu.CompilerParams(
            dimension_semantics=("parallel",), vmem_limit_bytes=VMEM_LIMIT),
        name="in_proj",
    )(x2, norm_w.reshape(1, d), w_bf16)


def _moba_constants(s):
    blk, hd, nbl = MOBA_BLOCK, MOBA_HD, MOBA_NBL
    row = np.arange(s)
    off = (row % blk).astype(np.float32)
    onehot = (row[:, None] // blk == np.arange(nbl)[None, :]).astype(np.float32)
    bias_lanes = np.concatenate([onehot, onehot, onehot, off[:, None], off[:, None], off[:, None],
                                 np.zeros((s, hd - 3 * nbl - 3), np.float32)], axis=1)
    zeros = np.zeros((s, hd), np.float32)
    a0 = np.concatenate([zeros, bias_lanes], axis=1)
    a1 = np.concatenate([bias_lanes, zeros], axis=1)
    aug = np.stack([a0, a1]).astype(BF16)
    avg = ((np.arange(nbl)[:, None] == (row // blk)[None, :]).astype(np.float32) / blk).astype(BF16)
    causal = np.where(np.arange(blk)[:, None] <= np.arange(blk)[None, :], 0.0, -np.inf).astype(np.float32)
    mask = np.concatenate([np.zeros((blk, blk), np.float32), causal])
    return aug, avg, mask


def _moba_body(slopes_ref, q_ref, qn_ref, k_ref, v_ref, z_ref, aug_ref, avg_ref, mask_ref, o_ref,
               ka_ref, vt_ref, kbp_ref, qa_ref, qan_ref, m_ref, acc_ref, p_ref, al_ref):
    s = k_ref.shape[0]
    nb = s // MOBA_BLOCK
    blk = MOBA_BLOCK
    nh = MOBA_HEADS
    hd = MOBA_HD
    nbl, gr = MOBA_NBL, MOBA_GATE_ROWS
    i = pl.program_id(1)

    def build_queries(src_ref, iq, dst_ref):
        nidx = lax.broadcasted_iota(jnp.int32, (nbl, blk), 0)
        nidx_f = nidx.astype(F32)
        qoff = lax.broadcasted_iota(jnp.int32, (nbl, blk), 1)
        dist0 = (qoff + (iq - nidx) * blk).astype(F32)
        qts = [(src_ref[:, hp * LANES:(hp + 1) * LANES].astype(F32) * (hd ** -0.5)).T
               for hp in range(nh // 2)]
        gates = [jnp.dot(kbp_ref[hp * 2 * gr:(hp + 1) * 2 * gr, :], qts[hp].astype(BF16),
                         preferred_element_type=F32) for hp in range(nh // 2)]
        for hp in range(nh // 2):
            qs = qts[hp] * LOG2E
            for e in range(2):
                h = 2 * hp + e
                gg = gates[hp][e * gr:(e + 1) * gr]
                g = gg[0:nbl] + gg[nbl:2 * nbl] + gg[2 * nbl:3 * nbl]
                gv = jnp.where(nidx < iq, g, -jnp.inf)
                sel = nidx == iq
                for _ in range(MOBA_TOPK):
                    mx = jnp.max(gv, axis=0, keepdims=True)
                    first = jnp.min(jnp.where(gv == mx, nidx_f, float(nbl)), axis=0, keepdims=True)
                    pick = nidx_f == jnp.where(mx > -jnp.inf, first, float(nbl))
                    sel = sel | pick
                    gv = jnp.where(pick, -jnp.inf, gv)
                sl = slopes_ref[h] * LOG2E
                bias = jnp.where(sel, -sl * dist0, NEG_BIG)
                b_hi, b_mid, b_lo = _split3(bias)
                s_hi, s_mid, s_lo = _split3(jnp.full((nbl, blk), sl, F32))
                srow = jnp.where(nidx == 0, s_hi, jnp.where(nidx == 1, s_mid, jnp.where(nidx == 2, s_lo, 0.0)))
                qh = qs[0:hd] if e == 0 else qs[hd:2 * hd]
                parts = [qh, b_hi, b_mid, b_lo, srow] if e == 0 else [b_hi, b_mid, b_lo, srow, qh]
                dst_ref[h] = jnp.concatenate(parts, axis=0).astype(BF16)

    @pl.when(i == 0)
    def _():
        lane = lax.broadcasted_iota(jnp.int32, (1, LANES), 1)
        first_half = lane < hd
        kbar = jnp.dot(avg_ref[...], k_ref[...], preferred_element_type=F32)
        for h in range(nh):
            hp, e = h // 2, h % 2
            cols = slice(hp * LANES, (hp + 1) * LANES)
            keep = first_half if e == 0 else jnp.logical_not(first_half)
            for j in range(nb):
                rows = slice(j * blk, (j + 1) * blk)
                ka_ref[h, j] = jnp.where(keep, k_ref[rows, cols], aug_ref[e, rows, :])
            pieces = _split3(jnp.where(keep, kbar[:, cols], 0.0))
            for p in range(3):
                kbp_ref[h * gr + p * nbl:h * gr + (p + 1) * nbl, :] = pieces[p].astype(BF16)
        ones_row = jnp.where(lax.broadcasted_iota(jnp.int32, (MOBA_VROWS - hd, blk), 0) == 0, 1.0, 0.0)
        for j in range(nb):
            vt = v_ref[j * blk:(j + 1) * blk, :].astype(F32).T
            for h in range(nh):
                vt_ref[j, h, 0:hd, :] = vt[h * hd:(h + 1) * hd].astype(BF16)
                vt_ref[j, h, hd:MOBA_VROWS, :] = ones_row.astype(BF16)
        build_queries(q_ref, 0, qa_ref)
        p_ref[...] = jnp.zeros(p_ref.shape, BF16)
        al_ref[...] = jnp.ones(al_ref.shape, F32)

    @pl.when(i > 0)
    def _():
        qa_ref[...] = qan_ref[...]

    m_ref[...] = jnp.full(m_ref.shape, -jnp.inf, F32)
    acc_ref[...] = jnp.zeros(acc_ref.shape, F32)

    def softmax(h, sT):
        m_old = m_ref[h]
        m_new = jnp.maximum(m_old, jnp.max(sT, axis=0, keepdims=True))
        m_ref[h] = m_new
        return jnp.exp2(sT - m_new).astype(BF16), jnp.exp2(m_old - m_new)

    def pair_scores(j0):
        return [jnp.dot(ka_ref[h, pl.ds(j0, 2)].reshape(2 * blk, LANES), qa_ref[h],
                        preferred_element_type=F32) for h in range(nh)]

    def pair_values(ja, jb, h):
        return jnp.concatenate([vt_ref[ja, h], vt_ref[jb, h]], axis=1)

    def pending_pv(j0):
        pa, pb = jnp.maximum(j0 - 2, 0), jnp.maximum(j0 - 1, 0)
        return [jnp.dot(pair_values(pa, pb, h), p_ref[h], preferred_element_type=F32) for h in range(nh)]

    def past(k, carry):
        scores = pair_scores(2 * k)
        pv = pending_pv(2 * k)
        sm = [softmax(h, scores[h]) for h in range(nh)]
        for h in range(nh):
            acc_ref[h] = al_ref[h] * acc_ref[h] + pv[h]
        for h in range(nh):
            p_ref[h], al_ref[h] = sm[h]
        return carry

    npairs = i // 2
    lax.fori_loop(0, npairs, past, 0)

    def finish(scores, last_values):
        pv = pending_pv(2 * npairs)
        build_queries(qn_ref, i + 1, qan_ref)
        sm = [softmax(h, scores[h]) for h in range(nh)]
        last = [jnp.dot(last_values(h), sm[h][0], preferred_element_type=F32) for h in range(nh)]
        outs = []
        for h in range(nh):
            acc = sm[h][1] * (al_ref[h] * acc_ref[h] + pv[h]) + last[h]
            outs.append(acc[0:hd] / acc[hd:hd + 1])
        ot = jnp.concatenate(outs, axis=0)
        o_ref[...] = (ot.T * _silu(z_ref[...].astype(F32))).astype(o_ref.dtype)

    @pl.when(i % 2 == 0)
    def _():
        finish([jnp.dot(ka_ref[h, i], qa_ref[h], preferred_element_type=F32) + mask_ref[blk:2 * blk, :]
                for h in range(nh)], lambda h: vt_ref[i, h])

    @pl.when(i % 2 == 1)
    def _():
        scores = pair_scores(i - 1)
        finish([scores[h] + mask_ref[...] for h in range(nh)], lambda h: pair_values(i - 1, i, h))


def _moba(proj, slopes, bsz, s):
    nb = s // MOBA_BLOCK
    assert nb <= MOBA_NBL and 2 * MOBA_HD == LANES and 3 * MOBA_NBL + 3 <= MOBA_HD
    blk = MOBA_BLOCK
    aug, avg, mask = _moba_constants(s)
    qblock = lambda c0: (lambda b, i, *_: (b * nb + i, c0 // BRANCH_W))
    qnext = lambda b, i, *_: (b * nb + jnp.minimum(i + 1, nb - 1), COL_QA // BRANCH_W)
    whole = lambda c0: (lambda b, i, *_: (b, c0 // BRANCH_W))
    return pl.pallas_call(
        _moba_body,
        grid_spec=pltpu.PrefetchScalarGridSpec(
            num_scalar_prefetch=1,
            grid=(bsz, nb),
            in_specs=[
                pl.BlockSpec((blk, BRANCH_W), qblock(COL_QA)),
                pl.BlockSpec((blk, BRANCH_W), qnext),
                pl.BlockSpec((s, BRANCH_W), whole(COL_KA)),
                pl.BlockSpec((s, BRANCH_W), whole(COL_VA)),
                pl.BlockSpec((blk, BRANCH_W), qblock(COL_ZA)),
                pl.BlockSpec((2, s, LANES), lambda b, i, *_: (0, 0, 0)),
                pl.BlockSpec((MOBA_NBL, s), lambda b, i, *_: (0, 0)),
                pl.BlockSpec((2 * blk, blk), lambda b, i, *_: (0, 0)),
            ],
            out_specs=pl.BlockSpec((blk, BRANCH_W), lambda b, i, *_: (b * nb + i, 0)),
            scratch_shapes=[
                pltpu.VMEM((MOBA_HEADS, nb, blk, LANES), BF16),
                pltpu.VMEM((nb, MOBA_HEADS, MOBA_VROWS, blk), BF16),
                pltpu.VMEM((MOBA_HEADS * MOBA_GATE_ROWS, LANES), BF16),
                pltpu.VMEM((MOBA_HEADS, LANES, blk), BF16),
                pltpu.VMEM((MOBA_HEADS, LANES, blk), BF16),
                pltpu.VMEM((MOBA_HEADS, 1, blk), F32),
                pltpu.VMEM((MOBA_HEADS, MOBA_VROWS, blk), F32),
                pltpu.VMEM((MOBA_HEADS, 2 * blk, blk), BF16),
                pltpu.VMEM((MOBA_HEADS, 1, blk), F32),
            ],
        ),
        out_shape=jax.ShapeDtypeStruct((bsz * s, BRANCH_W), BF16),
        compiler_params=pltpu.CompilerParams(
            dimension_semantics=("parallel", "arbitrary"), vmem_limit_bytes=VMEM_LIMIT),
        name="moba",
    )(slopes, proj, proj, proj, proj, proj, aug, avg, mask)


HGRN_TILE = 256
HGRN_SUB = 2


def _hgrn_body(f_ref, i_ref, q_ref, g_ref, lbl_ref, nw_ref, lmat_ref, tril_ref, o_ref, st_ref):
    r, c, d = HGRN_TILE, HGRN_CHUNK, HGRN_D
    n_chunks = r // c
    heads = range(HGRN_HEADS)
    tiles = range(HGRN_SUB)
    chunks = range(n_chunks)
    cols = [slice(h * d, (h + 1) * d) for h in heads]
    rows = [slice(t * r, (t + 1) * r) for t in tiles]
    nt_dims = (((1,), (1,)), ((), ()))

    @pl.when(pl.program_id(1) == 0)
    def _():
        st_ref[...] = jnp.zeros_like(st_ref)

    logits = lbl_ref[...]
    ex = jnp.exp(logits - jnp.max(logits, axis=0, keepdims=True))
    lb = ex[0:1, :] / jnp.sum(ex, axis=0, keepdims=True)

    kk, b = [], []
    for t in tiles:
        fl = f_ref[rows[t], :].astype(F32)
        log_f = jnp.log(lb + (1.0 - lb) * _sigmoid(fl))
        kk.append((1.0 - lb) * _sigmoid(-fl))
        lmat = lmat_ref[...]
        hi, mid, lo = [jnp.dot(lmat, piece.astype(BF16), preferred_element_type=F32)
                       for piece in _split3(log_f)]
        b.append(hi + mid + lo)

    q_t, k_t32, k_t, dec = [], [], [], []
    for t in tiles:
        q_t.append((q_ref[rows[t], :].astype(F32) * jnp.exp(b[t])).astype(BF16))
        kt = kk[t] * jnp.exp(-b[t])
        k_t32.append(kt)
        k_t.append(kt.astype(BF16))
        dec.append([jnp.exp(b[t][ci * c + c - 1:ci * c + c, :]) for ci in chunks])

    a = [[lax.dot_general(q_t[t][:, cols[h]], k_t[t][:, cols[h]], nt_dims, preferred_element_type=F32)
          for h in heads] for t in tiles]
    u_t = []
    for t in tiles:
        per_head = []
        for h in heads:
            vt = i_ref[rows[t], cols[h]].astype(F32).T.astype(BF16)
            blocks = []
            for ci in chunks:
                slab = (k_t32[t][ci * c:(ci + 1) * c, cols[h]] * dec[t][ci][:, cols[h]]).astype(BF16)
                pieces = ([jnp.zeros((c, ci * d), BF16)] if ci else []) + [slab]
                if ci < n_chunks - 1:
                    pieces.append(jnp.zeros((c, (n_chunks - 1 - ci) * d), BF16))
                blocks.append(jnp.concatenate(pieces, axis=1))
            kd_blk = jnp.concatenate(blocks, axis=0)
            per_head.append(jnp.dot(vt, kd_blk, preferred_element_type=F32))
        u_t.append(per_head)
    o_intra = [[jnp.dot((a[t][h] * tril_ref[...]).astype(BF16), i_ref[rows[t], cols[h]],
                        preferred_element_type=F32) for h in heads] for t in tiles]

    states = [[None] * HGRN_HEADS for _ in tiles]
    for h in heads:
        st = st_ref[h]
        for t in tiles:
            before = []
            for ci in chunks:
                before.append(st.astype(BF16))
                st = st * dec[t][ci][:, cols[h]] + u_t[t][h][:, ci * d:(ci + 1) * d]
            states[t][h] = before
        st_ref[h] = st

    for t in tiles:
        for h in heads:
            o_inter = [lax.dot_general(q_t[t][ci * c:(ci + 1) * c, cols[h]], states[t][h][ci], nt_dims,
                                       preferred_element_type=F32) for ci in chunks]
            o = o_intra[t][h] + jnp.concatenate(o_inter, axis=0)
            on = o * lax.rsqrt(jnp.mean(o * o, axis=-1, keepdims=True) + EPS) * nw_ref[...]
            o_ref[rows[t], cols[h]] = (on * _silu(g_ref[rows[t], cols[h]].astype(F32))).astype(o_ref.dtype)


def _hgrn(proj, lb_logits, norm_w, bsz, s):
    r, c = HGRN_TILE, HGRN_CHUNK
    rs = r * HGRN_SUB
    nt = s // rs
    ri = np.arange(r)
    same = (ri[:, None] // c) == (ri[None, :] // c)
    tril = (same & (ri[None, :] <= ri[:, None])).astype(np.float32)
    col = lambda c0: (lambda b, t: (b * nt + t, c0 // BRANCH_W))
    const = lambda shape: pl.BlockSpec(shape, lambda b, t: (0,) * len(shape))
    return pl.pallas_call(
        _hgrn_body,
        grid=(bsz, nt),
        in_specs=[
            pl.BlockSpec((rs, BRANCH_W), col(COL_FB)),
            pl.BlockSpec((rs, BRANCH_W), col(COL_IB)),
            pl.BlockSpec((rs, BRANCH_W), col(COL_QB)),
            pl.BlockSpec((rs, BRANCH_W), col(COL_GB)),
            const((lb_logits.shape[0], BRANCH_W)),
            const((1, HGRN_D)),
            const((r, r)),
            const((r, r)),
        ],
        out_specs=pl.BlockSpec((rs, BRANCH_W), lambda b, t: (b * nt + t, 0)),
        out_shape=jax.ShapeDtypeStruct((bsz * s, BRANCH_W), BF16),
        scratch_shapes=[pltpu.VMEM((HGRN_HEADS, HGRN_D, HGRN_D), F32)],
        compiler_params=pltpu.CompilerParams(
            dimension_semantics=("parallel", "arbitrary"), vmem_limit_bytes=VMEM_LIMIT),
        name="hgrn",
    )(proj, proj, proj, proj, lb_logits, norm_w.reshape(1, HGRN_D), tril.astype(BF16), tril)


def _xattn_body(q_ref, z_ref, mem_ref, mw_ref, wkv_ref, o_ref, km_ref, vm_ref):
    @pl.when(pl.program_id(1) == 0)
    def _():
        m = mem_ref[0]
        mn = m * lax.rsqrt(jnp.mean(m * m, axis=-1, keepdims=True) + EPS) * mw_ref[...]
        kv = jnp.dot(mn.astype(BF16), wkv_ref[...], preferred_element_type=F32)
        km_ref[...] = kv[:, :BRANCH_W].astype(BF16)
        vm_ref[...] = kv[:, BRANCH_W:].astype(BF16)

    for h in range(XA_HEADS):
        cols = slice(h * XA_HD, (h + 1) * XA_HD)
        logits = lax.dot_general(q_ref[:, cols], km_ref[:, cols], (((1,), (1,)), ((), ())),
                                 preferred_element_type=F32) * (XA_HD ** -0.5 * LOG2E)
        mx = jnp.max(logits, axis=-1, keepdims=True)
        p = jnp.exp2(logits - mx)
        den = jnp.sum(p, axis=-1, keepdims=True)
        o = jnp.dot(p.astype(BF16), vm_ref[:, cols], preferred_element_type=F32) / den
        o_ref[:, cols] = (o * _silu(z_ref[:, cols].astype(F32))).astype(o_ref.dtype)


def _xattn(proj, mem, mem_norm_w, wkv_bf16, bsz, s, tq=1024):
    n_mem, d = mem.shape[1], mem.shape[2]
    nq = s // tq
    return pl.pallas_call(
        _xattn_body,
        grid=(bsz, nq),
        in_specs=[
            pl.BlockSpec((tq, BRANCH_W), lambda b, i: (b * nq + i, COL_QC // BRANCH_W)),
            pl.BlockSpec((tq, BRANCH_W), lambda b, i: (b * nq + i, COL_ZC // BRANCH_W)),
            pl.BlockSpec((1, n_mem, d), lambda b, i: (b, 0, 0)),
            pl.BlockSpec((1, d), lambda b, i: (0, 0)),
            pl.BlockSpec((d, 2 * BRANCH_W), lambda b, i: (0, 0)),
        ],
        out_specs=pl.BlockSpec((tq, BRANCH_W), lambda b, i: (b * nq + i, 0)),
        out_shape=jax.ShapeDtypeStruct((bsz * s, BRANCH_W), BF16),
        scratch_shapes=[pltpu.VMEM((n_mem, BRANCH_W), BF16), pltpu.VMEM((n_mem, BRANCH_W), BF16)],
        compiler_params=pltpu.CompilerParams(
            dimension_semantics=("parallel", "arbitrary"), vmem_limit_bytes=VMEM_LIMIT),
        name="xattn",
    )(proj, proj, mem, mem_norm_w.reshape(1, d), wkv_bf16)


def _merge_body(x_ref, ga_ref, gb_ref, gc_ref, ya_ref, yb_ref, yc_ref,
                wa_ref, wb_ref, wc_ref, wo_ref, pw_ref, o_ref):
    tm = x_ref.shape[0]
    halves = [slice(0, tm // 2), slice(tm // 2, tm)]
    branches = ((ga_ref, ya_ref, wa_ref), (gb_ref, yb_ref, wb_ref), (gc_ref, yc_ref, wc_ref))
    proj = [[jnp.dot(y_ref[rows, :], w_ref[...], preferred_element_type=F32) for _, y_ref, w_ref in branches]
            for rows in halves]
    gated = [[_sigmoid(g_ref[rows, :].astype(F32)) * proj[k][n] for n, (g_ref, _, _) in enumerate(branches)]
             for k, rows in enumerate(halves)]
    merged = [gated[k][0] + gated[k][1] + gated[k][2] for k in range(2)]
    y = [jnp.dot(merged[k].astype(BF16), wo_ref[...], preferred_element_type=F32) for k in range(2)]
    for k, rows in enumerate(halves):
        yn = y[k] * lax.rsqrt(jnp.mean(y[k] * y[k], axis=-1, keepdims=True) + EPS) * pw_ref[...]
        o_ref[rows, :] = x_ref[rows, :] + yn


def _merge(x2, proj, ya, yb, yc, wa, wb, wc, wo, post_w, tm=1024):
    t, d = x2.shape
    row = lambda c: (lambda i: (i, c))
    const = lambda shape: pl.BlockSpec(shape, lambda i: (0, 0))
    return pl.pallas_call(
        _merge_body,
        grid=(t // tm,),
        in_specs=[
            pl.BlockSpec((tm, d), row(0)),
            pl.BlockSpec((tm, d), row(COL_GATE_A // d)),
            pl.BlockSpec((tm, d), row(COL_GATE_B // d)),
            pl.BlockSpec((tm, d), row(COL_GATE_C // d)),
            pl.BlockSpec((tm, BRANCH_W), row(0)),
            pl.BlockSpec((tm, BRANCH_W), row(0)),
            pl.BlockSpec((tm, BRANCH_W), row(0)),
            const((BRANCH_W, d)), const((BRANCH_W, d)), const((BRANCH_W, d)),
            const((d, d)), const((1, d)),
        ],
        out_specs=pl.BlockSpec((tm, d), row(0)),
        out_shape=jax.ShapeDtypeStruct((t, d), F32),
        compiler_params=pltpu.CompilerParams(
            dimension_semantics=("parallel",), vmem_limit_bytes=VMEM_LIMIT),
        name="merge",
    )(x2, proj, proj, proj, ya, yb, yc, wa, wb, wc, wo, post_w.reshape(1, d))


def kernel(x, mem, pre_norm_w, w_in, hgrn_lb_logits, hgrn_norm_w, mem_norm_w, w_mem_kv,
           w_branch_a, w_branch_b, w_branch_c, w_out, post_norm_w):
    bsz, s, d = x.shape
    assert w_in.shape[0] == 1 and w_in.shape[2] == PROJ_TOTAL and d == 1024
    assert s % MOBA_BLOCK == 0 and s % (HGRN_TILE * HGRN_SUB) == 0
    x2 = x.reshape(bsz * s, d)
    proj = _in_proj(x2, pre_norm_w[0], w_in[0].astype(BF16))
    slopes = jnp.exp2(-8.0 * jnp.arange(1, MOBA_HEADS + 1, dtype=F32) / MOBA_HEADS)
    ya = _moba(proj, slopes, bsz, s)
    yb = _hgrn(proj, hgrn_lb_logits, hgrn_norm_w[0], bsz, s)
    yc = _xattn(proj, mem, mem_norm_w[0], w_mem_kv[0].astype(BF16), bsz, s)
    out = _merge(x2, proj, ya, yb, yc, w_branch_a[0].astype(BF16), w_branch_b[0].astype(BF16),
                 w_branch_c[0].astype(BF16), w_out[0].astype(BF16), post_norm_w[0])
    return out.reshape(bsz, s, d)
```

```python
import jax
import jax.numpy as jnp
import numpy as np
from jax import lax
from jax.experimental import pallas as pl
from jax.experimental.pallas import tpu as pltpu

F32 = jnp.float32
BF16 = jnp.bfloat16
EPS = 1e-6

MOBA_HEADS, MOBA_HD, MOBA_BLOCK, MOBA_TOPK = 8, 64, 256, 3
HGRN_HEADS, HGRN_D, HGRN_CHUNK = 4, 128, 32
XA_HEADS, XA_HD = 4, 128
BRANCH_W = 512
LANES = 128
COL_QA, COL_KA, COL_VA, COL_ZA = 0, 512, 1024, 1536
COL_FB, COL_IB, COL_QB, COL_GB = 2048, 2560, 3072, 3584
COL_QC, COL_ZC = 4096, 4608
COL_GATE_A, COL_GATE_B, COL_GATE_C = 5120, 6144, 7168
PROJ_TOTAL = COL_GATE_C + 1024

NEG_BIG = -1e30
LOG2E = 1.4426950408889634
MOBA_VROWS = MOBA_HD + 16
MOBA_NBL = 16
MOBA_GATE_ROWS = 3 * MOBA_NBL
VMEM_LIMIT = 56 * 1024 * 1024


def _sigmoid(z):
    return 1.0 / (1.0 + jnp.exp2(z * (-LOG2E)))


def _silu(z):
    return z * _sigmoid(z)


def _split3(a):
    hi = a.astype(BF16).astype(F32)
    r = a - hi
    mid = r.astype(BF16).astype(F32)
    lo = (r - mid).astype(BF16).astype(F32)
    return hi, mid, lo


IN_PROJ_CHUNK = 1024


def _in_proj_body(x_ref, nw_ref, w_ref, o_ref):
    x = x_ref[...]
    ms = jnp.mean(x * x, axis=-1, keepdims=True)
    h = (x * lax.rsqrt(ms + EPS) * nw_ref[...]).astype(BF16)
    for c0 in range(0, o_ref.shape[1], IN_PROJ_CHUNK):
        cols = slice(c0, c0 + IN_PROJ_CHUNK)
        o_ref[:, cols] = jnp.dot(h, w_ref[:, cols], preferred_element_type=F32).astype(o_ref.dtype)


def _in_proj(x2, norm_w, w_bf16, tm=512):
    t, d = x2.shape
    n = w_bf16.shape[1]
    return pl.pallas_call(
        _in_proj_body,
        grid=(t // tm,),
        in_specs=[
            pl.BlockSpec((tm, d), lambda i: (i, 0)),
            pl.BlockSpec((1, d), lambda i: (0, 0)),
            pl.BlockSpec((d, n), lambda i: (0, 0), pipeline_mode=pl.Buffered(1)),
        ],
        out_specs=pl.BlockSpec((tm, n), lambda i: (i, 0)),
        out_shape=jax.ShapeDtypeStruct((t, n), BF16),
        compiler_params=pltpu.CompilerParams(
            dimension_semantics=("parallel",), vmem_limit_bytes=VMEM_LIMIT),
        name="in_proj",
    )(x2, norm_w.reshape(1, d), w_bf16)


def _moba_constants(s):
    blk, hd, nbl = MOBA_BLOCK, MOBA_HD, MOBA_NBL
    row = np.arange(s)
    off = (row % blk).astype(np.float32)
    onehot = (row[:, None] // blk == np.arange(nbl)[None, :]).astype(np.float32)
    bias_lanes = np.concatenate([onehot, onehot, onehot, off[:, None], off[:, None], off[:, None],
                                 np.zeros((s, hd - 3 * nbl - 3), np.float32)], axis=1)
    zeros = np.zeros((s, hd), np.float32)
    a0 = np.concatenate([zeros, bias_lanes], axis=1)
    a1 = np.concatenate([bias_lanes, zeros], axis=1)
    aug = np.stack([a0, a1]).astype(BF16)
    avg = ((np.arange(nbl)[:, None] == (row // blk)[None, :]).astype(np.float32) / blk).astype(BF16)
    causal = np.where(np.arange(blk)[:, None] <= np.arange(blk)[None, :], 0.0, -np.inf).astype(np.float32)
    mask = np.concatenate([np.zeros((blk, blk), np.float32), causal])
    return aug, avg, mask


def _moba_body(slopes_ref, q_ref, qn_ref, k_ref, v_ref, z_ref, aug_ref, avg_ref, mask_ref, o_ref,
               ka_ref, vt_ref, kbp_ref, qa_ref, qan_ref, m_ref, acc_ref, p_ref, al_ref):
    s = k_ref.shape[0]
    nb = s // MOBA_BLOCK
    blk = MOBA_BLOCK
    nh = MOBA_HEADS
    hd = MOBA_HD
    nbl, gr = MOBA_NBL, MOBA_GATE_ROWS
    i = pl.program_id(1)

    def build_queries(src_ref, iq, dst_ref):
        nidx = lax.broadcasted_iota(jnp.int32, (nbl, blk), 0)
        nidx_f = nidx.astype(F32)
        qoff = lax.broadcasted_iota(jnp.int32, (nbl, blk), 1)
        dist0 = (qoff + (iq - nidx) * blk).astype(F32)
        qts = [(src_ref[:, hp * LANES:(hp + 1) * LANES].astype(F32) * (hd ** -0.5)).T
               for hp in range(nh // 2)]
        gates = [jnp.dot(kbp_ref[hp * 2 * gr:(hp + 1) * 2 * gr, :], qts[hp].astype(BF16),
                         preferred_element_type=F32) for hp in range(nh // 2)]
        for hp in range(nh // 2):
            qs = qts[hp] * LOG2E
            for e in range(2):
                h = 2 * hp + e
                gg = gates[hp][e * gr:(e + 1) * gr]
                g = gg[0:nbl] + gg[nbl:2 * nbl] + gg[2 * nbl:3 * nbl]
                gv = jnp.where(nidx < iq, g, -jnp.inf)
                sel = nidx == iq
                for _ in range(MOBA_TOPK):
                    mx = jnp.max(gv, axis=0, keepdims=True)
                    first = jnp.min(jnp.where(gv == mx, nidx_f, float(nbl)), axis=0, keepdims=True)
                    pick = nidx_f == jnp.where(mx > -jnp.inf, first, float(nbl))
                    sel = sel | pick
                    gv = jnp.where(pick, -jnp.inf, gv)
                sl = slopes_ref[h] * LOG2E
                bias = jnp.where(sel, -sl * dist0, NEG_BIG)
                b_hi, b_mid, b_lo = _split3(bias)
                s_hi, s_mid, s_lo = _split3(jnp.full((nbl, blk), sl, F32))
                srow = jnp.where(nidx == 0, s_hi, jnp.where(nidx == 1, s_mid, jnp.where(nidx == 2, s_lo, 0.0)))
                qh = qs[0:hd] if e == 0 else qs[hd:2 * hd]
                parts = [qh, b_hi, b_mid, b_lo, srow] if e == 0 else [b_hi, b_mid, b_lo, srow, qh]
                dst_ref[h] = jnp.concatenate(parts, axis=0).astype(BF16)

    @pl.when(i == 0)
    def _():
        lane = lax.broadcasted_iota(jnp.int32, (1, LANES), 1)
        first_half = lane < hd
        kbar = jnp.dot(avg_ref[...], k_ref[...], preferred_element_type=F32)
        for h in range(nh):
            hp, e = h // 2, h % 2
            cols = slice(hp * LANES, (hp + 1) * LANES)
            keep = first_half if e == 0 else jnp.logical_not(first_half)
            for j in range(nb):
                rows = slice(j * blk, (j + 1) * blk)
                ka_ref[h, j] = jnp.where(keep, k_ref[rows, cols], aug_ref[e, rows, :])
            pieces = _split3(jnp.where(keep, kbar[:, cols], 0.0))
            for p in range(3):
                kbp_ref[h * gr + p * nbl:h * gr + (p + 1) * nbl, :] = pieces[p].astype(BF16)
        ones_row = jnp.where(lax.broadcasted_iota(jnp.int32, (MOBA_VROWS - hd, blk), 0) == 0, 1.0, 0.0)
        for j in range(nb):
            vt = v_ref[j * blk:(j + 1) * blk, :].astype(F32).T
            for h in range(nh):
                vt_ref[j, h, 0:hd, :] = vt[h * hd:(h + 1) * hd].astype(BF16)
                vt_ref[j, h, hd:MOBA_VROWS, :] = ones_row.astype(BF16)
        build_queries(q_ref, 0, qa_ref)
        p_ref[...] = jnp.zeros(p_ref.shape, BF16)
        al_ref[...] = jnp.ones(al_ref.shape, F32)

    @pl.when(i > 0)
    def _():
        qa_ref[...] = qan_ref[...]

    m_ref[...] = jnp.full(m_ref.shape, -jnp.inf, F32)
    acc_ref[...] = jnp.zeros(acc_ref.shape, F32)

    def softmax(h, sT):
        m_old = m_ref[h]
        m_new = jnp.maximum(m_old, jnp.max(sT, axis=0, keepdims=True))
        m_ref[h] = m_new
        return jnp.exp2(sT - m_new).astype(BF16), jnp.exp2(m_old - m_new)

    def pair_scores(j0):
        return [jnp.dot(ka_ref[h, pl.ds(j0, 2)].reshape(2 * blk, LANES), qa_ref[h],
                        preferred_element_type=F32) for h in range(nh)]

    def pair_values(ja, jb, h):
        return jnp.concatenate([vt_ref[ja, h], vt_ref[jb, h]], axis=1)

    def pending_pv(j0):
        pa, pb = jnp.maximum(j0 - 2, 0), jnp.maximum(j0 - 1, 0)
        return [jnp.dot(pair_values(pa, pb, h), p_ref[h], preferred_element_type=F32) for h in range(nh)]

    def past(k, carry):
        scores = pair_scores(2 * k)
        pv = pending_pv(2 * k)
        sm = [softmax(h, scores[h]) for h in range(nh)]
        for h in range(nh):
            acc_ref[h] = al_ref[h] * acc_ref[h] + pv[h]
        for h in range(nh):
            p_ref[h], al_ref[h] = sm[h]
        return carry

    npairs = i // 2
    lax.fori_loop(0, npairs, past, 0)

    def finish(scores, last_values):
        pv = pending_pv(2 * npairs)
        build_queries(qn_ref, i + 1, qan_ref)
        sm = [softmax(h, scores[h]) for h in range(nh)]
        last = [jnp.dot(last_values(h), sm[h][0], preferred_element_type=F32) for h in range(nh)]
        outs = []
        for h in range(nh):
            acc = sm[h][1] * (al_ref[h] * acc_ref[h] + pv[h]) + last[h]
            outs.append(acc[0:hd] / acc[hd:hd + 1])
        ot = jnp.concatenate(outs, axis=0)
        o_ref[...] = (ot.T * _silu(z_ref[...].astype(F32))).astype(o_ref.dtype)

    @pl.when(i % 2 == 0)
    def _():
        finish([jnp.dot(ka_ref[h, i], qa_ref[h], preferred_element_type=F32) + mask_ref[blk:2 * blk, :]
                for h in range(nh)], lambda h: vt_ref[i, h])

    @pl.when(i % 2 == 1)
    def _():
        scores = pair_scores(i - 1)
        finish([scores[h] + mask_ref[...] for h in range(nh)], lambda h: pair_values(i - 1, i, h))


def _moba(proj, slopes, bsz, s):
    nb = s // MOBA_BLOCK
    assert nb <= MOBA_NBL and 2 * MOBA_HD == LANES and 3 * MOBA_NBL + 3 <= MOBA_HD
    blk = MOBA_BLOCK
    aug, avg, mask = _moba_constants(s)
    qblock = lambda c0: (lambda b, i, *_: (b * nb + i, c0 // BRANCH_W))
    qnext = lambda b, i, *_: (b * nb + jnp.minimum(i + 1, nb - 1), COL_QA // BRANCH_W)
    whole = lambda c0: (lambda b, i, *_: (b, c0 // BRANCH_W))
    return pl.pallas_call(
        _moba_body,
        grid_spec=pltpu.PrefetchScalarGridSpec(
            num_scalar_prefetch=1,
            grid=(bsz, nb),
            in_specs=[
                pl.BlockSpec((blk, BRANCH_W), qblock(COL_QA)),
                pl.BlockSpec((blk, BRANCH_W), qnext),
                pl.BlockSpec((s, BRANCH_W), whole(COL_KA)),
                pl.BlockSpec((s, BRANCH_W), whole(COL_VA)),
                pl.BlockSpec((blk, BRANCH_W), qblock(COL_ZA)),
                pl.BlockSpec((2, s, LANES), lambda b, i, *_: (0, 0, 0)),
                pl.BlockSpec((MOBA_NBL, s), lambda b, i, *_: (0, 0)),
                pl.BlockSpec((2 * blk, blk), lambda b, i, *_: (0, 0)),
            ],
            out_specs=pl.BlockSpec((blk, BRANCH_W), lambda b, i, *_: (b * nb + i, 0)),
            scratch_shapes=[
                pltpu.VMEM((MOBA_HEADS, nb, blk, LANES), BF16),
                pltpu.VMEM((nb, MOBA_HEADS, MOBA_VROWS, blk), BF16),
                pltpu.VMEM((MOBA_HEADS * MOBA_GATE_ROWS, LANES), BF16),
                pltpu.VMEM((MOBA_HEADS, LANES, blk), BF16),
                pltpu.VMEM((MOBA_HEADS, LANES, blk), BF16),
                pltpu.VMEM((MOBA_HEADS, 1, blk), F32),
                pltpu.VMEM((MOBA_HEADS, MOBA_VROWS, blk), F32),
                pltpu.VMEM((MOBA_HEADS, 2 * blk, blk), BF16),
                pltpu.VMEM((MOBA_HEADS, 1, blk), F32),
            ],
        ),
        out_shape=jax.ShapeDtypeStruct((bsz * s, BRANCH_W), BF16),
        compiler_params=pltpu.CompilerParams(
            dimension_semantics=("parallel", "arbitrary"), vmem_limit_bytes=VMEM_LIMIT),
        name="moba",
    )(slopes, proj, proj, proj, proj, proj, aug, avg, mask)


HGRN_TILE = 256
HGRN_SUB = 4


def _hgrn_body(f_ref, i_ref, q_ref, g_ref, lbl_ref, nw_ref, lmat_ref, o_ref, st_ref):
    r, c, d = HGRN_TILE, HGRN_CHUNK, HGRN_D
    n_chunks = r // c
    heads = range(HGRN_HEADS)
    tiles = range(HGRN_SUB)
    chunks = range(n_chunks)
    cols = [slice(h * d, (h + 1) * d) for h in heads]
    rows = [slice(t * r, (t + 1) * r) for t in tiles]
    nt_dims = (((1,), (1,)), ((), ()))

    @pl.when(pl.program_id(1) == 0)
    def _():
        st_ref[...] = jnp.zeros_like(st_ref)

    logits = lbl_ref[...]
    ex = jnp.exp(logits - jnp.max(logits, axis=0, keepdims=True))
    lb = ex[0:1, :] / jnp.sum(ex, axis=0, keepdims=True)

    kk, b = [], []
    for t in tiles:
        fl = f_ref[rows[t], :].astype(F32)
        log_f = jnp.log(lb + (1.0 - lb) * _sigmoid(fl))
        kk.append((1.0 - lb) * _sigmoid(-fl))
        lmat = lmat_ref[...]
        hi, mid, lo = [jnp.dot(lmat, piece.astype(BF16), preferred_element_type=F32)
                       for piece in _split3(log_f)]
        b.append(hi + mid + lo)

    q_t, k_t32, k_t, dec = [], [], [], []
    for t in tiles:
        q_t.append((q_ref[rows[t], :].astype(F32) * jnp.exp(b[t])).astype(BF16))
        kt = kk[t] * jnp.exp(-b[t])
        k_t32.append(kt)
        k_t.append(kt.astype(BF16))
        dec.append([jnp.exp(b[t][ci * c + c - 1:ci * c + c, :]) for ci in chunks])

    a = [[lax.dot_general(q_t[t][:, cols[h]], k_t[t][:, cols[h]], nt_dims, preferred_element_type=F32)
          for h in heads] for t in tiles]
    u_t = []
    for t in tiles:
        per_head = []
        for h in heads:
            vt = i_ref[rows[t], cols[h]].astype(F32).T.astype(BF16)
            blocks = []
            for ci in chunks:
                slab = (k_t32[t][ci * c:(ci + 1) * c, cols[h]] * dec[t][ci][:, cols[h]]).astype(BF16)
                pieces = ([jnp.zeros((c, ci * d), BF16)] if ci else []) + [slab]
                if ci < n_chunks - 1:
                    pieces.append(jnp.zeros((c, (n_chunks - 1 - ci) * d), BF16))
                blocks.append(jnp.concatenate(pieces, axis=1))
            kd_blk = jnp.concatenate(blocks, axis=0)
            per_head.append(jnp.dot(vt, kd_blk, preferred_element_type=F32))
        u_t.append(per_head)
    o_intra = [[jnp.dot(a[t][h].astype(BF16) * lmat_ref[...], i_ref[rows[t], cols[h]],
                        preferred_element_type=F32) for h in heads] for t in tiles]

    states = [[None] * HGRN_HEADS for _ in tiles]
    for h in heads:
        st = st_ref[h]
        for t in tiles:
            before = []
            for ci in chunks:
                before.append(st.astype(BF16))
                st = st * dec[t][ci][:, cols[h]] + u_t[t][h][:, ci * d:(ci + 1) * d]
            states[t][h] = before
        st_ref[h] = st

    for t in tiles:
        for h in heads:
            o_inter = [lax.dot_general(q_t[t][ci * c:(ci + 1) * c, cols[h]], states[t][h][ci], nt_dims,
                                       preferred_element_type=F32) for ci in chunks]
            o = o_intra[t][h] + jnp.concatenate(o_inter, axis=0)
            on = o * lax.rsqrt(jnp.mean(o * o, axis=-1, keepdims=True) + EPS) * nw_ref[...]
            o_ref[rows[t], cols[h]] = (on * _silu(g_ref[rows[t], cols[h]].astype(F32))).astype(o_ref.dtype)


def _hgrn(proj, lb_logits, norm_w, bsz, s):
    r, c = HGRN_TILE, HGRN_CHUNK
    rs = r * HGRN_SUB
    nt = s // rs
    ri = np.arange(r)
    same = (ri[:, None] // c) == (ri[None, :] // c)
    tril = (same & (ri[None, :] <= ri[:, None])).astype(np.float32)
    col = lambda c0: (lambda b, t: (b * nt + t, c0 // BRANCH_W))
    const = lambda shape: pl.BlockSpec(shape, lambda b, t: (0,) * len(shape))
    return pl.pallas_call(
        _hgrn_body,
        grid=(bsz, nt),
        in_specs=[
            pl.BlockSpec((rs, BRANCH_W), col(COL_FB)),
            pl.BlockSpec((rs, BRANCH_W), col(COL_IB)),
            pl.BlockSpec((rs, BRANCH_W), col(COL_QB)),
            pl.BlockSpec((rs, BRANCH_W), col(COL_GB)),
            const((lb_logits.shape[0], BRANCH_W)),
            const((1, HGRN_D)),
            const((r, r)),
        ],
        out_specs=pl.BlockSpec((rs, BRANCH_W), lambda b, t: (b * nt + t, 0)),
        out_shape=jax.ShapeDtypeStruct((bsz * s, BRANCH_W), BF16),
        scratch_shapes=[pltpu.VMEM((HGRN_HEADS, HGRN_D, HGRN_D), F32)],
        compiler_params=pltpu.CompilerParams(
            dimension_semantics=("parallel", "arbitrary"), vmem_limit_bytes=VMEM_LIMIT),
        name="hgrn",
    )(proj, proj, proj, proj, lb_logits, norm_w.reshape(1, HGRN_D), tril.astype(BF16))


def _xattn_body(q_ref, z_ref, mem_ref, mw_ref, wkv_ref, o_ref, km_ref, vm_ref):
    @pl.when(pl.program_id(1) == 0)
    def _():
        m = mem_ref[0]
        mn = m * lax.rsqrt(jnp.mean(m * m, axis=-1, keepdims=True) + EPS) * mw_ref[...]
        kv = jnp.dot(mn.astype(BF16), wkv_ref[...], preferred_element_type=F32)
        km_ref[...] = kv[:, :BRANCH_W].astype(BF16)
        vm_ref[...] = kv[:, BRANCH_W:].astype(BF16)

    for h in range(XA_HEADS):
        cols = slice(h * XA_HD, (h + 1) * XA_HD)
        logits = lax.dot_general(q_ref[:, cols], km_ref[:, cols], (((1,), (1,)), ((), ())),
                                 preferred_element_type=F32) * (XA_HD ** -0.5 * LOG2E)
        mx = jnp.max(logits, axis=-1, keepdims=True)
        p = jnp.exp2(logits - mx)
        den = jnp.sum(p, axis=-1, keepdims=True)
        o = jnp.dot(p.astype(BF16), vm_ref[:, cols], preferred_element_type=F32) / den
        o_ref[:, cols] = (o * _silu(z_ref[:, cols].astype(F32))).astype(o_ref.dtype)


def _xattn(proj, mem, mem_norm_w, wkv_bf16, bsz, s, tq=1024):
    n_mem, d = mem.shape[1], mem.shape[2]
    nq = s // tq
    return pl.pallas_call(
        _xattn_body,
        grid=(bsz, nq),
        in_specs=[
            pl.BlockSpec((tq, BRANCH_W), lambda b, i: (b * nq + i, COL_QC // BRANCH_W)),
            pl.BlockSpec((tq, BRANCH_W), lambda b, i: (b * nq + i, COL_ZC // BRANCH_W)),
            pl.BlockSpec((1, n_mem, d), lambda b, i: (b, 0, 0)),
            pl.BlockSpec((1, d), lambda b, i: (0, 0)),
            pl.BlockSpec((d, 2 * BRANCH_W), lambda b, i: (0, 0)),
        ],
        out_specs=pl.BlockSpec((tq, BRANCH_W), lambda b, i: (b * nq + i, 0)),
        out_shape=jax.ShapeDtypeStruct((bsz * s, BRANCH_W), BF16),
        scratch_shapes=[pltpu.VMEM((n_mem, BRANCH_W), BF16), pltpu.VMEM((n_mem, BRANCH_W), BF16)],
        compiler_params=pltpu.CompilerParams(
            dimension_semantics=("parallel", "arbitrary"), vmem_limit_bytes=VMEM_LIMIT),
        name="xattn",
    )(proj, proj, mem, mem_norm_w.reshape(1, d), wkv_bf16)


def _merge_body(x_ref, ga_ref, gb_ref, gc_ref, ya_ref, yb_ref, yc_ref,
                wa_ref, wb_ref, wc_ref, wo_ref, pw_ref, o_ref):
    tm = x_ref.shape[0]
    halves = [slice(0, tm // 2), slice(tm // 2, tm)]
    branches = ((ga_ref, ya_ref, wa_ref), (gb_ref, yb_ref, wb_ref), (gc_ref, yc_ref, wc_ref))
    proj = [[jnp.dot(y_ref[rows, :], w_ref[...], preferred_element_type=F32) for _, y_ref, w_ref in branches]
            for rows in halves]
    gated = [[_sigmoid(g_ref[rows, :].astype(F32)) * proj[k][n] for n, (g_ref, _, _) in enumerate(branches)]
             for k, rows in enumerate(halves)]
    merged = [gated[k][0] + gated[k][1] + gated[k][2] for k in range(2)]
    y = [jnp.dot(merged[k].astype(BF16), wo_ref[...], preferred_element_type=F32) for k in range(2)]
    for k, rows in enumerate(halves):
        yn = y[k] * lax.rsqrt(jnp.mean(y[k] * y[k], axis=-1, keepdims=True) + EPS) * pw_ref[...]
        o_ref[rows, :] = x_ref[rows, :] + yn


def _merge(x2, proj, ya, yb, yc, wa, wb, wc, wo, post_w, tm=1024):
    t, d = x2.shape
    row = lambda c: (lambda i: (i, c))
    const = lambda shape: pl.BlockSpec(shape, lambda i: (0, 0))
    return pl.pallas_call(
        _merge_body,
        grid=(t // tm,),
        in_specs=[
            pl.BlockSpec((tm, d), row(0)),
            pl.BlockSpec((tm, d), row(COL_GATE_A // d)),
            pl.BlockSpec((tm, d), row(COL_GATE_B // d)),
            pl.BlockSpec((tm, d), row(COL_GATE_C // d)),
            pl.BlockSpec((tm, BRANCH_W), row(0)),
            pl.BlockSpec((tm, BRANCH_W), row(0)),
            pl.BlockSpec((tm, BRANCH_W), row(0)),
            const((BRANCH_W, d)), const((BRANCH_W, d)), const((BRANCH_W, d)),
            const((d, d)), const((1, d)),
        ],
        out_specs=pl.BlockSpec((tm, d), row(0)),
        out_shape=jax.ShapeDtypeStruct((t, d), F32),
        compiler_params=pltpu.CompilerParams(
            dimension_semantics=("parallel",), vmem_limit_bytes=VMEM_LIMIT),
        name="merge",
    )(x2, proj, proj, proj, ya, yb, yc, wa, wb, wc, wo, post_w.reshape(1, d))


def kernel(x, mem, pre_norm_w, w_in, hgrn_lb_logits, hgrn_norm_w, mem_norm_w, w_mem_kv,
           w_branch_a, w_branch_b, w_branch_c, w_out, post_norm_w):
    bsz, s, d = x.shape
    assert w_in.shape[0] == 1 and w_in.shape[2] == PROJ_TOTAL and d == 1024
    assert s % MOBA_BLOCK == 0 and s % (HGRN_TILE * HGRN_SUB) == 0
    x2 = x.reshape(bsz * s, d)
    proj = _in_proj(x2, pre_norm_w[0], w_in[0].astype(BF16))
    slopes = jnp.exp2(-8.0 * jnp.arange(1, MOBA_HEADS + 1, dtype=F32) / MOBA_HEADS)
    ya = _moba(proj, slopes, bsz, s)
    yb = _hgrn(proj, hgrn_lb_logits, hgrn_norm_w[0], bsz, s)
    yc = _xattn(proj, mem, mem_norm_w[0], w_mem_kv[0].astype(BF16), bsz, s)
    out = _merge(x2, proj, ya, yb, yc, w_branch_a[0].astype(BF16), w_branch_b[0].astype(BF16),
                 w_branch_c[0].astype(BF16), w_out[0].astype(BF16), post_norm_w[0])
    return out.reshape(bsz, s, d)
```

```python
import jax
import jax.numpy as jnp
import numpy as np
from jax import lax
from jax.experimental import pallas as pl
from jax.experimental.pallas import tpu as pltpu

F32 = jnp.float32
BF16 = jnp.bfloat16
EPS = 1e-6

MOBA_HEADS, MOBA_HD, MOBA_BLOCK, MOBA_TOPK = 8, 64, 256, 3
HGRN_HEADS, HGRN_D, HGRN_CHUNK = 4, 128, 32
XA_HEADS, XA_HD = 4, 128
BRANCH_W = 512
LANES = 128
COL_QA, COL_KA, COL_VA, COL_ZA = 0, 512, 1024, 1536
COL_FB, COL_IB, COL_QB, COL_GB = 2048, 2560, 3072, 3584
COL_QC, COL_ZC = 4096, 4608
COL_GATE_A, COL_GATE_B, COL_GATE_C = 5120, 6144, 7168
PROJ_TOTAL = COL_GATE_C + 1024

NEG_BIG = -1e30
LOG2E = 1.4426950408889634
MOBA_VROWS = MOBA_HD + 16
MOBA_NBL = 16
MOBA_GATE_ROWS = 3 * MOBA_NBL
VMEM_LIMIT = 56 * 1024 * 1024


def _sigmoid(z):
    return 1.0 / (1.0 + jnp.exp2(z * (-LOG2E)))


def _silu(z):
    return z * _sigmoid(z)


def _split3(a):
    hi = a.astype(BF16).astype(F32)
    r = a - hi
    mid = r.astype(BF16).astype(F32)
    lo = (r - mid).astype(BF16).astype(F32)
    return hi, mid, lo


IN_PROJ_CHUNK = 1024


def _in_proj_body(x_ref, nw_ref, w_ref, o_ref):
    x = x_ref[...]
    ms = jnp.mean(x * x, axis=-1, keepdims=True)
    h = (x * lax.rsqrt(ms + EPS) * nw_ref[...]).astype(BF16)
    for c0 in range(0, o_ref.shape[1], IN_PROJ_CHUNK):
        cols = slice(c0, c0 + IN_PROJ_CHUNK)
        o_ref[:, cols] = jnp.dot(h, w_ref[:, cols], preferred_element_type=F32).astype(o_ref.dtype)


def _in_proj(x2, norm_w, w_bf16, tm=512):
    t, d = x2.shape
    n = w_bf16.shape[1]
    return pl.pallas_call(
        _in_proj_body,
        grid=(t // tm,),
        in_specs=[
            pl.BlockSpec((tm, d), lambda i: (i, 0)),
            pl.BlockSpec((1, d), lambda i: (0, 0)),
            pl.BlockSpec((d, n), lambda i: (0, 0), pipeline_mode=pl.Buffered(1)),
        ],
        out_specs=pl.BlockSpec((tm, n), lambda i: (i, 0)),
        out_shape=jax.ShapeDtypeStruct((t, n), BF16),
        compiler_params=pltpu.CompilerParams(
            dimension_semantics=("parallel",), vmem_limit_bytes=VMEM_LIMIT),
        name="in_proj",
    )(x2, norm_w.reshape(1, d), w_bf16)


def _moba_constants(s):
    blk, hd, nbl = MOBA_BLOCK, MOBA_HD, MOBA_NBL
    row = np.arange(s)
    off = (row % blk).astype(np.float32)
    onehot = (row[:, None] // blk == np.arange(nbl)[None, :]).astype(np.float32)
    bias_lanes = np.concatenate([onehot, onehot, onehot, off[:, None], off[:, None], off[:, None],
                                 np.zeros((s, hd - 3 * nbl - 3), np.float32)], axis=1)
    zeros = np.zeros((s, hd), np.float32)
    a0 = np.concatenate([zeros, bias_lanes], axis=1)
    a1 = np.concatenate([bias_lanes, zeros], axis=1)
    aug = np.stack([a0, a1]).astype(BF16)
    avg = ((np.arange(nbl)[:, None] == (row // blk)[None, :]).astype(np.float32) / blk).astype(BF16)
    causal = np.where(np.arange(blk)[:, None] <= np.arange(blk)[None, :], 0.0, -np.inf).astype(np.float32)
    mask = np.concatenate([np.zeros((blk, blk), np.float32), causal])
    return aug, avg, mask


def _moba_body(slopes_ref, q_ref, qn_ref, k_ref, v_ref, z_ref, aug_ref, avg_ref, mask_ref, o_ref,
               ka_ref, vt_ref, kbp_ref, qa_ref, qan_ref, acc_ref, p_ref, m_ref, al_ref):
    s = k_ref.shape[0]
    nb = s // MOBA_BLOCK
    blk = MOBA_BLOCK
    nh = MOBA_HEADS
    hd = MOBA_HD
    nbl, gr = MOBA_NBL, MOBA_GATE_ROWS
    i = pl.program_id(1)

    def build_queries(src_ref, iq, dst_ref):
        nidx = lax.broadcasted_iota(jnp.int32, (nbl, blk), 0)
        nidx_f = nidx.astype(F32)
        qoff = lax.broadcasted_iota(jnp.int32, (nbl, blk), 1)
        dist0 = (qoff + (iq - nidx) * blk).astype(F32)
        qts = [(src_ref[:, hp * LANES:(hp + 1) * LANES].astype(F32) * (hd ** -0.5)).T
               for hp in range(nh // 2)]
        gates = [jnp.dot(kbp_ref[hp * 2 * gr:(hp + 1) * 2 * gr, :], qts[hp].astype(BF16),
                         preferred_element_type=F32) for hp in range(nh // 2)]
        for hp in range(nh // 2):
            qs = qts[hp] * LOG2E
            for e in range(2):
                h = 2 * hp + e
                gg = gates[hp][e * gr:(e + 1) * gr]
                g = gg[0:nbl] + gg[nbl:2 * nbl] + gg[2 * nbl:3 * nbl]
                gv = jnp.where(nidx < iq, g, -jnp.inf)
                sel = nidx == iq
                for _ in range(MOBA_TOPK):
                    mx = jnp.max(gv, axis=0, keepdims=True)
                    first = jnp.min(jnp.where(gv == mx, nidx_f, float(nbl)), axis=0, keepdims=True)
                    pick = nidx_f == jnp.where(mx > -jnp.inf, first, float(nbl))
                    sel = sel | pick
                    gv = jnp.where(pick, -jnp.inf, gv)
                sl = slopes_ref[h] * LOG2E
                bias = jnp.where(sel, -sl * dist0, NEG_BIG)
                b_hi, b_mid, b_lo = _split3(bias)
                s_hi, s_mid, s_lo = _split3(jnp.full((nbl, blk), sl, F32))
                srow = jnp.where(nidx == 0, s_hi, jnp.where(nidx == 1, s_mid, jnp.where(nidx == 2, s_lo, 0.0)))
                qh = qs[0:hd] if e == 0 else qs[hd:2 * hd]
                parts = [qh, b_hi, b_mid, b_lo, srow] if e == 0 else [b_hi, b_mid, b_lo, srow, qh]
                dst_ref[h] = jnp.concatenate(parts, axis=0).astype(BF16)

    @pl.when(i == 0)
    def _():
        lane = lax.broadcasted_iota(jnp.int32, (1, LANES), 1)
        first_half = lane < hd
        kbar = jnp.dot(avg_ref[...], k_ref[...], preferred_element_type=F32)
        for h in range(nh):
            hp, e = h // 2, h % 2
            cols = slice(hp * LANES, (hp + 1) * LANES)
            keep = first_half if e == 0 else jnp.logical_not(first_half)
            for j in range(nb):
                rows = slice(j * blk, (j + 1) * blk)
                ka_ref[h, j] = jnp.where(keep, k_ref[rows, cols], aug_ref[e, rows, :])
            pieces = _split3(jnp.where(keep, kbar[:, cols], 0.0))
            for p in range(3):
                kbp_ref[h * gr + p * nbl:h * gr + (p + 1) * nbl, :] = pieces[p].astype(BF16)
        ones_row = jnp.where(lax.broadcasted_iota(jnp.int32, (MOBA_VROWS - hd, blk), 0) == 0, 1.0, 0.0)
        for j in range(nb):
            vt = v_ref[j * blk:(j + 1) * blk, :].astype(F32).T
            for h in range(nh):
                vt_ref[j, h, 0:hd, :] = vt[h * hd:(h + 1) * hd].astype(BF16)
                vt_ref[j, h, hd:MOBA_VROWS, :] = ones_row.astype(BF16)
        build_queries(q_ref, 0, qa_ref)
        p_ref[...] = jnp.zeros(p_ref.shape, BF16)
        al_ref[...] = jnp.ones(al_ref.shape, F32)

    @pl.when(i > 0)
    def _():
        qa_ref[...] = qan_ref[...]

    m_ref[...] = jnp.full(m_ref.shape, -jnp.inf, F32)
    acc_ref[...] = jnp.zeros(acc_ref.shape, F32)

    def softmax(h, sT):
        m_old = m_ref[h]
        m_new = jnp.maximum(m_old, jnp.max(sT, axis=0, keepdims=True))
        m_ref[h] = m_new
        return jnp.exp2(sT - m_new).astype(BF16), jnp.exp2(m_old - m_new)

    def pair_scores(j0):
        return [jnp.dot(ka_ref[h, pl.ds(j0, 2)].reshape(2 * blk, LANES), qa_ref[h],
                        preferred_element_type=F32) for h in range(nh)]

    def pair_values(ja, jb, h):
        return jnp.concatenate([vt_ref[ja, h], vt_ref[jb, h]], axis=1)

    def pending_pv(j0):
        pa, pb = jnp.maximum(j0 - 2, 0), jnp.maximum(j0 - 1, 0)
        return [jnp.dot(pair_values(pa, pb, h), p_ref[h], preferred_element_type=F32) for h in range(nh)]

    def past(k, carry):
        scores = pair_scores(2 * k)
        pv = pending_pv(2 * k)
        sm = [softmax(h, scores[h]) for h in range(nh)]
        for h in range(nh):
            acc_ref[h] = al_ref[h] * acc_ref[h] + pv[h]
        for h in range(nh):
            p_ref[h], al_ref[h] = sm[h]
        return carry

    npairs = i // 2
    lax.fori_loop(0, npairs, past, 0)

    def finish(scores, last_values):
        pv = pending_pv(2 * npairs)
        build_queries(qn_ref, i + 1, qan_ref)
        sm = [softmax(h, scores[h]) for h in range(nh)]
        last = [jnp.dot(last_values(h), sm[h][0], preferred_element_type=F32) for h in range(nh)]
        outs = []
        for h in range(nh):
            acc = sm[h][1] * (al_ref[h] * acc_ref[h] + pv[h]) + last[h]
            outs.append(acc[0:hd] / acc[hd:hd + 1])
        ot = jnp.concatenate(outs, axis=0)
        o_ref[...] = (ot.T * _silu(z_ref[...].astype(F32))).astype(o_ref.dtype)

    @pl.when(i % 2 == 0)
    def _():
        finish([jnp.dot(ka_ref[h, i], qa_ref[h], preferred_element_type=F32) + mask_ref[blk:2 * blk, :]
                for h in range(nh)], lambda h: vt_ref[i, h])

    @pl.when(i % 2 == 1)
    def _():
        scores = pair_scores(i - 1)
        finish([scores[h] + mask_ref[...] for h in range(nh)], lambda h: pair_values(i - 1, i, h))


def _moba(proj, slopes, bsz, s):
    nb = s // MOBA_BLOCK
    assert nb <= MOBA_NBL and 2 * MOBA_HD == LANES and 3 * MOBA_NBL + 3 <= MOBA_HD
    blk = MOBA_BLOCK
    aug, avg, mask = _moba_constants(s)
    qblock = lambda c0: (lambda b, i, *_: (b * nb + i, c0 // BRANCH_W))
    qnext = lambda b, i, *_: (b * nb + jnp.minimum(i + 1, nb - 1), COL_QA // BRANCH_W)
    whole = lambda c0: (lambda b, i, *_: (b, c0 // BRANCH_W))
    return pl.pallas_call(
        _moba_body,
        grid_spec=pltpu.PrefetchScalarGridSpec(
            num_scalar_prefetch=1,
            grid=(bsz, nb),
            in_specs=[
                pl.BlockSpec((blk, BRANCH_W), qblock(COL_QA)),
                pl.BlockSpec((blk, BRANCH_W), qnext),
                pl.BlockSpec((s, BRANCH_W), whole(COL_KA)),
                pl.BlockSpec((s, BRANCH_W), whole(COL_VA)),
                pl.BlockSpec((blk, BRANCH_W), qblock(COL_ZA)),
                pl.BlockSpec((2, s, LANES), lambda b, i, *_: (0, 0, 0)),
                pl.BlockSpec((MOBA_NBL, s), lambda b, i, *_: (0, 0)),
                pl.BlockSpec((2 * blk, blk), lambda b, i, *_: (0, 0)),
            ],
            out_specs=pl.BlockSpec((blk, BRANCH_W), lambda b, i, *_: (b * nb + i, 0)),
            scratch_shapes=[
                pltpu.VMEM((MOBA_HEADS, nb, blk, LANES), BF16),
                pltpu.VMEM((nb, MOBA_HEADS, MOBA_VROWS, blk), BF16),
                pltpu.VMEM((MOBA_HEADS * MOBA_GATE_ROWS, LANES), BF16),
                pltpu.VMEM((MOBA_HEADS, LANES, blk), BF16),
                pltpu.VMEM((MOBA_HEADS, LANES, blk), BF16),
                pltpu.VMEM((MOBA_HEADS, MOBA_VROWS, blk), F32),
                pltpu.VMEM((MOBA_HEADS, 2 * blk, blk), BF16),
                pltpu.VMEM((MOBA_HEADS, 1, blk), F32),
                pltpu.VMEM((MOBA_HEADS, 1, blk), F32),
            ],
        ),
        out_shape=jax.ShapeDtypeStruct((bsz * s, BRANCH_W), BF16),
        compiler_params=pltpu.CompilerParams(
            dimension_semantics=("parallel", "arbitrary"), vmem_limit_bytes=VMEM_LIMIT),
        name="moba",
    )(slopes, proj, proj, proj, proj, proj, aug, avg, mask)


HGRN_TILE = 256
HGRN_SUB = 4


def _hgrn_body(f_ref, i_ref, q_ref, g_ref, lbl_ref, nw_ref, lmat_ref, o_ref, st_ref):
    r, c, d = HGRN_TILE, HGRN_CHUNK, HGRN_D
    n_chunks = r // c
    heads = range(HGRN_HEADS)
    tiles = range(HGRN_SUB)
    chunks = range(n_chunks)
    cols = [slice(h * d, (h + 1) * d) for h in heads]
    rows = [slice(t * r, (t + 1) * r) for t in tiles]
    nt_dims = (((1,), (1,)), ((), ()))

    @pl.when(pl.program_id(1) == 0)
    def _():
        st_ref[...] = jnp.zeros_like(st_ref)

    logits = lbl_ref[...]
    ex = jnp.exp(logits - jnp.max(logits, axis=0, keepdims=True))
    lb = ex[0:1, :] / jnp.sum(ex, axis=0, keepdims=True)

    kk, b = [], []
    for t in tiles:
        fl = f_ref[rows[t], :].astype(F32)
        log_f = jnp.log(lb + (1.0 - lb) * _sigmoid(fl))
        kk.append((1.0 - lb) * _sigmoid(-fl))
        lmat = lmat_ref[...]
        hi, mid, lo = [jnp.dot(lmat, piece.astype(BF16), preferred_element_type=F32)
                       for piece in _split3(log_f)]
        b.append(hi + mid + lo)

    q_t, k_t32, k_t, dec = [], [], [], []
    for t in tiles:
        q_t.append((q_ref[rows[t], :].astype(F32) * jnp.exp(b[t])).astype(BF16))
        kt = kk[t] * jnp.exp(-b[t])
        k_t32.append(kt)
        k_t.append(kt.astype(BF16))
        dec.append([jnp.exp(b[t][ci * c + c - 1:ci * c + c, :]) for ci in chunks])

    a = [[lax.dot_general(q_t[t][:, cols[h]], k_t[t][:, cols[h]], nt_dims, preferred_element_type=F32)
          for h in heads] for t in tiles]
    u_t = []
    for t in tiles:
        per_head = []
        for h in heads:
            vt = i_ref[rows[t], cols[h]].astype(F32).T.astype(BF16)
            blocks = []
            for ci in chunks:
                slab = (k_t32[t][ci * c:(ci + 1) * c, cols[h]] * dec[t][ci][:, cols[h]]).astype(BF16)
                pieces = ([jnp.zeros((c, ci * d), BF16)] if ci else []) + [slab]
                if ci < n_chunks - 1:
                    pieces.append(jnp.zeros((c, (n_chunks - 1 - ci) * d), BF16))
                blocks.append(jnp.concatenate(pieces, axis=1))
            kd_blk = jnp.concatenate(blocks, axis=0)
            per_head.append(jnp.dot(vt, kd_blk, preferred_element_type=F32))
        u_t.append(per_head)
    o_intra = [[jnp.dot(a[t][h].astype(BF16) * lmat_ref[...], i_ref[rows[t], cols[h]],
                        preferred_element_type=F32) for h in heads] for t in tiles]

    states = [[None] * HGRN_HEADS for _ in tiles]
    for h in heads:
        st = st_ref[h]
        for t in tiles:
            before = []
            for ci in chunks:
                before.append(st.astype(BF16))
                st = st * dec[t][ci][:, cols[h]] + u_t[t][h][:, ci * d:(ci + 1) * d]
            states[t][h] = before
        st_ref[h] = st

    for t in tiles:
        for h in heads:
            o_inter = [lax.dot_general(q_t[t][ci * c:(ci + 1) * c, cols[h]], states[t][h][ci], nt_dims,
                                       preferred_element_type=F32) for ci in chunks]
            o = o_intra[t][h] + jnp.concatenate(o_inter, axis=0)
            on = o * lax.rsqrt(jnp.mean(o * o, axis=-1, keepdims=True) + EPS) * nw_ref[...]
            o_ref[rows[t], cols[h]] = (on * _silu(g_ref[rows[t], cols[h]].astype(F32))).astype(o_ref.dtype)


def _hgrn(proj, lb_logits, norm_w, bsz, s):
    r, c = HGRN_TILE, HGRN_CHUNK
    rs = r * HGRN_SUB
    nt = s // rs
    ri = np.arange(r)
    same = (ri[:, None] // c) == (ri[None, :] // c)
    tril = (same & (ri[None, :] <= ri[:, None])).astype(np.float32)
    col = lambda c0: (lambda b, t: (b * nt + t, c0 // BRANCH_W))
    const = lambda shape: pl.BlockSpec(shape, lambda b, t: (0,) * len(shape))
    return pl.pallas_call(
        _hgrn_body,
        grid=(bsz, nt),
        in_specs=[
            pl.BlockSpec((rs, BRANCH_W), col(COL_FB)),
            pl.BlockSpec((rs, BRANCH_W), col(COL_IB)),
            pl.BlockSpec((rs, BRANCH_W), col(COL_QB)),
            pl.BlockSpec((rs, BRANCH_W), col(COL_GB)),
            const((lb_logits.shape[0], BRANCH_W)),
            const((1, HGRN_D)),
            const((r, r)),
        ],
        out_specs=pl.BlockSpec((rs, BRANCH_W), lambda b, t: (b * nt + t, 0)),
        out_shape=jax.ShapeDtypeStruct((bsz * s, BRANCH_W), BF16),
        scratch_shapes=[pltpu.VMEM((HGRN_HEADS, HGRN_D, HGRN_D), F32)],
        compiler_params=pltpu.CompilerParams(
            dimension_semantics=("parallel", "arbitrary"), vmem_limit_bytes=VMEM_LIMIT),
        name="hgrn",
    )(proj, proj, proj, proj, lb_logits, norm_w.reshape(1, HGRN_D), tril.astype(BF16))


def _xattn_body(q_ref, z_ref, mem_ref, mw_ref, wkv_ref, o_ref, km_ref, vm_ref):
    @pl.when(pl.program_id(1) == 0)
    def _():
        m = mem_ref[0]
        mn = m * lax.rsqrt(jnp.mean(m * m, axis=-1, keepdims=True) + EPS) * mw_ref[...]
        kv = jnp.dot(mn.astype(BF16), wkv_ref[...], preferred_element_type=F32)
        km_ref[...] = kv[:, :BRANCH_W].astype(BF16)
        vm_ref[...] = kv[:, BRANCH_W:].astype(BF16)

    for h in range(XA_HEADS):
        cols = slice(h * XA_HD, (h + 1) * XA_HD)
        logits = lax.dot_general(q_ref[:, cols], km_ref[:, cols], (((1,), (1,)), ((), ())),
                                 preferred_element_type=F32) * (XA_HD ** -0.5 * LOG2E)
        mx = jnp.max(logits, axis=-1, keepdims=True)
        p = jnp.exp2(logits - mx)
        den = jnp.sum(p, axis=-1, keepdims=True)
        o = jnp.dot(p.astype(BF16), vm_ref[:, cols], preferred_element_type=F32) / den
        o_ref[:, cols] = (o * _silu(z_ref[:, cols].astype(F32))).astype(o_ref.dtype)


def _xattn(proj, mem, mem_norm_w, wkv_bf16, bsz, s, tq=1024):
    n_mem, d = mem.shape[1], mem.shape[2]
    nq = s // tq
    return pl.pallas_call(
        _xattn_body,
        grid=(bsz, nq),
        in_specs=[
            pl.BlockSpec((tq, BRANCH_W), lambda b, i: (b * nq + i, COL_QC // BRANCH_W)),
            pl.BlockSpec((tq, BRANCH_W), lambda b, i: (b * nq + i, COL_ZC // BRANCH_W)),
            pl.BlockSpec((1, n_mem, d), lambda b, i: (b, 0, 0)),
            pl.BlockSpec((1, d), lambda b, i: (0, 0)),
            pl.BlockSpec((d, 2 * BRANCH_W), lambda b, i: (0, 0)),
        ],
        out_specs=pl.BlockSpec((tq, BRANCH_W), lambda b, i: (b * nq + i, 0)),
        out_shape=jax.ShapeDtypeStruct((bsz * s, BRANCH_W), BF16),
        scratch_shapes=[pltpu.VMEM((n_mem, BRANCH_W), BF16), pltpu.VMEM((n_mem, BRANCH_W), BF16)],
        compiler_params=pltpu.CompilerParams(
            dimension_semantics=("parallel", "arbitrary"), vmem_limit_bytes=VMEM_LIMIT),
        name="xattn",
    )(proj, proj, mem, mem_norm_w.reshape(1, d), wkv_bf16)


def _merge_body(x_ref, ga_ref, gb_ref, gc_ref, ya_ref, yb_ref, yc_ref,
                wa_ref, wb_ref, wc_ref, wo_ref, pw_ref, o_ref):
    tm = x_ref.shape[0]
    halves = [slice(0, tm // 2), slice(tm // 2, tm)]
    branches = ((ga_ref, ya_ref, wa_ref), (gb_ref, yb_ref, wb_ref), (gc_ref, yc_ref, wc_ref))
    proj = [[jnp.dot(y_ref[rows, :], w_ref[...], preferred_element_type=F32) for _, y_ref, w_ref in branches]
            for rows in halves]
    gated = [[_sigmoid(g_ref[rows, :].astype(F32)) * proj[k][n] for n, (g_ref, _, _) in enumerate(branches)]
             for k, rows in enumerate(halves)]
    merged = [gated[k][0] + gated[k][1] + gated[k][2] for k in range(2)]
    y = [jnp.dot(merged[k].astype(BF16), wo_ref[...], preferred_element_type=F32) for k in range(2)]
    for k, rows in enumerate(halves):
        yn = y[k] * lax.rsqrt(jnp.mean(y[k] * y[k], axis=-1, keepdims=True) + EPS) * pw_ref[...]
        o_ref[rows, :] = x_ref[rows, :] + yn


def _merge(x2, proj, ya, yb, yc, wa, wb, wc, wo, post_w, tm=1024):
    t, d = x2.shape
    row = lambda c: (lambda i: (i, c))
    const = lambda shape: pl.BlockSpec(shape, lambda i: (0, 0))
    return pl.pallas_call(
        _merge_body,
        grid=(t // tm,),
        in_specs=[
            pl.BlockSpec((tm, d), row(0)),
            pl.BlockSpec((tm, d), row(COL_GATE_A // d)),
            pl.BlockSpec((tm, d), row(COL_GATE_B // d)),
            pl.BlockSpec((tm, d), row(COL_GATE_C // d)),
            pl.BlockSpec((tm, BRANCH_W), row(0)),
            pl.BlockSpec((tm, BRANCH_W), row(0)),
            pl.BlockSpec((tm, BRANCH_W), row(0)),
            const((BRANCH_W, d)), const((BRANCH_W, d)), const((BRANCH_W, d)),
            const((d, d)), const((1, d)),
        ],
        out_specs=pl.BlockSpec((tm, d), row(0)),
        out_shape=jax.ShapeDtypeStruct((t, d), F32),
        compiler_params=pltpu.CompilerParams(
            dimension_semantics=("parallel",), vmem_limit_bytes=VMEM_LIMIT),
        name="merge",
    )(x2, proj, proj, proj, ya, yb, yc, wa, wb, wc, wo, post_w.reshape(1, d))


def kernel(x, mem, pre_norm_w, w_in, hgrn_lb_logits, hgrn_norm_w, mem_norm_w, w_mem_kv,
           w_branch_a, w_branch_b, w_branch_c, w_out, post_norm_w):
    bsz, s, d = x.shape
    assert w_in.shape[0] == 1 and w_in.shape[2] == PROJ_TOTAL and d == 1024
    assert s % MOBA_BLOCK == 0 and s % (HGRN_TILE * HGRN_SUB) == 0
    x2 = x.reshape(bsz * s, d)
    proj = _in_proj(x2, pre_norm_w[0], w_in[0].astype(BF16))
    slopes = jnp.exp2(-8.0 * jnp.arange(1, MOBA_HEADS + 1, dtype=F32) / MOBA_HEADS)
    ya = _moba(proj, slopes, bsz, s)
    yb = _hgrn(proj, hgrn_lb_logits, hgrn_norm_w[0], bsz, s)
    yc = _xattn(proj, mem, mem_norm_w[0], w_mem_kv[0].astype(BF16), bsz, s)
    out = _merge(x2, proj, ya, yb, yc, w_branch_a[0].astype(BF16), w_branch_b[0].astype(BF16),
                 w_branch_c[0].astype(BF16), w_out[0].astype(BF16), post_norm_w[0])
    return out.reshape(bsz, s, d)
```

```python
import functools

import jax
import jax.numpy as jnp
import numpy as np
from jax import lax
from jax.experimental import pallas as pl
from jax.experimental.pallas import tpu as pltpu

F32 = jnp.float32
BF16 = jnp.bfloat16
EPS = 1e-6

MOBA_HEADS, MOBA_HD, MOBA_BLOCK, MOBA_TOPK = 8, 64, 256, 3
HGRN_HEADS, HGRN_D, HGRN_CHUNK = 4, 128, 32
XA_HEADS, XA_HD = 4, 128
BRANCH_W = 512
LANES = 128
COL_QA, COL_KA, COL_VA, COL_ZA = 0, 512, 1024, 1536
COL_FB, COL_IB, COL_QB, COL_GB = 2048, 2560, 3072, 3584
COL_QC, COL_ZC = 4096, 4608
COL_GATE_A, COL_GATE_B, COL_GATE_C = 5120, 6144, 7168
PROJ_TOTAL = COL_GATE_C + 1024

NEG_BIG = -1e30
LOG2E = 1.4426950408889634
MOBA_VROWS = MOBA_HD + 16
MOBA_NBL = 16
MOBA_GATE_ROWS = 3 * MOBA_NBL
VMEM_LIMIT = 56 * 1024 * 1024


def _sigmoid(z):
    return 1.0 / (1.0 + jnp.exp2(z * (-LOG2E)))


def _silu(z):
    return z * _sigmoid(z)


def _split3(a):
    hi = a.astype(BF16).astype(F32)
    r = a - hi
    mid = r.astype(BF16).astype(F32)
    lo = (r - mid).astype(BF16).astype(F32)
    return hi, mid, lo


IN_PROJ_CHUNK = 1024
HGRN_TILE = 256


def _in_proj_body(x_ref, nw_ref, w_ref, lbl_ref, hnw_ref, lmat_ref, o_ref, yb_ref, st_ref, *, steps_per_batch):
    @pl.when(pl.program_id(0) % steps_per_batch == 0)
    def _():
        st_ref[...] = jnp.zeros_like(st_ref)

    x = x_ref[...]
    ms = jnp.mean(x * x, axis=-1, keepdims=True)
    hx = (x * lax.rsqrt(ms + EPS) * nw_ref[...]).astype(BF16)

    def chunk(c0):
        out = jnp.dot(hx, w_ref[:, c0:c0 + IN_PROJ_CHUNK], preferred_element_type=F32).astype(o_ref.dtype)
        o_ref[:, c0:c0 + IN_PROJ_CHUNK] = out
        return out

    fi = chunk(COL_FB)
    qg = chunk(COL_QB)
    f_all, i_all = fi[:, :BRANCH_W], fi[:, BRANCH_W:]
    q_all, g_all = qg[:, :BRANCH_W], qg[:, BRANCH_W:]

    r, c, d = HGRN_TILE, HGRN_CHUNK, HGRN_D
    n_chunks = r // c
    heads = range(HGRN_HEADS)
    tiles = range(x_ref.shape[0] // r)
    chunks = range(n_chunks)
    cols = [slice(h * d, (h + 1) * d) for h in heads]
    rows = [slice(t * r, (t + 1) * r) for t in tiles]
    nt_dims = (((1,), (1,)), ((), ()))

    logits = lbl_ref[...]
    ex = jnp.exp(logits - jnp.max(logits, axis=0, keepdims=True))
    lb = ex[0:1, :] / jnp.sum(ex, axis=0, keepdims=True)

    kk, pieces = [], []
    for t in tiles:
        fl = f_all[rows[t], :].astype(F32)
        log_f = jnp.log(lb + (1.0 - lb) * _sigmoid(fl))
        kk.append((1.0 - lb) * _sigmoid(-fl))
        pieces.append([p.astype(BF16) for p in _split3(log_f)])
    chunk(COL_QA)
    b = []
    for t in tiles:
        lmat = lmat_ref[...]
        hi, mid, lo = [jnp.dot(lmat, p, preferred_element_type=F32) for p in pieces[t]]
        b.append(hi + mid + lo)
    chunk(COL_VA)

    q_t, k_t32, k_t, dec = [], [], [], []
    for t in tiles:
        q_t.append((q_all[rows[t], :].astype(F32) * jnp.exp(b[t])).astype(BF16))
        kt = kk[t] * jnp.exp(-b[t])
        k_t32.append(kt)
        k_t.append(kt.astype(BF16))
        dec.append([jnp.exp(b[t][ci * c + c - 1:ci * c + c, :]) for ci in chunks])

    a = [[lax.dot_general(q_t[t][:, cols[h]], k_t[t][:, cols[h]], nt_dims, preferred_element_type=F32)
          for h in heads] for t in tiles]
    u_t = []
    for t in tiles:
        per_head = []
        for h in heads:
            vt = i_all[rows[t], cols[h]].astype(F32).T.astype(BF16)
            blocks = []
            for ci in chunks:
                slab = (k_t32[t][ci * c:(ci + 1) * c, cols[h]] * dec[t][ci][:, cols[h]]).astype(BF16)
                parts = ([jnp.zeros((c, ci * d), BF16)] if ci else []) + [slab]
                if ci < n_chunks - 1:
                    parts.append(jnp.zeros((c, (n_chunks - 1 - ci) * d), BF16))
                blocks.append(jnp.concatenate(parts, axis=1))
            kd_blk = jnp.concatenate(blocks, axis=0)
            per_head.append(jnp.dot(vt, kd_blk, preferred_element_type=F32))
        u_t.append(per_head)
    chunk(COL_QC)
    o_intra = [[jnp.dot(a[t][h].astype(BF16) * lmat_ref[...], i_all[rows[t], cols[h]],
                        preferred_element_type=F32) for h in heads] for t in tiles]
    chunk(COL_GATE_A)

    states = [[None] * HGRN_HEADS for _ in tiles]
    for h in heads:
        st = st_ref[h]
        for t in tiles:
            before = []
            for ci in chunks:
                before.append(st.astype(BF16))
                st = st * dec[t][ci][:, cols[h]] + u_t[t][h][:, ci * d:(ci + 1) * d]
            states[t][h] = before
        st_ref[h] = st
    chunk(COL_GATE_B)

    o_inter = [[[lax.dot_general(q_t[t][ci * c:(ci + 1) * c, cols[h]], states[t][h][ci], nt_dims,
                                 preferred_element_type=F32) for ci in chunks] for h in heads] for t in tiles]
    chunk(COL_GATE_C)
    for t in tiles:
        for h in heads:
            o = o_intra[t][h] + jnp.concatenate(o_inter[t][h], axis=0)
            on = o * lax.rsqrt(jnp.mean(o * o, axis=-1, keepdims=True) + EPS) * hnw_ref[...]
            yb_ref[rows[t], cols[h]] = (on * _silu(g_all[rows[t], cols[h]].astype(F32))).astype(yb_ref.dtype)


def _in_proj(x2, norm_w, w_bf16, lb_logits, hgrn_norm_w, s, tm=512):
    t, d = x2.shape
    n = w_bf16.shape[1]
    assert n == 8 * IN_PROJ_CHUNK and s % tm == 0 and tm % HGRN_TILE == 0
    r, c = HGRN_TILE, HGRN_CHUNK
    ri = np.arange(r)
    same = (ri[:, None] // c) == (ri[None, :] // c)
    tril = (same & (ri[None, :] <= ri[:, None])).astype(np.float32)
    const = lambda shape: pl.BlockSpec(shape, lambda i: (0,) * len(shape))
    return pl.pallas_call(
        functools.partial(_in_proj_body, steps_per_batch=s // tm),
        grid=(t // tm,),
        in_specs=[
            pl.BlockSpec((tm, d), lambda i: (i, 0)),
            const((1, d)),
            pl.BlockSpec((d, n), lambda i: (0, 0), pipeline_mode=pl.Buffered(1)),
            const((lb_logits.shape[0], BRANCH_W)),
            const((1, HGRN_D)),
            const((r, r)),
        ],
        out_specs=[pl.BlockSpec((tm, n), lambda i: (i, 0)), pl.BlockSpec((tm, BRANCH_W), lambda i: (i, 0))],
        out_shape=[jax.ShapeDtypeStruct((t, n), BF16), jax.ShapeDtypeStruct((t, BRANCH_W), BF16)],
        scratch_shapes=[pltpu.VMEM((HGRN_HEADS, HGRN_D, HGRN_D), F32)],
        compiler_params=pltpu.CompilerParams(
            dimension_semantics=("arbitrary",), vmem_limit_bytes=VMEM_LIMIT),
        name="in_proj",
    )(x2, norm_w.reshape(1, d), w_bf16, lb_logits, hgrn_norm_w.reshape(1, HGRN_D), tril.astype(BF16))


def _moba_constants(s):
    blk, hd, nbl = MOBA_BLOCK, MOBA_HD, MOBA_NBL
    row = np.arange(s)
    off = (row % blk).astype(np.float32)
    onehot = (row[:, None] // blk == np.arange(nbl)[None, :]).astype(np.float32)
    bias_lanes = np.concatenate([onehot, onehot, onehot, off[:, None], off[:, None], off[:, None],
                                 np.zeros((s, hd - 3 * nbl - 3), np.float32)], axis=1)
    zeros = np.zeros((s, hd), np.float32)
    a0 = np.concatenate([zeros, bias_lanes], axis=1)
    a1 = np.concatenate([bias_lanes, zeros], axis=1)
    aug = np.stack([a0, a1]).astype(BF16)
    avg = ((np.arange(nbl)[:, None] == (row // blk)[None, :]).astype(np.float32) / blk).astype(BF16)
    causal = np.where(np.arange(blk)[:, None] <= np.arange(blk)[None, :], 0.0, -np.inf).astype(np.float32)
    mask = np.concatenate([np.zeros((blk, blk), np.float32), causal])
    return aug, avg, mask


def _moba_body(slopes_ref, q_ref, qn_ref, k_ref, v_ref, z_ref, aug_ref, avg_ref, mask_ref, o_ref,
               ka_ref, vt_ref, kbp_ref, qa_ref, qan_ref, acc_ref, p_ref, m_ref, al_ref):
    s = k_ref.shape[0]
    nb = s // MOBA_BLOCK
    blk = MOBA_BLOCK
    nh = MOBA_HEADS
    hd = MOBA_HD
    nbl, gr = MOBA_NBL, MOBA_GATE_ROWS
    i = pl.program_id(1)

    def build_queries(src_ref, iq, dst_ref):
        nidx = lax.broadcasted_iota(jnp.int32, (nbl, blk), 0)
        nidx_f = nidx.astype(F32)
        qoff = lax.broadcasted_iota(jnp.int32, (nbl, blk), 1)
        dist0 = (qoff + (iq - nidx) * blk).astype(F32)
        qts = [(src_ref[:, hp * LANES:(hp + 1) * LANES].astype(F32) * (hd ** -0.5)).T
               for hp in range(nh // 2)]
        gates = [jnp.dot(kbp_ref[hp * 2 * gr:(hp + 1) * 2 * gr, :], qts[hp].astype(BF16),
                         preferred_element_type=F32) for hp in range(nh // 2)]
        for hp in range(nh // 2):
            qs = qts[hp] * LOG2E
            for e in range(2):
                h = 2 * hp + e
                gg = gates[hp][e * gr:(e + 1) * gr]
                g = gg[0:nbl] + gg[nbl:2 * nbl] + gg[2 * nbl:3 * nbl]
                gv = jnp.where(nidx < iq, g, -jnp.inf)
                sel = nidx == iq
                for _ in range(MOBA_TOPK):
                    mx = jnp.max(gv, axis=0, keepdims=True)
                    first = jnp.min(jnp.where(gv == mx, nidx_f, float(nbl)), axis=0, keepdims=True)
                    pick = nidx_f == jnp.where(mx > -jnp.inf, first, float(nbl))
                    sel = sel | pick
                    gv = jnp.where(pick, -jnp.inf, gv)
                sl = slopes_ref[h] * LOG2E
                bias = jnp.where(sel, -sl * dist0, NEG_BIG)
                b_hi, b_mid, b_lo = _split3(bias)
                s_hi, s_mid, s_lo = _split3(jnp.full((nbl, blk), sl, F32))
                srow = jnp.where(nidx == 0, s_hi, jnp.where(nidx == 1, s_mid, jnp.where(nidx == 2, s_lo, 0.0)))
                qh = qs[0:hd] if e == 0 else qs[hd:2 * hd]
                parts = [qh, b_hi, b_mid, b_lo, srow] if e == 0 else [b_hi, b_mid, b_lo, srow, qh]
                dst_ref[h] = jnp.concatenate(parts, axis=0).astype(BF16)

    @pl.when(i == 0)
    def _():
        lane = lax.broadcasted_iota(jnp.int32, (1, LANES), 1)
        first_half = lane < hd
        kbar = jnp.dot(avg_ref[...], k_ref[...], preferred_element_type=F32)
        for h in range(nh):
            hp, e = h // 2, h % 2
            cols = slice(hp * LANES, (hp + 1) * LANES)
            keep = first_half if e == 0 else jnp.logical_not(first_half)
            for j in range(nb):
                rows = slice(j * blk, (j + 1) * blk)
                ka_ref[h, j] = jnp.where(keep, k_ref[rows, cols], aug_ref[e, rows, :])
            pieces = _split3(jnp.where(keep, kbar[:, cols], 0.0))
            for p in range(3):
                kbp_ref[h * gr + p * nbl:h * gr + (p + 1) * nbl, :] = pieces[p].astype(BF16)
        ones_row = jnp.where(lax.broadcasted_iota(jnp.int32, (MOBA_VROWS - hd, blk), 0) == 0, 1.0, 0.0)
        for j in range(nb):
            vt = v_ref[j * blk:(j + 1) * blk, :].astype(F32).T
            for h in range(nh):
                vt_ref[j, h, 0:hd, :] = vt[h * hd:(h + 1) * hd].astype(BF16)
                vt_ref[j, h, hd:MOBA_VROWS, :] = ones_row.astype(BF16)
        build_queries(q_ref, 0, qa_ref)
        p_ref[...] = jnp.zeros(p_ref.shape, BF16)
        al_ref[...] = jnp.ones(al_ref.shape, F32)

    @pl.when(i > 0)
    def _():
        qa_ref[...] = qan_ref[...]

    m_ref[...] = jnp.full(m_ref.shape, -jnp.inf, F32)
    acc_ref[...] = jnp.zeros(acc_ref.shape, F32)

    def softmax(h, sT):
        m_old = m_ref[h]
        m_new = jnp.maximum(m_old, jnp.max(sT, axis=0, keepdims=True))
        m_ref[h] = m_new
        return jnp.exp2(sT - m_new).astype(BF16), jnp.exp2(m_old - m_new)

    def pair_scores(j0):
        return [jnp.dot(ka_ref[h, pl.ds(j0, 2)].reshape(2 * blk, LANES), qa_ref[h],
                        preferred_element_type=F32) for h in range(nh)]

    def pair_values(ja, jb, h):
        return jnp.concatenate([vt_ref[ja, h], vt_ref[jb, h]], axis=1)

    def pending_pv(j0):
        pa, pb = jnp.maximum(j0 - 2, 0), jnp.maximum(j0 - 1, 0)
        return [jnp.dot(pair_values(pa, pb, h), p_ref[h], preferred_element_type=F32) for h in range(nh)]

    def past(k, carry):
        scores = pair_scores(2 * k)
        pv = pending_pv(2 * k)
        sm = [softmax(h, scores[h]) for h in range(nh)]
        for h in range(nh):
            acc_ref[h] = al_ref[h] * acc_ref[h] + pv[h]
        for h in range(nh):
            p_ref[h], al_ref[h] = sm[h]
        return carry

    npairs = i // 2
    lax.fori_loop(0, npairs, past, 0)

    def finish(scores, last_values):
        pv = pending_pv(2 * npairs)
        build_queries(qn_ref, i + 1, qan_ref)
        sm = [softmax(h, scores[h]) for h in range(nh)]
        last = [jnp.dot(last_values(h), sm[h][0], preferred_element_type=F32) for h in range(nh)]
        outs = []
        for h in range(nh):
            acc = sm[h][1] * (al_ref[h] * acc_ref[h] + pv[h]) + last[h]
            outs.append(acc[0:hd] / acc[hd:hd + 1])
        ot = jnp.concatenate(outs, axis=0)
        o_ref[...] = (ot.T * _silu(z_ref[...].astype(F32))).astype(o_ref.dtype)

    @pl.when(i % 2 == 0)
    def _():
        finish([jnp.dot(ka_ref[h, i], qa_ref[h], preferred_element_type=F32) + mask_ref[blk:2 * blk, :]
                for h in range(nh)], lambda h: vt_ref[i, h])

    @pl.when(i % 2 == 1)
    def _():
        scores = pair_scores(i - 1)
        finish([scores[h] + mask_ref[...] for h in range(nh)], lambda h: pair_values(i - 1, i, h))


def _moba(proj, slopes, bsz, s):
    nb = s // MOBA_BLOCK
    assert nb <= MOBA_NBL and 2 * MOBA_HD == LANES and 3 * MOBA_NBL + 3 <= MOBA_HD
    blk = MOBA_BLOCK
    aug, avg, mask = _moba_constants(s)
    qblock = lambda c0: (lambda b, i, *_: (b * nb + i, c0 // BRANCH_W))
    qnext = lambda b, i, *_: (b * nb + jnp.minimum(i + 1, nb - 1), COL_QA // BRANCH_W)
    whole = lambda c0: (lambda b, i, *_: (b, c0 // BRANCH_W))
    return pl.pallas_call(
        _moba_body,
        grid_spec=pltpu.PrefetchScalarGridSpec(
            num_scalar_prefetch=1,
            grid=(bsz, nb),
            in_specs=[
                pl.BlockSpec((blk, BRANCH_W), qblock(COL_QA)),
                pl.BlockSpec((blk, BRANCH_W), qnext),
                pl.BlockSpec((s, BRANCH_W), whole(COL_KA)),
                pl.BlockSpec((s, BRANCH_W), whole(COL_VA)),
                pl.BlockSpec((blk, BRANCH_W), qblock(COL_ZA)),
                pl.BlockSpec((2, s, LANES), lambda b, i, *_: (0, 0, 0)),
                pl.BlockSpec((MOBA_NBL, s), lambda b, i, *_: (0, 0)),
                pl.BlockSpec((2 * blk, blk), lambda b, i, *_: (0, 0)),
            ],
            out_specs=pl.BlockSpec((blk, BRANCH_W), lambda b, i, *_: (b * nb + i, 0)),
            scratch_shapes=[
                pltpu.VMEM((MOBA_HEADS, nb, blk, LANES), BF16),
                pltpu.VMEM((nb, MOBA_HEADS, MOBA_VROWS, blk), BF16),
                pltpu.VMEM((MOBA_HEADS * MOBA_GATE_ROWS, LANES), BF16),
                pltpu.VMEM((MOBA_HEADS, LANES, blk), BF16),
                pltpu.VMEM((MOBA_HEADS, LANES, blk), BF16),
                pltpu.VMEM((MOBA_HEADS, MOBA_VROWS, blk), F32),
                pltpu.VMEM((MOBA_HEADS, 2 * blk, blk), BF16),
                pltpu.VMEM((MOBA_HEADS, 1, blk), F32),
                pltpu.VMEM((MOBA_HEADS, 1, blk), F32),
            ],
        ),
        out_shape=jax.ShapeDtypeStruct((bsz * s, BRANCH_W), BF16),
        compiler_params=pltpu.CompilerParams(
            dimension_semantics=("parallel", "arbitrary"), vmem_limit_bytes=VMEM_LIMIT),
        name="moba",
    )(slopes, proj, proj, proj, proj, proj, aug, avg, mask)


def _xattn_body(q_ref, z_ref, mem_ref, mw_ref, wkv_ref, o_ref, km_ref, vm_ref):
    @pl.when(pl.program_id(1) == 0)
    def _():
        m = mem_ref[0]
        mn = m * lax.rsqrt(jnp.mean(m * m, axis=-1, keepdims=True) + EPS) * mw_ref[...]
        kv = jnp.dot(mn.astype(BF16), wkv_ref[...], preferred_element_type=F32)
        km_ref[...] = kv[:, :BRANCH_W].astype(BF16)
        vm_ref[...] = kv[:, BRANCH_W:].astype(BF16)

    for h in range(XA_HEADS):
        cols = slice(h * XA_HD, (h + 1) * XA_HD)
        logits = lax.dot_general(q_ref[:, cols], km_ref[:, cols], (((1,), (1,)), ((), ())),
                                 preferred_element_type=F32) * (XA_HD ** -0.5 * LOG2E)
        mx = jnp.max(logits, axis=-1, keepdims=True)
        p = jnp.exp2(logits - mx)
        den = jnp.sum(p, axis=-1, keepdims=True)
        o = jnp.dot(p.astype(BF16), vm_ref[:, cols], preferred_element_type=F32) / den
        o_ref[:, cols] = (o * _silu(z_ref[:, cols].astype(F32))).astype(o_ref.dtype)


def _xattn(proj, mem, mem_norm_w, wkv_bf16, bsz, s, tq=1024):
    n_mem, d = mem.shape[1], mem.shape[2]
    nq = s // tq
    return pl.pallas_call(
        _xattn_body,
        grid=(bsz, nq),
        in_specs=[
            pl.BlockSpec((tq, BRANCH_W), lambda b, i: (b * nq + i, COL_QC // BRANCH_W)),
            pl.BlockSpec((tq, BRANCH_W), lambda b, i: (b * nq + i, COL_ZC // BRANCH_W)),
            pl.BlockSpec((1, n_mem, d), lambda b, i: (b, 0, 0)),
            pl.BlockSpec((1, d), lambda b, i: (0, 0)),
            pl.BlockSpec((d, 2 * BRANCH_W), lambda b, i: (0, 0)),
        ],
        out_specs=pl.BlockSpec((tq, BRANCH_W), lambda b, i: (b * nq + i, 0)),
        out_shape=jax.ShapeDtypeStruct((bsz * s, BRANCH_W), BF16),
        scratch_shapes=[pltpu.VMEM((n_mem, BRANCH_W), BF16), pltpu.VMEM((n_mem, BRANCH_W), BF16)],
        compiler_params=pltpu.CompilerParams(
            dimension_semantics=("parallel", "arbitrary"), vmem_limit_bytes=VMEM_LIMIT),
        name="xattn",
    )(proj, proj, mem, mem_norm_w.reshape(1, d), wkv_bf16)


def _merge_body(x_ref, ga_ref, gb_ref, gc_ref, ya_ref, yb_ref, yc_ref,
                wa_ref, wb_ref, wc_ref, wo_ref, pw_ref, o_ref):
    tm = x_ref.shape[0]
    halves = [slice(0, tm // 2), slice(tm // 2, tm)]
    branches = ((ga_ref, ya_ref, wa_ref), (gb_ref, yb_ref, wb_ref), (gc_ref, yc_ref, wc_ref))
    proj = [[jnp.dot(y_ref[rows, :], w_ref[...], preferred_element_type=F32) for _, y_ref, w_ref in branches]
            for rows in halves]
    gated = [[_sigmoid(g_ref[rows, :].astype(F32)) * proj[k][n] for n, (g_ref, _, _) in enumerate(branches)]
             for k, rows in enumerate(halves)]
    merged = [gated[k][0] + gated[k][1] + gated[k][2] for k in range(2)]
    y = [jnp.dot(merged[k].astype(BF16), wo_ref[...], preferred_element_type=F32) for k in range(2)]
    for k, rows in enumerate(halves):
        yn = y[k] * lax.rsqrt(jnp.mean(y[k] * y[k], axis=-1, keepdims=True) + EPS) * pw_ref[...]
        o_ref[rows, :] = x_ref[rows, :] + yn


def _merge(x2, proj, ya, yb, yc, wa, wb, wc, wo, post_w, tm=1024):
    t, d = x2.shape
    row = lambda c: (lambda i: (i, c))
    const = lambda shape: pl.BlockSpec(shape, lambda i: (0, 0))
    return pl.pallas_call(
        _merge_body,
        grid=(t // tm,),
        in_specs=[
            pl.BlockSpec((tm, d), row(0)),
            pl.BlockSpec((tm, d), row(COL_GATE_A // d)),
            pl.BlockSpec((tm, d), row(COL_GATE_B // d)),
            pl.BlockSpec((tm, d), row(COL_GATE_C // d)),
            pl.BlockSpec((tm, BRANCH_W), row(0)),
            pl.BlockSpec((tm, BRANCH_W), row(0)),
            pl.BlockSpec((tm, BRANCH_W), row(0)),
            const((BRANCH_W, d)), const((BRANCH_W, d)), const((BRANCH_W, d)),
            const((d, d)), const((1, d)),
        ],
        out_specs=pl.BlockSpec((tm, d), row(0)),
        out_shape=jax.ShapeDtypeStruct((t, d), F32),
        compiler_params=pltpu.CompilerParams(
            dimension_semantics=("parallel",), vmem_limit_bytes=VMEM_LIMIT),
        name="merge",
    )(x2, proj, proj, proj, ya, yb, yc, wa, wb, wc, wo, post_w.reshape(1, d))


def kernel(x, mem, pre_norm_w, w_in, hgrn_lb_logits, hgrn_norm_w, mem_norm_w, w_mem_kv,
           w_branch_a, w_branch_b, w_branch_c, w_out, post_norm_w):
    bsz, s, d = x.shape
    assert w_in.shape[0] == 1 and w_in.shape[2] == PROJ_TOTAL and d == 1024
    assert s % MOBA_BLOCK == 0
    x2 = x.reshape(bsz * s, d)
    proj, yb = _in_proj(x2, pre_norm_w[0], w_in[0].astype(BF16), hgrn_lb_logits, hgrn_norm_w[0], s)
    slopes = jnp.exp2(-8.0 * jnp.arange(1, MOBA_HEADS + 1, dtype=F32) / MOBA_HEADS)
    ya = _moba(proj, slopes, bsz, s)
    yc = _xattn(proj, mem, mem_norm_w[0], w_mem_kv[0].astype(BF16), bsz, s)
    out = _merge(x2, proj, ya, yb, yc, w_branch_a[0].astype(BF16), w_branch_b[0].astype(BF16),
                 w_branch_c[0].astype(BF16), w_out[0].astype(BF16), post_norm_w[0])
    return out.reshape(bsz, s, d)
```

```python
import functools

import jax
import jax.numpy as jnp
import numpy as np
from jax import lax
from jax.experimental import pallas as pl
from jax.experimental.pallas import tpu as pltpu

F32 = jnp.float32
BF16 = jnp.bfloat16
EPS = 1e-6

MOBA_HEADS, MOBA_HD, MOBA_BLOCK, MOBA_TOPK = 8, 64, 256, 3
HGRN_HEADS, HGRN_D, HGRN_CHUNK = 4, 128, 32
XA_HEADS, XA_HD = 4, 128
BRANCH_W = 512
LANES = 128
COL_QA, COL_KA, COL_VA, COL_ZA = 0, 512, 1024, 1536
COL_FB, COL_IB, COL_QB, COL_GB = 2048, 2560, 3072, 3584
COL_QC, COL_ZC = 4096, 4608
COL_GATE_A, COL_GATE_B, COL_GATE_C = 5120, 6144, 7168
PROJ_TOTAL = COL_GATE_C + 1024

NEG_BIG = -1e30
LOG2E = 1.4426950408889634
MOBA_VROWS = MOBA_HD + 16
MOBA_NBL = 16
MOBA_GATE_ROWS = 3 * MOBA_NBL
VMEM_LIMIT = 56 * 1024 * 1024


def _sigmoid(z):
    return 1.0 / (1.0 + jnp.exp2(z * (-LOG2E)))


def _silu(z):
    return z * _sigmoid(z)


def _split3(a):
    hi = a.astype(BF16).astype(F32)
    r = a - hi
    mid = r.astype(BF16).astype(F32)
    lo = (r - mid).astype(BF16).astype(F32)
    return hi, mid, lo


IN_PROJ_CHUNK = 1024
HGRN_TILE = 256


def _in_proj_body(x_ref, nw_ref, w_ref, lbl_ref, hnw_ref, lmat_ref, mem_ref, mw_ref, wkv_ref,
                  o_ref, yb_ref, yc_ref, st_ref, km_ref, vm_ref, *, steps_per_batch):
    @pl.when(pl.program_id(0) % steps_per_batch == 0)
    def _():
        st_ref[...] = jnp.zeros_like(st_ref)
        m = mem_ref[0]
        mn = m * lax.rsqrt(jnp.mean(m * m, axis=-1, keepdims=True) + EPS) * mw_ref[...]
        kv = jnp.dot(mn.astype(BF16), wkv_ref[...], preferred_element_type=F32)
        km_ref[...] = kv[:, :BRANCH_W].astype(BF16)
        vm_ref[...] = kv[:, BRANCH_W:].astype(BF16)

    x = x_ref[...]
    ms = jnp.mean(x * x, axis=-1, keepdims=True)
    hx = (x * lax.rsqrt(ms + EPS) * nw_ref[...]).astype(BF16)

    def chunk(c0):
        out = jnp.dot(hx, w_ref[:, c0:c0 + IN_PROJ_CHUNK], preferred_element_type=F32).astype(o_ref.dtype)
        o_ref[:, c0:c0 + IN_PROJ_CHUNK] = out
        return out

    fi = chunk(COL_FB)
    qg = chunk(COL_QB)
    f_all, i_all = fi[:, :BRANCH_W], fi[:, BRANCH_W:]
    q_all, g_all = qg[:, :BRANCH_W], qg[:, BRANCH_W:]

    r, c, d = HGRN_TILE, HGRN_CHUNK, HGRN_D
    n_chunks = r // c
    heads = range(HGRN_HEADS)
    tiles = range(x_ref.shape[0] // r)
    chunks = range(n_chunks)
    cols = [slice(h * d, (h + 1) * d) for h in heads]
    rows = [slice(t * r, (t + 1) * r) for t in tiles]
    nt_dims = (((1,), (1,)), ((), ()))

    logits = lbl_ref[...]
    ex = jnp.exp(logits - jnp.max(logits, axis=0, keepdims=True))
    lb = ex[0:1, :] / jnp.sum(ex, axis=0, keepdims=True)

    kk, pieces = [], []
    for t in tiles:
        fl = f_all[rows[t], :].astype(F32)
        log_f = jnp.log(lb + (1.0 - lb) * _sigmoid(fl))
        kk.append((1.0 - lb) * _sigmoid(-fl))
        pieces.append([p.astype(BF16) for p in _split3(log_f)])
    qz = chunk(COL_QC)
    xcols = [slice(h * XA_HD, (h + 1) * XA_HD) for h in range(XA_HEADS)]

    def xa_logits(hs):
        return [lax.dot_general(qz[:, xcols[h]], km_ref[:, xcols[h]], nt_dims,
                                preferred_element_type=F32) * (XA_HD ** -0.5 * LOG2E) for h in hs]

    def xa_softmax(logits):
        ps, dens = [], []
        for lg in logits:
            p = jnp.exp2(lg - jnp.max(lg, axis=-1, keepdims=True))
            dens.append(jnp.sum(p, axis=-1, keepdims=True))
            ps.append(p.astype(BF16))
        return ps, dens

    def xa_values(ps, dens, hs):
        return [jnp.dot(p, vm_ref[:, xcols[h]], preferred_element_type=F32) / den
                for p, den, h in zip(ps, dens, hs)]

    b = []
    for t in tiles:
        lmat = lmat_ref[...]
        hi, mid, lo = [jnp.dot(lmat, p, preferred_element_type=F32) for p in pieces[t]]
        b.append(hi + mid + lo)
    chunk(COL_QA)

    q_t, k_t32, k_t, dec = [], [], [], []
    for t in tiles:
        q_t.append((q_all[rows[t], :].astype(F32) * jnp.exp(b[t])).astype(BF16))
        kt = kk[t] * jnp.exp(-b[t])
        k_t32.append(kt)
        k_t.append(kt.astype(BF16))
        dec.append([jnp.exp(b[t][ci * c + c - 1:ci * c + c, :]) for ci in chunks])

    a = [[lax.dot_general(q_t[t][:, cols[h]], k_t[t][:, cols[h]], nt_dims, preferred_element_type=F32)
          for h in heads] for t in tiles]
    u_t = []
    for t in tiles:
        per_head = []
        for h in heads:
            vt = i_all[rows[t], cols[h]].astype(F32).T.astype(BF16)
            blocks = []
            for ci in chunks:
                slab = (k_t32[t][ci * c:(ci + 1) * c, cols[h]] * dec[t][ci][:, cols[h]]).astype(BF16)
                parts = ([jnp.zeros((c, ci * d), BF16)] if ci else []) + [slab]
                if ci < n_chunks - 1:
                    parts.append(jnp.zeros((c, (n_chunks - 1 - ci) * d), BF16))
                blocks.append(jnp.concatenate(parts, axis=1))
            kd_blk = jnp.concatenate(blocks, axis=0)
            per_head.append(jnp.dot(vt, kd_blk, preferred_element_type=F32))
        u_t.append(per_head)
    xa_lg = xa_logits((0, 1))
    chunk(COL_VA)
    o_intra = [[jnp.dot(a[t][h].astype(BF16) * lmat_ref[...], i_all[rows[t], cols[h]],
                        preferred_element_type=F32) for h in heads] for t in tiles]
    xa_o = xa_values(*xa_softmax(xa_lg), (0, 1))
    xa_lg = xa_logits((2, 3))
    chunk(COL_GATE_A)

    states = [[None] * HGRN_HEADS for _ in tiles]
    for h in heads:
        st = st_ref[h]
        for t in tiles:
            before = []
            for ci in chunks:
                before.append(st.astype(BF16))
                st = st * dec[t][ci][:, cols[h]] + u_t[t][h][:, ci * d:(ci + 1) * d]
            states[t][h] = before
        st_ref[h] = st
    xa_o += xa_values(*xa_softmax(xa_lg), (2, 3))
    chunk(COL_GATE_B)

    o_inter = [[[lax.dot_general(q_t[t][ci * c:(ci + 1) * c, cols[h]], states[t][h][ci], nt_dims,
                                 preferred_element_type=F32) for ci in chunks] for h in heads] for t in tiles]
    for h in range(XA_HEADS):
        zc = qz[:, BRANCH_W + h * XA_HD:BRANCH_W + (h + 1) * XA_HD].astype(F32)
        yc_ref[:, xcols[h]] = (xa_o[h] * _silu(zc)).astype(yc_ref.dtype)
    chunk(COL_GATE_C)
    for t in tiles:
        for h in heads:
            o = o_intra[t][h] + jnp.concatenate(o_inter[t][h], axis=0)
            on = o * lax.rsqrt(jnp.mean(o * o, axis=-1, keepdims=True) + EPS) * hnw_ref[...]
            yb_ref[rows[t], cols[h]] = (on * _silu(g_all[rows[t], cols[h]].astype(F32))).astype(yb_ref.dtype)


def _in_proj(x2, norm_w, w_bf16, lb_logits, hgrn_norm_w, mem, mem_norm_w, wkv_bf16, s, tm=512):
    t, d = x2.shape
    n = w_bf16.shape[1]
    n_mem = mem.shape[1]
    spb = s // tm
    assert n == 8 * IN_PROJ_CHUNK and s % tm == 0 and tm % HGRN_TILE == 0
    r, c = HGRN_TILE, HGRN_CHUNK
    ri = np.arange(r)
    same = (ri[:, None] // c) == (ri[None, :] // c)
    tril = (same & (ri[None, :] <= ri[:, None])).astype(np.float32)
    const = lambda shape: pl.BlockSpec(shape, lambda i: (0,) * len(shape))
    return pl.pallas_call(
        functools.partial(_in_proj_body, steps_per_batch=spb),
        grid=(t // tm,),
        in_specs=[
            pl.BlockSpec((tm, d), lambda i: (i, 0)),
            const((1, d)),
            pl.BlockSpec((d, n), lambda i: (0, 0), pipeline_mode=pl.Buffered(1)),
            const((lb_logits.shape[0], BRANCH_W)),
            const((1, HGRN_D)),
            const((r, r)),
            pl.BlockSpec((1, n_mem, d), lambda i: (i // spb, 0, 0)),
            const((1, d)),
            pl.BlockSpec((d, 2 * BRANCH_W), lambda i: (0, 0), pipeline_mode=pl.Buffered(1)),
        ],
        out_specs=[pl.BlockSpec((tm, n), lambda i: (i, 0)), pl.BlockSpec((tm, BRANCH_W), lambda i: (i, 0)),
                   pl.BlockSpec((tm, BRANCH_W), lambda i: (i, 0))],
        out_shape=[jax.ShapeDtypeStruct((t, n), BF16), jax.ShapeDtypeStruct((t, BRANCH_W), BF16),
                   jax.ShapeDtypeStruct((t, BRANCH_W), BF16)],
        scratch_shapes=[pltpu.VMEM((HGRN_HEADS, HGRN_D, HGRN_D), F32),
                        pltpu.VMEM((n_mem, BRANCH_W), BF16), pltpu.VMEM((n_mem, BRANCH_W), BF16)],
        compiler_params=pltpu.CompilerParams(
            dimension_semantics=("arbitrary",), vmem_limit_bytes=VMEM_LIMIT),
        name="in_proj",
    )(x2, norm_w.reshape(1, d), w_bf16, lb_logits, hgrn_norm_w.reshape(1, HGRN_D), tril.astype(BF16),
      mem, mem_norm_w.reshape(1, d), wkv_bf16)


def _moba_constants(s):
    blk, hd, nbl = MOBA_BLOCK, MOBA_HD, MOBA_NBL
    row = np.arange(s)
    off = (row % blk).astype(np.float32)
    onehot = (row[:, None] // blk == np.arange(nbl)[None, :]).astype(np.float32)
    bias_lanes = np.concatenate([onehot, onehot, onehot, off[:, None], off[:, None], off[:, None],
                                 np.zeros((s, hd - 3 * nbl - 3), np.float32)], axis=1)
    zeros = np.zeros((s, hd), np.float32)
    a0 = np.concatenate([zeros, bias_lanes], axis=1)
    a1 = np.concatenate([bias_lanes, zeros], axis=1)
    aug = np.stack([a0, a1]).astype(BF16)
    avg = ((np.arange(nbl)[:, None] == (row // blk)[None, :]).astype(np.float32) / blk).astype(BF16)
    causal = np.where(np.arange(blk)[:, None] <= np.arange(blk)[None, :], 0.0, -np.inf).astype(np.float32)
    mask = np.concatenate([np.zeros((blk, blk), np.float32), causal])
    return aug, avg, mask


def _moba_body(slopes_ref, q_ref, qn_ref, k_ref, v_ref, z_ref, aug_ref, avg_ref, mask_ref, o_ref,
               ka_ref, vt_ref, kbp_ref, qa_ref, qan_ref, acc_ref, p_ref, m_ref, al_ref):
    s = k_ref.shape[0]
    nb = s // MOBA_BLOCK
    blk = MOBA_BLOCK
    nh = MOBA_HEADS
    hd = MOBA_HD
    nbl, gr = MOBA_NBL, MOBA_GATE_ROWS
    i = pl.program_id(1)

    def build_queries(src_ref, iq, dst_ref):
        nidx = lax.broadcasted_iota(jnp.int32, (nbl, blk), 0)
        nidx_f = nidx.astype(F32)
        qoff = lax.broadcasted_iota(jnp.int32, (nbl, blk), 1)
        dist0 = (qoff + (iq - nidx) * blk).astype(F32)
        qts = [(src_ref[:, hp * LANES:(hp + 1) * LANES].astype(F32) * (hd ** -0.5)).T
               for hp in range(nh // 2)]
        gates = [jnp.dot(kbp_ref[hp * 2 * gr:(hp + 1) * 2 * gr, :], qts[hp].astype(BF16),
                         preferred_element_type=F32) for hp in range(nh // 2)]
        for hp in range(nh // 2):
            qs = qts[hp] * LOG2E
            for e in range(2):
                h = 2 * hp + e
                gg = gates[hp][e * gr:(e + 1) * gr]
                g = gg[0:nbl] + gg[nbl:2 * nbl] + gg[2 * nbl:3 * nbl]
                gv = jnp.where(nidx < iq, g, -jnp.inf)
                sel = nidx == iq
                for _ in range(MOBA_TOPK):
                    mx = jnp.max(gv, axis=0, keepdims=True)
                    first = jnp.min(jnp.where(gv == mx, nidx_f, float(nbl)), axis=0, keepdims=True)
                    pick = nidx_f == jnp.where(mx > -jnp.inf, first, float(nbl))
                    sel = sel | pick
                    gv = jnp.where(pick, -jnp.inf, gv)
                sl = slopes_ref[h] * LOG2E
                bias = jnp.where(sel, -sl * dist0, NEG_BIG)
                b_hi, b_mid, b_lo = _split3(bias)
                s_hi, s_mid, s_lo = _split3(jnp.full((nbl, blk), sl, F32))
                srow = jnp.where(nidx == 0, s_hi, jnp.where(nidx == 1, s_mid, jnp.where(nidx == 2, s_lo, 0.0)))
                qh = qs[0:hd] if e == 0 else qs[hd:2 * hd]
                parts = [qh, b_hi, b_mid, b_lo, srow] if e == 0 else [b_hi, b_mid, b_lo, srow, qh]
                dst_ref[h] = jnp.concatenate(parts, axis=0).astype(BF16)

    @pl.when(i == 0)
    def _():
        lane = lax.broadcasted_iota(jnp.int32, (1, LANES), 1)
        first_half = lane < hd
        kbar = jnp.dot(avg_ref[...], k_ref[...], preferred_element_type=F32)
        for h in range(nh):
            hp, e = h // 2, h % 2
            cols = slice(hp * LANES, (hp + 1) * LANES)
            keep = first_half if e == 0 else jnp.logical_not(first_half)
            for j in range(nb):
                rows = slice(j * blk, (j + 1) * blk)
                ka_ref[h, j] = jnp.where(keep, k_ref[rows, cols], aug_ref[e, rows, :])
            pieces = _split3(jnp.where(keep, kbar[:, cols], 0.0))
            for p in range(3):
                kbp_ref[h * gr + p * nbl:h * gr + (p + 1) * nbl, :] = pieces[p].astype(BF16)
        ones_row = jnp.where(lax.broadcasted_iota(jnp.int32, (MOBA_VROWS - hd, blk), 0) == 0, 1.0, 0.0)
        for j in range(nb):
            vt = v_ref[j * blk:(j + 1) * blk, :].astype(F32).T
            for h in range(nh):
                vt_ref[j, h, 0:hd, :] = vt[h * hd:(h + 1) * hd].astype(BF16)
                vt_ref[j, h, hd:MOBA_VROWS, :] = ones_row.astype(BF16)
        build_queries(q_ref, 0, qa_ref)
        p_ref[...] = jnp.zeros(p_ref.shape, BF16)
        al_ref[...] = jnp.ones(al_ref.shape, F32)

    @pl.when(i > 0)
    def _():
        qa_ref[...] = qan_ref[...]

    m_ref[...] = jnp.full(m_ref.shape, -jnp.inf, F32)
    acc_ref[...] = jnp.zeros(acc_ref.shape, F32)

    def softmax(h, sT):
        m_old = m_ref[h]
        m_new = jnp.maximum(m_old, jnp.max(sT, axis=0, keepdims=True))
        m_ref[h] = m_new
        return jnp.exp2(sT - m_new).astype(BF16), jnp.exp2(m_old - m_new)

    def pair_scores(j0):
        return [jnp.dot(ka_ref[h, pl.ds(j0, 2)].reshape(2 * blk, LANES), qa_ref[h],
                        preferred_element_type=F32) for h in range(nh)]

    def pair_values(ja, jb, h):
        return jnp.concatenate([vt_ref[ja, h], vt_ref[jb, h]], axis=1)

    def pending_pv(j0):
        pa, pb = jnp.maximum(j0 - 2, 0), jnp.maximum(j0 - 1, 0)
        return [jnp.dot(pair_values(pa, pb, h), p_ref[h], preferred_element_type=F32) for h in range(nh)]

    def past(k, carry):
        scores = pair_scores(2 * k)
        pv = pending_pv(2 * k)
        sm = [softmax(h, scores[h]) for h in range(nh)]
        for h in range(nh):
            acc_ref[h] = al_ref[h] * acc_ref[h] + pv[h]
        for h in range(nh):
            p_ref[h], al_ref[h] = sm[h]
        return carry

    npairs = i // 2
    lax.fori_loop(0, npairs, past, 0)

    def finish(scores, last_values):
        pv = pending_pv(2 * npairs)
        build_queries(qn_ref, i + 1, qan_ref)
        sm = [softmax(h, scores[h]) for h in range(nh)]
        last = [jnp.dot(last_values(h), sm[h][0], preferred_element_type=F32) for h in range(nh)]
        outs = []
        for h in range(nh):
            acc = sm[h][1] * (al_ref[h] * acc_ref[h] + pv[h]) + last[h]
            outs.append(acc[0:hd] / acc[hd:hd + 1])
        ot = jnp.concatenate(outs, axis=0)
        o_ref[...] = (ot.T * _silu(z_ref[...].astype(F32))).astype(o_ref.dtype)

    @pl.when(i % 2 == 0)
    def _():
        finish([jnp.dot(ka_ref[h, i], qa_ref[h], preferred_element_type=F32) + mask_ref[blk:2 * blk, :]
                for h in range(nh)], lambda h: vt_ref[i, h])

    @pl.when(i % 2 == 1)
    def _():
        scores = pair_scores(i - 1)
        finish([scores[h] + mask_ref[...] for h in range(nh)], lambda h: pair_values(i - 1, i, h))


def _moba(proj, slopes, bsz, s):
    nb = s // MOBA_BLOCK
    assert nb <= MOBA_NBL and 2 * MOBA_HD == LANES and 3 * MOBA_NBL + 3 <= MOBA_HD
    blk = MOBA_BLOCK
    aug, avg, mask = _moba_constants(s)
    qblock = lambda c0: (lambda b, i, *_: (b * nb + i, c0 // BRANCH_W))
    qnext = lambda b, i, *_: (b * nb + jnp.minimum(i + 1, nb - 1), COL_QA // BRANCH_W)
    whole = lambda c0: (lambda b, i, *_: (b, c0 // BRANCH_W))
    return pl.pallas_call(
        _moba_body,
        grid_spec=pltpu.PrefetchScalarGridSpec(
            num_scalar_prefetch=1,
            grid=(bsz, nb),
            in_specs=[
                pl.BlockSpec((blk, BRANCH_W), qblock(COL_QA)),
                pl.BlockSpec((blk, BRANCH_W), qnext),
                pl.BlockSpec((s, BRANCH_W), whole(COL_KA)),
                pl.BlockSpec((s, BRANCH_W), whole(COL_VA)),
                pl.BlockSpec((blk, BRANCH_W), qblock(COL_ZA)),
                pl.BlockSpec((2, s, LANES), lambda b, i, *_: (0, 0, 0)),
                pl.BlockSpec((MOBA_NBL, s), lambda b, i, *_: (0, 0)),
                pl.BlockSpec((2 * blk, blk), lambda b, i, *_: (0, 0)),
            ],
            out_specs=pl.BlockSpec((blk, BRANCH_W), lambda b, i, *_: (b * nb + i, 0)),
            scratch_shapes=[
                pltpu.VMEM((MOBA_HEADS, nb, blk, LANES), BF16),
                pltpu.VMEM((nb, MOBA_HEADS, MOBA_VROWS, blk), BF16),
                pltpu.VMEM((MOBA_HEADS * MOBA_GATE_ROWS, LANES), BF16),
                pltpu.VMEM((MOBA_HEADS, LANES, blk), BF16),
                pltpu.VMEM((MOBA_HEADS, LANES, blk), BF16),
                pltpu.VMEM((MOBA_HEADS, MOBA_VROWS, blk), F32),
                pltpu.VMEM((MOBA_HEADS, 2 * blk, blk), BF16),
                pltpu.VMEM((MOBA_HEADS, 1, blk), F32),
                pltpu.VMEM((MOBA_HEADS, 1, blk), F32),
            ],
        ),
        out_shape=jax.ShapeDtypeStruct((bsz * s, BRANCH_W), BF16),
        compiler_params=pltpu.CompilerParams(
            dimension_semantics=("parallel", "arbitrary"), vmem_limit_bytes=VMEM_LIMIT),
        name="moba",
    )(slopes, proj, proj, proj, proj, proj, aug, avg, mask)


def _merge_body(x_ref, ga_ref, gb_ref, gc_ref, ya_ref, yb_ref, yc_ref,
                wa_ref, wb_ref, wc_ref, wo_ref, pw_ref, o_ref):
    tm = x_ref.shape[0]
    halves = [slice(0, tm // 2), slice(tm // 2, tm)]
    branches = ((ga_ref, ya_ref, wa_ref), (gb_ref, yb_ref, wb_ref), (gc_ref, yc_ref, wc_ref))
    proj = [[jnp.dot(y_ref[rows, :], w_ref[...], preferred_element_type=F32) for _, y_ref, w_ref in branches]
            for rows in halves]
    gated = [[_sigmoid(g_ref[rows, :].astype(F32)) * proj[k][n] for n, (g_ref, _, _) in enumerate(branches)]
             for k, rows in enumerate(halves)]
    merged = [gated[k][0] + gated[k][1] + gated[k][2] for k in range(2)]
    y = [jnp.dot(merged[k].astype(BF16), wo_ref[...], preferred_element_type=F32) for k in range(2)]
    for k, rows in enumerate(halves):
        yn = y[k] * lax.rsqrt(jnp.mean(y[k] * y[k], axis=-1, keepdims=True) + EPS) * pw_ref[...]
        o_ref[rows, :] = x_ref[rows, :] + yn


def _merge(x2, proj, ya, yb, yc, wa, wb, wc, wo, post_w, tm=1024):
    t, d = x2.shape
    row = lambda c: (lambda i: (i, c))
    const = lambda shape: pl.BlockSpec(shape, lambda i: (0, 0))
    return pl.pallas_call(
        _merge_body,
        grid=(t // tm,),
        in_specs=[
            pl.BlockSpec((tm, d), row(0)),
            pl.BlockSpec((tm, d), row(COL_GATE_A // d)),
            pl.BlockSpec((tm, d), row(COL_GATE_B // d)),
            pl.BlockSpec((tm, d), row(COL_GATE_C // d)),
            pl.BlockSpec((tm, BRANCH_W), row(0)),
            pl.BlockSpec((tm, BRANCH_W), row(0)),
            pl.BlockSpec((tm, BRANCH_W), row(0)),
            const((BRANCH_W, d)), const((BRANCH_W, d)), const((BRANCH_W, d)),
            const((d, d)), const((1, d)),
        ],
        out_specs=pl.BlockSpec((tm, d), row(0)),
        out_shape=jax.ShapeDtypeStruct((t, d), F32),
        compiler_params=pltpu.CompilerParams(
            dimension_semantics=("parallel",), vmem_limit_bytes=VMEM_LIMIT),
        name="merge",
    )(x2, proj, proj, proj, ya, yb, yc, wa, wb, wc, wo, post_w.reshape(1, d))


def kernel(x, mem, pre_norm_w, w_in, hgrn_lb_logits, hgrn_norm_w, mem_norm_w, w_mem_kv,
           w_branch_a, w_branch_b, w_branch_c, w_out, post_norm_w):
    bsz, s, d = x.shape
    assert w_in.shape[0] == 1 and w_in.shape[2] == PROJ_TOTAL and d == 1024
    assert s % MOBA_BLOCK == 0
    x2 = x.reshape(bsz * s, d)
    proj, yb, yc = _in_proj(x2, pre_norm_w[0], w_in[0].astype(BF16), hgrn_lb_logits, hgrn_norm_w[0],
                            mem, mem_norm_w[0], w_mem_kv[0].astype(BF16), s)
    slopes = jnp.exp2(-8.0 * jnp.arange(1, MOBA_HEADS + 1, dtype=F32) / MOBA_HEADS)
    ya = _moba(proj, slopes, bsz, s)
    out = _merge(x2, proj, ya, yb, yc, w_branch_a[0].astype(BF16), w_branch_b[0].astype(BF16),
                 w_branch_c[0].astype(BF16), w_out[0].astype(BF16), post_norm_w[0])
    return out.reshape(bsz, s, d)
```

```python
import functools

import jax
import jax.numpy as jnp
import numpy as np
from jax import lax
from jax.experimental import pallas as pl
from jax.experimental.pallas import tpu as pltpu

F32 = jnp.float32
BF16 = jnp.bfloat16
EPS = 1e-6

MOBA_HEADS, MOBA_HD, MOBA_BLOCK, MOBA_TOPK = 8, 64, 256, 3
HGRN_HEADS, HGRN_D, HGRN_CHUNK = 4, 128, 32
XA_HEADS, XA_HD = 4, 128
BRANCH_W = 512
LANES = 128
COL_QA, COL_KA, COL_VA, COL_ZA = 0, 512, 1024, 1536
COL_FB, COL_IB, COL_QB, COL_GB = 2048, 2560, 3072, 3584
COL_QC, COL_ZC = 4096, 4608
COL_GATE_A, COL_GATE_B, COL_GATE_C = 5120, 6144, 7168
PROJ_TOTAL = COL_GATE_C + 1024

NEG_BIG = -1e30
LOG2E = 1.4426950408889634
MOBA_VROWS = MOBA_HD + 16
MOBA_NBL = 16
MOBA_GATE_ROWS = 3 * MOBA_NBL
VMEM_LIMIT = 56 * 1024 * 1024


def _sigmoid(z):
    return 1.0 / (1.0 + jnp.exp2(z * (-LOG2E)))


def _silu(z):
    return z * _sigmoid(z)


def _split3(a):
    hi = a.astype(BF16).astype(F32)
    r = a - hi
    mid = r.astype(BF16).astype(F32)
    lo = (r - mid).astype(BF16).astype(F32)
    return hi, mid, lo


IN_PROJ_CHUNK = 1024
HGRN_TILE = 256


def _in_proj_body(x_ref, nw_ref, w_ref, lbl_ref, hnw_ref, lmat_ref, mem_ref, mw_ref, wkv_ref,
                  o_ref, yb_ref, yc_ref, st_ref, km_ref, vm_ref, *, steps_per_batch):
    @pl.when(pl.program_id(0) % steps_per_batch == 0)
    def _():
        st_ref[...] = jnp.zeros_like(st_ref)
        m = mem_ref[0]
        mn = m * lax.rsqrt(jnp.mean(m * m, axis=-1, keepdims=True) + EPS) * mw_ref[...]
        kv = jnp.dot(mn.astype(BF16), wkv_ref[...], preferred_element_type=F32)
        km_ref[...] = kv[:, :BRANCH_W].astype(BF16)
        vm_ref[...] = kv[:, BRANCH_W:].astype(BF16)

    x = x_ref[...]
    ms = jnp.mean(x * x, axis=-1, keepdims=True)
    hx = (x * lax.rsqrt(ms + EPS) * nw_ref[...]).astype(BF16)

    def chunk(c0):
        out = jnp.dot(hx, w_ref[:, c0:c0 + IN_PROJ_CHUNK], preferred_element_type=F32).astype(o_ref.dtype)
        o_ref[:, c0:c0 + IN_PROJ_CHUNK] = out
        return out

    fi = chunk(COL_FB)
    qg = chunk(COL_QB)
    f_all, i_all = fi[:, :BRANCH_W], fi[:, BRANCH_W:]
    q_all, g_all = qg[:, :BRANCH_W], qg[:, BRANCH_W:]

    r, c, d = HGRN_TILE, HGRN_CHUNK, HGRN_D
    n_chunks = r // c
    heads = range(HGRN_HEADS)
    tiles = range(x_ref.shape[0] // r)
    chunks = range(n_chunks)
    cols = [slice(h * d, (h + 1) * d) for h in heads]
    rows = [slice(t * r, (t + 1) * r) for t in tiles]
    nt_dims = (((1,), (1,)), ((), ()))

    logits = lbl_ref[...]
    ex = jnp.exp(logits - jnp.max(logits, axis=0, keepdims=True))
    lb = ex[0:1, :] / jnp.sum(ex, axis=0, keepdims=True)

    kk, pieces = [], []
    for t in tiles:
        fl = f_all[rows[t], :].astype(F32)
        log_f = jnp.log(lb + (1.0 - lb) * _sigmoid(fl))
        kk.append((1.0 - lb) * _sigmoid(-fl))
        pieces.append([p.astype(BF16) for p in _split3(log_f)])
    qz = chunk(COL_QC)
    xcols = [slice(h * XA_HD, (h + 1) * XA_HD) for h in range(XA_HEADS)]

    def xa_logits(hs):
        return [lax.dot_general(qz[:, xcols[h]], km_ref[:, xcols[h]], nt_dims,
                                preferred_element_type=F32) * (XA_HD ** -0.5 * LOG2E) for h in hs]

    def xa_softmax(logits):
        ps, dens = [], []
        for lg in logits:
            p = jnp.exp2(lg - jnp.max(lg, axis=-1, keepdims=True))
            dens.append(jnp.sum(p, axis=-1, keepdims=True))
            ps.append(p.astype(BF16))
        return ps, dens

    def xa_values(ps, dens, hs):
        return [jnp.dot(p, vm_ref[:, xcols[h]], preferred_element_type=F32) / den
                for p, den, h in zip(ps, dens, hs)]

    b = []
    for t in tiles:
        lmat = lmat_ref[...]
        hi, mid, lo = [jnp.dot(lmat, p, preferred_element_type=F32) for p in pieces[t]]
        b.append(hi + mid + lo)
    chunk(COL_QA)

    q_t, k_t32, k_t, dec = [], [], [], []
    for t in tiles:
        q_t.append((q_all[rows[t], :].astype(F32) * jnp.exp(b[t])).astype(BF16))
        kt = kk[t] * jnp.exp(-b[t])
        k_t32.append(kt)
        k_t.append(kt.astype(BF16))
        dec.append([jnp.exp(b[t][ci * c + c - 1:ci * c + c, :]) for ci in chunks])

    a = [[lax.dot_general(q_t[t][:, cols[h]], k_t[t][:, cols[h]], nt_dims, preferred_element_type=F32)
          for h in heads] for t in tiles]
    u_t = []
    for t in tiles:
        per_head = []
        for h in heads:
            vt = i_all[rows[t], cols[h]].astype(F32).T.astype(BF16)
            blocks = []
            for ci in chunks:
                slab = (k_t32[t][ci * c:(ci + 1) * c, cols[h]] * dec[t][ci][:, cols[h]]).astype(BF16)
                parts = ([jnp.zeros((c, ci * d), BF16)] if ci else []) + [slab]
                if ci < n_chunks - 1:
                    parts.append(jnp.zeros((c, (n_chunks - 1 - ci) * d), BF16))
                blocks.append(jnp.concatenate(parts, axis=1))
            kd_blk = jnp.concatenate(blocks, axis=0)
            per_head.append(jnp.dot(vt, kd_blk, preferred_element_type=F32))
        u_t.append(per_head)
    xa_lg = xa_logits((0, 1))
    chunk(COL_VA)
    o_intra = [[jnp.dot(a[t][h].astype(BF16) * lmat_ref[...], i_all[rows[t], cols[h]],
                        preferred_element_type=F32) for h in heads] for t in tiles]
    xa_o = xa_values(*xa_softmax(xa_lg), (0, 1))
    xa_lg = xa_logits((2, 3))
    chunk(COL_GATE_A)

    states = [[None] * HGRN_HEADS for _ in tiles]
    for h in heads:
        st = st_ref[h]
        for t in tiles:
            before = []
            for ci in chunks:
                before.append(st.astype(BF16))
                st = st * dec[t][ci][:, cols[h]] + u_t[t][h][:, ci * d:(ci + 1) * d]
            states[t][h] = before
        st_ref[h] = st
    xa_o += xa_values(*xa_softmax(xa_lg), (2, 3))
    chunk(COL_GATE_B)

    odd = (lax.broadcasted_iota(jnp.int32, (2 * c, d), 0) >= c)
    o_inter = []
    for t in tiles:
        per_head = []
        for h in heads:
            outs = []
            for ci in range(0, n_chunks, 2):
                qp = q_t[t][ci * c:(ci + 2) * c, cols[h]]
                zero = jnp.zeros_like(qp)
                lhs = jnp.concatenate([jnp.where(odd, zero, qp), jnp.where(odd, qp, zero)], axis=1)
                st2 = jnp.concatenate([states[t][h][ci], states[t][h][ci + 1]], axis=1)
                outs.append(lax.dot_general(lhs, st2, nt_dims, preferred_element_type=F32))
            per_head.append(outs)
        o_inter.append(per_head)
    for h in range(XA_HEADS):
        zc = qz[:, BRANCH_W + h * XA_HD:BRANCH_W + (h + 1) * XA_HD].astype(F32)
        yc_ref[:, xcols[h]] = (xa_o[h] * _silu(zc)).astype(yc_ref.dtype)
    chunk(COL_GATE_C)
    for t in tiles:
        for h in heads:
            o = o_intra[t][h] + jnp.concatenate(o_inter[t][h], axis=0)
            on = o * lax.rsqrt(jnp.mean(o * o, axis=-1, keepdims=True) + EPS) * hnw_ref[...]
            yb_ref[rows[t], cols[h]] = (on * _silu(g_all[rows[t], cols[h]].astype(F32))).astype(yb_ref.dtype)


def _in_proj(x2, norm_w, w_bf16, lb_logits, hgrn_norm_w, mem, mem_norm_w, wkv_bf16, s, tm=512):
    t, d = x2.shape
    n = w_bf16.shape[1]
    n_mem = mem.shape[1]
    spb = s // tm
    assert n == 8 * IN_PROJ_CHUNK and s % tm == 0 and tm % HGRN_TILE == 0
    r, c = HGRN_TILE, HGRN_CHUNK
    ri = np.arange(r)
    same = (ri[:, None] // c) == (ri[None, :] // c)
    tril = (same & (ri[None, :] <= ri[:, None])).astype(np.float32)
    const = lambda shape: pl.BlockSpec(shape, lambda i: (0,) * len(shape))
    return pl.pallas_call(
        functools.partial(_in_proj_body, steps_per_batch=spb),
        grid=(t // tm,),
        in_specs=[
            pl.BlockSpec((tm, d), lambda i: (i, 0)),
            const((1, d)),
            pl.BlockSpec((d, n), lambda i: (0, 0), pipeline_mode=pl.Buffered(1)),
            const((lb_logits.shape[0], BRANCH_W)),
            const((1, HGRN_D)),
            const((r, r)),
            pl.BlockSpec((1, n_mem, d), lambda i: (i // spb, 0, 0)),
            const((1, d)),
            pl.BlockSpec((d, 2 * BRANCH_W), lambda i: (0, 0), pipeline_mode=pl.Buffered(1)),
        ],
        out_specs=[pl.BlockSpec((tm, n), lambda i: (i, 0)), pl.BlockSpec((tm, BRANCH_W), lambda i: (i, 0)),
                   pl.BlockSpec((tm, BRANCH_W), lambda i: (i, 0))],
        out_shape=[jax.ShapeDtypeStruct((t, n), BF16), jax.ShapeDtypeStruct((t, BRANCH_W), BF16),
                   jax.ShapeDtypeStruct((t, BRANCH_W), BF16)],
        scratch_shapes=[pltpu.VMEM((HGRN_HEADS, HGRN_D, HGRN_D), F32),
                        pltpu.VMEM((n_mem, BRANCH_W), BF16), pltpu.VMEM((n_mem, BRANCH_W), BF16)],
        compiler_params=pltpu.CompilerParams(
            dimension_semantics=("arbitrary",), vmem_limit_bytes=VMEM_LIMIT),
        name="in_proj",
    )(x2, norm_w.reshape(1, d), w_bf16, lb_logits, hgrn_norm_w.reshape(1, HGRN_D), tril.astype(BF16),
      mem, mem_norm_w.reshape(1, d), wkv_bf16)


def _moba_constants(s):
    blk, hd, nbl = MOBA_BLOCK, MOBA_HD, MOBA_NBL
    row = np.arange(s)
    off = (row % blk).astype(np.float32)
    onehot = (row[:, None] // blk == np.arange(nbl)[None, :]).astype(np.float32)
    bias_lanes = np.concatenate([onehot, onehot, onehot, off[:, None], off[:, None], off[:, None],
                                 np.zeros((s, hd - 3 * nbl - 3), np.float32)], axis=1)
    zeros = np.zeros((s, hd), np.float32)
    a0 = np.concatenate([zeros, bias_lanes], axis=1)
    a1 = np.concatenate([bias_lanes, zeros], axis=1)
    aug = np.stack([a0, a1]).astype(BF16)
    avg = ((np.arange(nbl)[:, None] == (row // blk)[None, :]).astype(np.float32) / blk).astype(BF16)
    causal = np.where(np.arange(blk)[:, None] <= np.arange(blk)[None, :], 0.0, -np.inf).astype(np.float32)
    mask = np.concatenate([np.zeros((blk, blk), np.float32), causal])
    return aug, avg, mask


def _moba_body(slopes_ref, q_ref, qn_ref, k_ref, v_ref, z_ref, aug_ref, avg_ref, mask_ref, o_ref,
               ka_ref, vt_ref, kbp_ref, qa_ref, qan_ref, acc_ref, p_ref, m_ref, al_ref):
    s = k_ref.shape[0]
    nb = s // MOBA_BLOCK
    blk = MOBA_BLOCK
    nh = MOBA_HEADS
    hd = MOBA_HD
    nbl, gr = MOBA_NBL, MOBA_GATE_ROWS
    i = pl.program_id(1)

    def build_queries(src_ref, iq, dst_ref):
        nidx = lax.broadcasted_iota(jnp.int32, (nbl, blk), 0)
        nidx_f = nidx.astype(F32)
        qoff = lax.broadcasted_iota(jnp.int32, (nbl, blk), 1)
        dist0 = (qoff + (iq - nidx) * blk).astype(F32)
        qts = [(src_ref[:, hp * LANES:(hp + 1) * LANES].astype(F32) * (hd ** -0.5)).T
               for hp in range(nh // 2)]
        gates = [jnp.dot(kbp_ref[hp * 2 * gr:(hp + 1) * 2 * gr, :], qts[hp].astype(BF16),
                         preferred_element_type=F32) for hp in range(nh // 2)]
        for hp in range(nh // 2):
            qs = qts[hp] * LOG2E
            for e in range(2):
                h = 2 * hp + e
                gg = gates[hp][e * gr:(e + 1) * gr]
                g = gg[0:nbl] + gg[nbl:2 * nbl] + gg[2 * nbl:3 * nbl]
                gv = jnp.where(nidx < iq, g, -jnp.inf)
                sel = nidx == iq
                for _ in range(MOBA_TOPK):
                    mx = jnp.max(gv, axis=0, keepdims=True)
                    first = jnp.min(jnp.where(gv == mx, nidx_f, float(nbl)), axis=0, keepdims=True)
                    pick = nidx_f == jnp.where(mx > -jnp.inf, first, float(nbl))
                    sel = sel | pick
                    gv = jnp.where(pick, -jnp.inf, gv)
                sl = slopes_ref[h] * LOG2E
                bias = jnp.where(sel, -sl * dist0, NEG_BIG)
                b_hi, b_mid, b_lo = _split3(bias)
                s_hi, s_mid, s_lo = _split3(jnp.full((nbl, blk), sl, F32))
                srow = jnp.where(nidx == 0, s_hi, jnp.where(nidx == 1, s_mid, jnp.where(nidx == 2, s_lo, 0.0)))
                qh = qs[0:hd] if e == 0 else qs[hd:2 * hd]
                parts = [qh, b_hi, b_mid, b_lo, srow] if e == 0 else [b_hi, b_mid, b_lo, srow, qh]
                dst_ref[h] = jnp.concatenate(parts, axis=0).astype(BF16)

    @pl.when(i == 0)
    def _():
        lane = lax.broadcasted_iota(jnp.int32, (1, LANES), 1)
        first_half = lane < hd
        kbar = jnp.dot(avg_ref[...], k_ref[...], preferred_element_type=F32)
        for h in range(nh):
            hp, e = h // 2, h % 2
            cols = slice(hp * LANES, (hp + 1) * LANES)
            keep = first_half if e == 0 else jnp.logical_not(first_half)
            for j in range(nb):
                rows = slice(j * blk, (j + 1) * blk)
                ka_ref[h, j] = jnp.where(keep, k_ref[rows, cols], aug_ref[e, rows, :])
            pieces = _split3(jnp.where(keep, kbar[:, cols], 0.0))
            for p in range(3):
                kbp_ref[h * gr + p * nbl:h * gr + (p + 1) * nbl, :] = pieces[p].astype(BF16)
        ones_row = jnp.where(lax.broadcasted_iota(jnp.int32, (MOBA_VROWS - hd, blk), 0) == 0, 1.0, 0.0)
        for j in range(nb):
            vt = v_ref[j * blk:(j + 1) * blk, :].astype(F32).T
            for h in range(nh):
                vt_ref[j, h, 0:hd, :] = vt[h * hd:(h + 1) * hd].astype(BF16)
                vt_ref[j, h, hd:MOBA_VROWS, :] = ones_row.astype(BF16)
        build_queries(q_ref, 0, qa_ref)
        p_ref[...] = jnp.zeros(p_ref.shape, BF16)
        al_ref[...] = jnp.ones(al_ref.shape, F32)

    @pl.when(i > 0)
    def _():
        qa_ref[...] = qan_ref[...]

    m_ref[...] = jnp.full(m_ref.shape, -jnp.inf, F32)
    acc_ref[...] = jnp.zeros(acc_ref.shape, F32)

    def softmax(h, sT):
        m_old = m_ref[h]
        m_new = jnp.maximum(m_old, jnp.max(sT, axis=0, keepdims=True))
        m_ref[h] = m_new
        return jnp.exp2(sT - m_new).astype(BF16), jnp.exp2(m_old - m_new)

    def pair_scores(j0):
        return [jnp.dot(ka_ref[h, pl.ds(j0, 2)].reshape(2 * blk, LANES), qa_ref[h],
                        preferred_element_type=F32) for h in range(nh)]

    def pair_values(ja, jb, h):
        return jnp.concatenate([vt_ref[ja, h], vt_ref[jb, h]], axis=1)

    def pending_pv(j0):
        pa, pb = jnp.maximum(j0 - 2, 0), jnp.maximum(j0 - 1, 0)
        return [jnp.dot(pair_values(pa, pb, h), p_ref[h], preferred_element_type=F32) for h in range(nh)]

    def past(k, carry):
        scores = pair_scores(2 * k)
        pv = pending_pv(2 * k)
        sm = [softmax(h, scores[h]) for h in range(nh)]
        for h in range(nh):
            acc_ref[h] = al_ref[h] * acc_ref[h] + pv[h]
        for h in range(nh):
            p_ref[h], al_ref[h] = sm[h]
        return carry

    npairs = i // 2
    lax.fori_loop(0, npairs, past, 0)

    def finish(scores, last_values):
        pv = pending_pv(2 * npairs)
        build_queries(qn_ref, i + 1, qan_ref)
        sm = [softmax(h, scores[h]) for h in range(nh)]
        last = [jnp.dot(last_values(h), sm[h][0], preferred_element_type=F32) for h in range(nh)]
        outs = []
        for h in range(nh):
            acc = sm[h][1] * (al_ref[h] * acc_ref[h] + pv[h]) + last[h]
            outs.append(acc[0:hd] / acc[hd:hd + 1])
        ot = jnp.concatenate(outs, axis=0)
        o_ref[...] = (ot.T * _silu(z_ref[...].astype(F32))).astype(o_ref.dtype)

    @pl.when(i % 2 == 0)
    def _():
        finish([jnp.dot(ka_ref[h, i], qa_ref[h], preferred_element_type=F32) + mask_ref[blk:2 * blk, :]
                for h in range(nh)], lambda h: vt_ref[i, h])

    @pl.when(i % 2 == 1)
    def _():
        scores = pair_scores(i - 1)
        finish([scores[h] + mask_ref[...] for h in range(nh)], lambda h: pair_values(i - 1, i, h))


def _moba(proj, slopes, bsz, s):
    nb = s // MOBA_BLOCK
    assert nb <= MOBA_NBL and 2 * MOBA_HD == LANES and 3 * MOBA_NBL + 3 <= MOBA_HD
    blk = MOBA_BLOCK
    aug, avg, mask = _moba_constants(s)
    qblock = lambda c0: (lambda b, i, *_: (b * nb + i, c0 // BRANCH_W))
    qnext = lambda b, i, *_: (b * nb + jnp.minimum(i + 1, nb - 1), COL_QA // BRANCH_W)
    whole = lambda c0: (lambda b, i, *_: (b, c0 // BRANCH_W))
    return pl.pallas_call(
        _moba_body,
        grid_spec=pltpu.PrefetchScalarGridSpec(
            num_scalar_prefetch=1,
            grid=(bsz, nb),
            in_specs=[
                pl.BlockSpec((blk, BRANCH_W), qblock(COL_QA)),
                pl.BlockSpec((blk, BRANCH_W), qnext),
                pl.BlockSpec((s, BRANCH_W), whole(COL_KA)),
                pl.BlockSpec((s, BRANCH_W), whole(COL_VA)),
                pl.BlockSpec((blk, BRANCH_W), qblock(COL_ZA)),
                pl.BlockSpec((2, s, LANES), lambda b, i, *_: (0, 0, 0)),
                pl.BlockSpec((MOBA_NBL, s), lambda b, i, *_: (0, 0)),
                pl.BlockSpec((2 * blk, blk), lambda b, i, *_: (0, 0)),
            ],
            out_specs=pl.BlockSpec((blk, BRANCH_W), lambda b, i, *_: (b * nb + i, 0)),
            scratch_shapes=[
                pltpu.VMEM((MOBA_HEADS, nb, blk, LANES), BF16),
                pltpu.VMEM((nb, MOBA_HEADS, MOBA_VROWS, blk), BF16),
                pltpu.VMEM((MOBA_HEADS * MOBA_GATE_ROWS, LANES), BF16),
                pltpu.VMEM((MOBA_HEADS, LANES, blk), BF16),
                pltpu.VMEM((MOBA_HEADS, LANES, blk), BF16),
                pltpu.VMEM((MOBA_HEADS, MOBA_VROWS, blk), F32),
                pltpu.VMEM((MOBA_HEADS, 2 * blk, blk), BF16),
                pltpu.VMEM((MOBA_HEADS, 1, blk), F32),
                pltpu.VMEM((MOBA_HEADS, 1, blk), F32),
            ],
        ),
        out_shape=jax.ShapeDtypeStruct((bsz * s, BRANCH_W), BF16),
        compiler_params=pltpu.CompilerParams(
            dimension_semantics=("parallel", "arbitrary"), vmem_limit_bytes=VMEM_LIMIT),
        name="moba",
    )(slopes, proj, proj, proj, proj, proj, aug, avg, mask)


def _merge_body(x_ref, ga_ref, gb_ref, gc_ref, ya_ref, yb_ref, yc_ref,
                wa_ref, wb_ref, wc_ref, wo_ref, pw_ref, o_ref):
    tm = x_ref.shape[0]
    halves = [slice(0, tm // 2), slice(tm // 2, tm)]
    branches = ((ga_ref, ya_ref, wa_ref), (gb_ref, yb_ref, wb_ref), (gc_ref, yc_ref, wc_ref))
    proj = [[jnp.dot(y_ref[rows, :], w_ref[...], preferred_element_type=F32) for _, y_ref, w_ref in branches]
            for rows in halves]
    gated = [[_sigmoid(g_ref[rows, :].astype(F32)) * proj[k][n] for n, (g_ref, _, _) in enumerate(branches)]
             for k, rows in enumerate(halves)]
    merged = [gated[k][0] + gated[k][1] + gated[k][2] for k in range(2)]
    y = [jnp.dot(merged[k].astype(BF16), wo_ref[...], preferred_element_type=F32) for k in range(2)]
    for k, rows in enumerate(halves):
        yn = y[k] * lax.rsqrt(jnp.mean(y[k] * y[k], axis=-1, keepdims=True) + EPS) * pw_ref[...]
        o_ref[rows, :] = x_ref[rows, :] + yn


def _merge(x2, proj, ya, yb, yc, wa, wb, wc, wo, post_w, tm=1024):
    t, d = x2.shape
    row = lambda c: (lambda i: (i, c))
    const = lambda shape: pl.BlockSpec(shape, lambda i: (0, 0))
    return pl.pallas_call(
        _merge_body,
        grid=(t // tm,),
        in_specs=[
            pl.BlockSpec((tm, d), row(0)),
            pl.BlockSpec((tm, d), row(COL_GATE_A // d)),
            pl.BlockSpec((tm, d), row(COL_GATE_B // d)),
            pl.BlockSpec((tm, d), row(COL_GATE_C // d)),
            pl.BlockSpec((tm, BRANCH_W), row(0)),
            pl.BlockSpec((tm, BRANCH_W), row(0)),
            pl.BlockSpec((tm, BRANCH_W), row(0)),
            const((BRANCH_W, d)), const((BRANCH_W, d)), const((BRANCH_W, d)),
            const((d, d)), const((1, d)),
        ],
        out_specs=pl.BlockSpec((tm, d), row(0)),
        out_shape=jax.ShapeDtypeStruct((t, d), F32),
        compiler_params=pltpu.CompilerParams(
            dimension_semantics=("parallel",), vmem_limit_bytes=VMEM_LIMIT),
        name="merge",
    )(x2, proj, proj, proj, ya, yb, yc, wa, wb, wc, wo, post_w.reshape(1, d))


def kernel(x, mem, pre_norm_w, w_in, hgrn_lb_logits, hgrn_norm_w, mem_norm_w, w_mem_kv,
           w_branch_a, w_branch_b, w_branch_c, w_out, post_norm_w):
    bsz, s, d = x.shape
    assert w_in.shape[0] == 1 and w_in.shape[2] == PROJ_TOTAL and d == 1024
    assert s % MOBA_BLOCK == 0
    x2 = x.reshape(bsz * s, d)
    proj, yb, yc = _in_proj(x2, pre_norm_w[0], w_in[0].astype(BF16), hgrn_lb_logits, hgrn_norm_w[0],
                            mem, mem_norm_w[0], w_mem_kv[0].astype(BF16), s)
    slopes = jnp.exp2(-8.0 * jnp.arange(1, MOBA_HEADS + 1, dtype=F32) / MOBA_HEADS)
    ya = _moba(proj, slopes, bsz, s)
    out = _merge(x2, proj, ya, yb, yc, w_branch_a[0].astype(BF16), w_branch_b[0].astype(BF16),
                 w_branch_c[0].astype(BF16), w_out[0].astype(BF16), post_norm_w[0])
    return out.reshape(bsz, s, d)
```

```python
import functools

import jax
import jax.numpy as jnp
import numpy as np
from jax import lax
from jax.experimental import pallas as pl
from jax.experimental.pallas import tpu as pltpu

F32 = jnp.float32
BF16 = jnp.bfloat16
EPS = 1e-6

MOBA_HEADS, MOBA_HD, MOBA_BLOCK, MOBA_TOPK = 8, 64, 256, 3
HGRN_HEADS, HGRN_D, HGRN_CHUNK = 4, 128, 32
XA_HEADS, XA_HD = 4, 128
BRANCH_W = 512
LANES = 128
COL_QA, COL_KA, COL_VA, COL_ZA = 0, 512, 1024, 1536
COL_FB, COL_IB, COL_QB, COL_GB = 2048, 2560, 3072, 3584
COL_QC, COL_ZC = 4096, 4608
COL_GATE_A, COL_GATE_B, COL_GATE_C = 5120, 6144, 7168
PROJ_TOTAL = COL_GATE_C + 1024

NEG_BIG = -1e30
LOG2E = 1.4426950408889634
MOBA_VROWS = MOBA_HD + 16
MOBA_NBL = 16
MOBA_GATE_ROWS = 3 * MOBA_NBL
VMEM_LIMIT = 56 * 1024 * 1024


def _sigmoid(z):
    return 1.0 / (1.0 + jnp.exp2(z * (-LOG2E)))


def _silu(z):
    return z * _sigmoid(z)


def _split3(a):
    hi = a.astype(BF16).astype(F32)
    r = a - hi
    mid = r.astype(BF16).astype(F32)
    lo = (r - mid).astype(BF16).astype(F32)
    return hi, mid, lo


IN_PROJ_CHUNK = 1024
HGRN_TILE = 256


def _in_proj_body(x_ref, nw_ref, w_ref, lbl_ref, hnw_ref, lmat_ref, mem_ref, mw_ref, wkv_ref,
                  o_ref, yb_ref, yc_ref, st_ref, km_ref, vm_ref, *, steps_per_batch):
    @pl.when(pl.program_id(0) % steps_per_batch == 0)
    def _():
        st_ref[...] = jnp.zeros_like(st_ref)
        m = mem_ref[0]
        mn = m * lax.rsqrt(jnp.mean(m * m, axis=-1, keepdims=True) + EPS) * mw_ref[...]
        kv = jnp.dot(mn.astype(BF16), wkv_ref[...], preferred_element_type=F32)
        km_ref[...] = kv[:, :BRANCH_W].astype(BF16)
        vm_ref[...] = kv[:, BRANCH_W:].astype(BF16)

    x = x_ref[...]
    ms = jnp.mean(x * x, axis=-1, keepdims=True)
    hx = (x * lax.rsqrt(ms + EPS) * nw_ref[...]).astype(BF16)

    def chunk(c0):
        out = jnp.dot(hx, w_ref[:, c0:c0 + IN_PROJ_CHUNK], preferred_element_type=F32).astype(o_ref.dtype)
        o_ref[:, c0:c0 + IN_PROJ_CHUNK] = out
        return out

    fi = chunk(COL_FB)
    qg = chunk(COL_QB)
    f_all, i_all = fi[:, :BRANCH_W], fi[:, BRANCH_W:]
    q_all, g_all = qg[:, :BRANCH_W], qg[:, BRANCH_W:]

    r, c, d = HGRN_TILE, HGRN_CHUNK, HGRN_D
    n_chunks = r // c
    heads = range(HGRN_HEADS)
    tiles = range(x_ref.shape[0] // r)
    chunks = range(n_chunks)
    cols = [slice(h * d, (h + 1) * d) for h in heads]
    rows = [slice(t * r, (t + 1) * r) for t in tiles]
    nt_dims = (((1,), (1,)), ((), ()))

    logits = lbl_ref[...]
    ex = jnp.exp(logits - jnp.max(logits, axis=0, keepdims=True))
    lb = ex[0:1, :] / jnp.sum(ex, axis=0, keepdims=True)

    kk, pieces = [], []
    for t in tiles:
        fl = f_all[rows[t], :].astype(F32)
        log_f = jnp.log(lb + (1.0 - lb) * _sigmoid(fl))
        kk.append((1.0 - lb) * _sigmoid(-fl))
        pieces.append([p.astype(BF16) for p in _split3(log_f)])
    qz = chunk(COL_QC)
    xcols = [slice(h * XA_HD, (h + 1) * XA_HD) for h in range(XA_HEADS)]

    def xa_logits(hs):
        return [lax.dot_general(qz[:, xcols[h]], km_ref[:, xcols[h]], nt_dims,
                                preferred_element_type=F32) * (XA_HD ** -0.5 * LOG2E) for h in hs]

    def xa_softmax(logits):
        ps, dens = [], []
        for lg in logits:
            p = jnp.exp2(lg - jnp.max(lg, axis=-1, keepdims=True))
            dens.append(jnp.sum(p, axis=-1, keepdims=True))
            ps.append(p.astype(BF16))
        return ps, dens

    def xa_values(ps, dens, hs):
        return [jnp.dot(p, vm_ref[:, xcols[h]], preferred_element_type=F32) / den
                for p, den, h in zip(ps, dens, hs)]

    b = []
    for t in tiles:
        lmat = lmat_ref[...]
        hi, mid, lo = [jnp.dot(lmat, p, preferred_element_type=F32) for p in pieces[t]]
        b.append(hi + mid + lo)
    chunk(COL_QA)

    q_t, k_t32, k_t, dec = [], [], [], []
    for t in tiles:
        q_t.append((q_all[rows[t], :].astype(F32) * jnp.exp(b[t])).astype(BF16))
        kt = kk[t] * jnp.exp(-b[t])
        k_t32.append(kt)
        k_t.append(kt.astype(BF16))
        dec.append([jnp.exp(b[t][ci * c + c - 1:ci * c + c, :]) for ci in chunks])

    a = [[lax.dot_general(q_t[t][:, cols[h]], k_t[t][:, cols[h]], nt_dims, preferred_element_type=F32)
          for h in heads] for t in tiles]
    u_t = []
    for t in tiles:
        per_head = []
        for h in heads:
            vt = i_all[rows[t], cols[h]].astype(F32).T.astype(BF16)
            blocks = []
            for ci in chunks:
                slab = (k_t32[t][ci * c:(ci + 1) * c, cols[h]] * dec[t][ci][:, cols[h]]).astype(BF16)
                parts = ([jnp.zeros((c, ci * d), BF16)] if ci else []) + [slab]
                if ci < n_chunks - 1:
                    parts.append(jnp.zeros((c, (n_chunks - 1 - ci) * d), BF16))
                blocks.append(jnp.concatenate(parts, axis=1))
            kd_blk = jnp.concatenate(blocks, axis=0)
            per_head.append(jnp.dot(vt, kd_blk, preferred_element_type=F32))
        u_t.append(per_head)
    xa_lg = xa_logits((0, 1))
    chunk(COL_VA)
    o_intra = [[jnp.dot(a[t][h].astype(BF16) * lmat_ref[...], i_all[rows[t], cols[h]],
                        preferred_element_type=F32) for h in heads] for t in tiles]
    xa_o = xa_values(*xa_softmax(xa_lg), (0, 1))
    xa_lg = xa_logits((2, 3))
    chunk(COL_GATE_A)

    states = [[None] * HGRN_HEADS for _ in tiles]
    for h in heads:
        st = st_ref[h]
        for t in tiles:
            before = []
            for ci in chunks:
                before.append(st.astype(BF16))
                st = st * dec[t][ci][:, cols[h]] + u_t[t][h][:, ci * d:(ci + 1) * d]
            states[t][h] = before
        st_ref[h] = st
    xa_o += xa_values(*xa_softmax(xa_lg), (2, 3))

    odd = (lax.broadcasted_iota(jnp.int32, (2 * c, d), 0) >= c)
    o_inter = []
    for t in tiles:
        per_head = []
        for h in heads:
            outs = []
            for ci in range(0, n_chunks, 2):
                qp = q_t[t][ci * c:(ci + 2) * c, cols[h]]
                zero = jnp.zeros_like(qp)
                lhs = jnp.concatenate([jnp.where(odd, zero, qp), jnp.where(odd, qp, zero)], axis=1)
                st2 = jnp.concatenate([states[t][h][ci], states[t][h][ci + 1]], axis=1)
                outs.append(lax.dot_general(lhs, st2, nt_dims, preferred_element_type=F32))
            per_head.append(outs)
        o_inter.append(per_head)
    chunk(COL_GATE_B)
    for h in range(XA_HEADS):
        zc = qz[:, BRANCH_W + h * XA_HD:BRANCH_W + (h + 1) * XA_HD].astype(F32)
        yc_ref[:, xcols[h]] = (xa_o[h] * _silu(zc)).astype(yc_ref.dtype)
    for t in tiles:
        for h in heads:
            o = o_intra[t][h] + jnp.concatenate(o_inter[t][h], axis=0)
            on = o * lax.rsqrt(jnp.mean(o * o, axis=-1, keepdims=True) + EPS) * hnw_ref[...]
            yb_ref[rows[t], cols[h]] = (on * _silu(g_all[rows[t], cols[h]].astype(F32))).astype(yb_ref.dtype)
    chunk(COL_GATE_C)


def _in_proj(x2, norm_w, w_bf16, lb_logits, hgrn_norm_w, mem, mem_norm_w, wkv_bf16, s, tm=512):
    t, d = x2.shape
    n = w_bf16.shape[1]
    n_mem = mem.shape[1]
    spb = s // tm
    assert n == 8 * IN_PROJ_CHUNK and s % tm == 0 and tm % HGRN_TILE == 0
    r, c = HGRN_TILE, HGRN_CHUNK
    ri = np.arange(r)
    same = (ri[:, None] // c) == (ri[None, :] // c)
    tril = (same & (ri[None, :] <= ri[:, None])).astype(np.float32)
    const = lambda shape: pl.BlockSpec(shape, lambda i: (0,) * len(shape))
    return pl.pallas_call(
        functools.partial(_in_proj_body, steps_per_batch=spb),
        grid=(t // tm,),
        in_specs=[
            pl.BlockSpec((tm, d), lambda i: (i, 0)),
            const((1, d)),
            pl.BlockSpec((d, n), lambda i: (0, 0), pipeline_mode=pl.Buffered(1)),
            const((lb_logits.shape[0], BRANCH_W)),
            const((1, HGRN_D)),
            const((r, r)),
            pl.BlockSpec((1, n_mem, d), lambda i: (i // spb, 0, 0)),
            const((1, d)),
            pl.BlockSpec((d, 2 * BRANCH_W), lambda i: (0, 0), pipeline_mode=pl.Buffered(1)),
        ],
        out_specs=[pl.BlockSpec((tm, n), lambda i: (i, 0)), pl.BlockSpec((tm, BRANCH_W), lambda i: (i, 0)),
                   pl.BlockSpec((tm, BRANCH_W), lambda i: (i, 0))],
        out_shape=[jax.ShapeDtypeStruct((t, n), BF16), jax.ShapeDtypeStruct((t, BRANCH_W), BF16),
                   jax.ShapeDtypeStruct((t, BRANCH_W), BF16)],
        scratch_shapes=[pltpu.VMEM((HGRN_HEADS, HGRN_D, HGRN_D), F32),
                        pltpu.VMEM((n_mem, BRANCH_W), BF16), pltpu.VMEM((n_mem, BRANCH_W), BF16)],
        compiler_params=pltpu.CompilerParams(
            dimension_semantics=("arbitrary",), vmem_limit_bytes=VMEM_LIMIT),
        name="in_proj",
    )(x2, norm_w.reshape(1, d), w_bf16, lb_logits, hgrn_norm_w.reshape(1, HGRN_D), tril.astype(BF16),
      mem, mem_norm_w.reshape(1, d), wkv_bf16)


def _moba_constants(s):
    blk, hd, nbl = MOBA_BLOCK, MOBA_HD, MOBA_NBL
    row = np.arange(s)
    off = (row % blk).astype(np.float32)
    onehot = (row[:, None] // blk == np.arange(nbl)[None, :]).astype(np.float32)
    bias_lanes = np.concatenate([onehot, onehot, onehot, off[:, None], off[:, None], off[:, None],
                                 np.zeros((s, hd - 3 * nbl - 3), np.float32)], axis=1)
    zeros = np.zeros((s, hd), np.float32)
    a0 = np.concatenate([zeros, bias_lanes], axis=1)
    a1 = np.concatenate([bias_lanes, zeros], axis=1)
    aug = np.stack([a0, a1]).astype(BF16)
    avg = ((np.arange(nbl)[:, None] == (row // blk)[None, :]).astype(np.float32) / blk).astype(BF16)
    causal = np.where(np.arange(blk)[:, None] <= np.arange(blk)[None, :], 0.0, -np.inf).astype(np.float32)
    mask = np.concatenate([np.zeros((blk, blk), np.float32), causal])
    return aug, avg, mask


def _moba_body(slopes_ref, q_ref, qn_ref, k_ref, v_ref, z_ref, aug_ref, avg_ref, mask_ref, o_ref,
               ka_ref, vt_ref, kbp_ref, qa_ref, qan_ref, acc_ref, p_ref, m_ref, al_ref):
    s = k_ref.shape[0]
    nb = s // MOBA_BLOCK
    blk = MOBA_BLOCK
    nh = MOBA_HEADS
    hd = MOBA_HD
    nbl, gr = MOBA_NBL, MOBA_GATE_ROWS
    i = pl.program_id(1)

    def build_queries(src_ref, iq, dst_ref):
        nidx = lax.broadcasted_iota(jnp.int32, (nbl, blk), 0)
        nidx_f = nidx.astype(F32)
        qoff = lax.broadcasted_iota(jnp.int32, (nbl, blk), 1)
        dist0 = (qoff + (iq - nidx) * blk).astype(F32)
        qts = [(src_ref[:, hp * LANES:(hp + 1) * LANES].astype(F32) * (hd ** -0.5)).T
               for hp in range(nh // 2)]
        gates = [jnp.dot(kbp_ref[hp * 2 * gr:(hp + 1) * 2 * gr, :], qts[hp].astype(BF16),
                         preferred_element_type=F32) for hp in range(nh // 2)]
        for hp in range(nh // 2):
            qs = qts[hp] * LOG2E
            for e in range(2):
                h = 2 * hp + e
                gg = gates[hp][e * gr:(e + 1) * gr]
                g = gg[0:nbl] + gg[nbl:2 * nbl] + gg[2 * nbl:3 * nbl]
                gv = jnp.where(nidx < iq, g, -jnp.inf)
                sel = nidx == iq
                for _ in range(MOBA_TOPK):
                    mx = jnp.max(gv, axis=0, keepdims=True)
                    first = jnp.min(jnp.where(gv == mx, nidx_f, float(nbl)), axis=0, keepdims=True)
                    pick = nidx_f == jnp.where(mx > -jnp.inf, first, float(nbl))
                    sel = sel | pick
                    gv = jnp.where(pick, -jnp.inf, gv)
                sl = slopes_ref[h] * LOG2E
                bias = jnp.where(sel, -sl * dist0, NEG_BIG)
                b_hi, b_mid, b_lo = _split3(bias)
                s_hi, s_mid, s_lo = _split3(jnp.full((nbl, blk), sl, F32))
                srow = jnp.where(nidx == 0, s_hi, jnp.where(nidx == 1, s_mid, jnp.where(nidx == 2, s_lo, 0.0)))
                qh = qs[0:hd] if e == 0 else qs[hd:2 * hd]
                parts = [qh, b_hi, b_mid, b_lo, srow] if e == 0 else [b_hi, b_mid, b_lo, srow, qh]
                dst_ref[h] = jnp.concatenate(parts, axis=0).astype(BF16)

    @pl.when(i == 0)
    def _():
        lane = lax.broadcasted_iota(jnp.int32, (1, LANES), 1)
        first_half = lane < hd
        kbar = jnp.dot(avg_ref[...], k_ref[...], preferred_element_type=F32)
        for h in range(nh):
            hp, e = h // 2, h % 2
            cols = slice(hp * LANES, (hp + 1) * LANES)
            keep = first_half if e == 0 else jnp.logical_not(first_half)
            for j in range(nb):
                rows = slice(j * blk, (j + 1) * blk)
                ka_ref[h, j] = jnp.where(keep, k_ref[rows, cols], aug_ref[e, rows, :])
            pieces = _split3(jnp.where(keep, kbar[:, cols], 0.0))
            for p in range(3):
                kbp_ref[h * gr + p * nbl:h * gr + (p + 1) * nbl, :] = pieces[p].astype(BF16)
        ones_row = jnp.where(lax.broadcasted_iota(jnp.int32, (MOBA_VROWS - hd, blk), 0) == 0, 1.0, 0.0)
        for j in range(nb):
            vt = v_ref[j * blk:(j + 1) * blk, :].astype(F32).T
            for h in range(nh):
                vt_ref[j, h, 0:hd, :] = vt[h * hd:(h + 1) * hd].astype(BF16)
                vt_ref[j, h, hd:MOBA_VROWS, :] = ones_row.astype(BF16)
        build_queries(q_ref, 0, qa_ref)
        p_ref[...] = jnp.zeros(p_ref.shape, BF16)
        al_ref[...] = jnp.ones(al_ref.shape, F32)

    @pl.when(i > 0)
    def _():
        qa_ref[...] = qan_ref[...]

    m_ref[...] = jnp.full(m_ref.shape, -jnp.inf, F32)
    acc_ref[...] = jnp.zeros(acc_ref.shape, F32)

    def softmax(h, sT):
        m_old = m_ref[h]
        m_new = jnp.maximum(m_old, jnp.max(sT, axis=0, keepdims=True))
        m_ref[h] = m_new
        return jnp.exp2(sT - m_new).astype(BF16), jnp.exp2(m_old - m_new)

    def pair_scores(j0):
        return [jnp.dot(ka_ref[h, pl.ds(j0, 2)].reshape(2 * blk, LANES), qa_ref[h],
                        preferred_element_type=F32) for h in range(nh)]

    def pair_values(ja, jb, h):
        return jnp.concatenate([vt_ref[ja, h], vt_ref[jb, h]], axis=1)

    def pending_pv(j0):
        pa, pb = jnp.maximum(j0 - 2, 0), jnp.maximum(j0 - 1, 0)
        return [jnp.dot(pair_values(pa, pb, h), p_ref[h], preferred_element_type=F32) for h in range(nh)]

    def past(k, carry):
        scores = pair_scores(2 * k)
        pv = pending_pv(2 * k)
        sm = [softmax(h, scores[h]) for h in range(nh)]
        for h in range(nh):
            acc_ref[h] = al_ref[h] * acc_ref[h] + pv[h]
        for h in range(nh):
            p_ref[h], al_ref[h] = sm[h]
        return carry

    npairs = i // 2
    lax.fori_loop(0, npairs, past, 0)

    def finish(scores, last_values):
        pv = pending_pv(2 * npairs)
        build_queries(qn_ref, i + 1, qan_ref)
        sm = [softmax(h, scores[h]) for h in range(nh)]
        last = [jnp.dot(last_values(h), sm[h][0], preferred_element_type=F32) for h in range(nh)]
        outs = []
        for h in range(nh):
            acc = sm[h][1] * (al_ref[h] * acc_ref[h] + pv[h]) + last[h]
            outs.append(acc[0:hd] / acc[hd:hd + 1])
        ot = jnp.concatenate(outs, axis=0)
        o_ref[...] = (ot.T * _silu(z_ref[...].astype(F32))).astype(o_ref.dtype)

    @pl.when(i % 2 == 0)
    def _():
        finish([jnp.dot(ka_ref[h, i], qa_ref[h], preferred_element_type=F32) + mask_ref[blk:2 * blk, :]
                for h in range(nh)], lambda h: vt_ref[i, h])

    @pl.when(i % 2 == 1)
    def _():
        scores = pair_scores(i - 1)
        finish([scores[h] + mask_ref[...] for h in range(nh)], lambda h: pair_values(i - 1, i, h))


def _moba(proj, slopes, bsz, s):
    nb = s // MOBA_BLOCK
    assert nb <= MOBA_NBL and 2 * MOBA_HD == LANES and 3 * MOBA_NBL + 3 <= MOBA_HD
    blk = MOBA_BLOCK
    aug, avg, mask = _moba_constants(s)
    qblock = lambda c0: (lambda b, i, *_: (b * nb + i, c0 // BRANCH_W))
    qnext = lambda b, i, *_: (b * nb + jnp.minimum(i + 1, nb - 1), COL_QA // BRANCH_W)
    whole = lambda c0: (lambda b, i, *_: (b, c0 // BRANCH_W))
    return pl.pallas_call(
        _moba_body,
        grid_spec=pltpu.PrefetchScalarGridSpec(
            num_scalar_prefetch=1,
            grid=(bsz, nb),
            in_specs=[
                pl.BlockSpec((blk, BRANCH_W), qblock(COL_QA)),
                pl.BlockSpec((blk, BRANCH_W), qnext),
                pl.BlockSpec((s, BRANCH_W), whole(COL_KA)),
                pl.BlockSpec((s, BRANCH_W), whole(COL_VA)),
                pl.BlockSpec((blk, BRANCH_W), qblock(COL_ZA)),
                pl.BlockSpec((2, s, LANES), lambda b, i, *_: (0, 0, 0)),
                pl.BlockSpec((MOBA_NBL, s), lambda b, i, *_: (0, 0)),
                pl.BlockSpec((2 * blk, blk), lambda b, i, *_: (0, 0)),
            ],
            out_specs=pl.BlockSpec((blk, BRANCH_W), lambda b, i, *_: (b * nb + i, 0)),
            scratch_shapes=[
                pltpu.VMEM((MOBA_HEADS, nb, blk, LANES), BF16),
                pltpu.VMEM((nb, MOBA_HEADS, MOBA_VROWS, blk), BF16),
                pltpu.VMEM((MOBA_HEADS * MOBA_GATE_ROWS, LANES), BF16),
                pltpu.VMEM((MOBA_HEADS, LANES, blk), BF16),
                pltpu.VMEM((MOBA_HEADS, LANES, blk), BF16),
                pltpu.VMEM((MOBA_HEADS, MOBA_VROWS, blk), F32),
                pltpu.VMEM((MOBA_HEADS, 2 * blk, blk), BF16),
                pltpu.VMEM((MOBA_HEADS, 1, blk), F32),
                pltpu.VMEM((MOBA_HEADS, 1, blk), F32),
            ],
        ),
        out_shape=jax.ShapeDtypeStruct((bsz * s, BRANCH_W), BF16),
        compiler_params=pltpu.CompilerParams(
            dimension_semantics=("parallel", "arbitrary"), vmem_limit_bytes=VMEM_LIMIT),
        name="moba",
    )(slopes, proj, proj, proj, proj, proj, aug, avg, mask)


def _merge_body(x_ref, ga_ref, gb_ref, gc_ref, ya_ref, yb_ref, yc_ref,
                wa_ref, wb_ref, wc_ref, wo_ref, pw_ref, o_ref):
    tm = x_ref.shape[0]
    halves = [slice(0, tm // 2), slice(tm // 2, tm)]
    branches = ((ga_ref, ya_ref, wa_ref), (gb_ref, yb_ref, wb_ref), (gc_ref, yc_ref, wc_ref))
    proj = [[jnp.dot(y_ref[rows, :], w_ref[...], preferred_element_type=F32) for _, y_ref, w_ref in branches]
            for rows in halves]
    gated = [[_sigmoid(g_ref[rows, :].astype(F32)) * proj[k][n] for n, (g_ref, _, _) in enumerate(branches)]
             for k, rows in enumerate(halves)]
    merged = [gated[k][0] + gated[k][1] + gated[k][2] for k in range(2)]
    y = [jnp.dot(merged[k].astype(BF16), wo_ref[...], preferred_element_type=F32) for k in range(2)]
    for k, rows in enumerate(halves):
        yn = y[k] * lax.rsqrt(jnp.mean(y[k] * y[k], axis=-1, keepdims=True) + EPS) * pw_ref[...]
        o_ref[rows, :] = x_ref[rows, :] + yn


def _merge(x2, proj, ya, yb, yc, wa, wb, wc, wo, post_w, tm=1024):
    t, d = x2.shape
    row = lambda c: (lambda i: (i, c))
    const = lambda shape: pl.BlockSpec(shape, lambda i: (0, 0))
    return pl.pallas_call(
        _merge_body,
        grid=(t // tm,),
        in_specs=[
            pl.BlockSpec((tm, d), row(0)),
            pl.BlockSpec((tm, d), row(COL_GATE_A // d)),
            pl.BlockSpec((tm, d), row(COL_GATE_B // d)),
            pl.BlockSpec((tm, d), row(COL_GATE_C // d)),
            pl.BlockSpec((tm, BRANCH_W), row(0)),
            pl.BlockSpec((tm, BRANCH_W), row(0)),
            pl.BlockSpec((tm, BRANCH_W), row(0)),
            const((BRANCH_W, d)), const((BRANCH_W, d)), const((BRANCH_W, d)),
            const((d, d)), const((1, d)),
        ],
        out_specs=pl.BlockSpec((tm, d), row(0)),
        out_shape=jax.ShapeDtypeStruct((t, d), F32),
        compiler_params=pltpu.CompilerParams(
            dimension_semantics=("parallel",), vmem_limit_bytes=VMEM_LIMIT),
        name="merge",
    )(x2, proj, proj, proj, ya, yb, yc, wa, wb, wc, wo, post_w.reshape(1, d))


def kernel(x, mem, pre_norm_w, w_in, hgrn_lb_logits, hgrn_norm_w, mem_norm_w, w_mem_kv,
           w_branch_a, w_branch_b, w_branch_c, w_out, post_norm_w):
    bsz, s, d = x.shape
    assert w_in.shape[0] == 1 and w_in.shape[2] == PROJ_TOTAL and d == 1024
    assert s % MOBA_BLOCK == 0
    x2 = x.reshape(bsz * s, d)
    proj, yb, yc = _in_proj(x2, pre_norm_w[0], w_in[0].astype(BF16), hgrn_lb_logits, hgrn_norm_w[0],
                            mem, mem_norm_w[0], w_mem_kv[0].astype(BF16), s)
    slopes = jnp.exp2(-8.0 * jnp.arange(1, MOBA_HEADS + 1, dtype=F32) / MOBA_HEADS)
    ya = _moba(proj, slopes, bsz, s)
    out = _merge(x2, proj, ya, yb, yc, w_branch_a[0].astype(BF16), w_branch_b[0].astype(BF16),
                 w_branch_c[0].astype(BF16), w_out[0].astype(BF16), post_norm_w[0])
    return out.reshape(bsz, s, d)
```

```python
import functools

import jax
import jax.numpy as jnp
import numpy as np
from jax import lax
from jax.experimental import pallas as pl
from jax.experimental.pallas import tpu as pltpu

F32 = jnp.float32
BF16 = jnp.bfloat16
EPS = 1e-6

MOBA_HEADS, MOBA_HD, MOBA_BLOCK, MOBA_TOPK = 8, 64, 256, 3
HGRN_HEADS, HGRN_D, HGRN_CHUNK = 4, 128, 32
XA_HEADS, XA_HD = 4, 128
BRANCH_W = 512
LANES = 128
COL_QA, COL_KA, COL_VA, COL_ZA = 0, 512, 1024, 1536
COL_FB, COL_IB, COL_QB, COL_GB = 2048, 2560, 3072, 3584
COL_QC, COL_ZC = 4096, 4608
COL_GATE_A, COL_GATE_B, COL_GATE_C = 5120, 6144, 7168
PROJ_TOTAL = COL_GATE_C + 1024
OUT_GATE_A, OUT_GATE_B, OUT_GATE_C = 2048, 3072, 4096
OUT_TOTAL = OUT_GATE_C + 1024

NEG_BIG = -1e30
LOG2E = 1.4426950408889634
MOBA_VROWS = MOBA_HD + 16
MOBA_NBL = 16
MOBA_GATE_ROWS = 3 * MOBA_NBL
VMEM_LIMIT = 56 * 1024 * 1024


def _sigmoid(z):
    return 1.0 / (1.0 + jnp.exp2(z * (-LOG2E)))


def _silu(z):
    return z * _sigmoid(z)


def _split3(a):
    hi = a.astype(BF16).astype(F32)
    r = a - hi
    mid = r.astype(BF16).astype(F32)
    lo = (r - mid).astype(BF16).astype(F32)
    return hi, mid, lo


IN_PROJ_CHUNK = 1024
HGRN_TILE = 256


def _in_proj_body(x_ref, nw_ref, w_ref, lbl_ref, hnw_ref, lmat_ref, mem_ref, mw_ref, wkv_ref,
                  o_ref, yb_ref, yc_ref, st_ref, km_ref, vm_ref, *, steps_per_batch):
    @pl.when(pl.program_id(0) % steps_per_batch == 0)
    def _():
        st_ref[...] = jnp.zeros_like(st_ref)
        m = mem_ref[0]
        mn = m * lax.rsqrt(jnp.mean(m * m, axis=-1, keepdims=True) + EPS) * mw_ref[...]
        kv = jnp.dot(mn.astype(BF16), wkv_ref[...], preferred_element_type=F32)
        km_ref[...] = kv[:, :BRANCH_W].astype(BF16)
        vm_ref[...] = kv[:, BRANCH_W:].astype(BF16)

    x = x_ref[...]
    ms = jnp.mean(x * x, axis=-1, keepdims=True)
    hx = (x * lax.rsqrt(ms + EPS) * nw_ref[...]).astype(BF16)

    def chunk(c0, out0=None):
        out = jnp.dot(hx, w_ref[:, c0:c0 + IN_PROJ_CHUNK], preferred_element_type=F32).astype(o_ref.dtype)
        if out0 is not None:
            o_ref[:, out0:out0 + IN_PROJ_CHUNK] = out
        return out

    fi = chunk(COL_FB)
    qg = chunk(COL_QB)
    f_all, i_all = fi[:, :BRANCH_W], fi[:, BRANCH_W:]
    q_all, g_all = qg[:, :BRANCH_W], qg[:, BRANCH_W:]

    r, c, d = HGRN_TILE, HGRN_CHUNK, HGRN_D
    n_chunks = r // c
    heads = range(HGRN_HEADS)
    tiles = range(x_ref.shape[0] // r)
    chunks = range(n_chunks)
    cols = [slice(h * d, (h + 1) * d) for h in heads]
    rows = [slice(t * r, (t + 1) * r) for t in tiles]
    nt_dims = (((1,), (1,)), ((), ()))

    logits = lbl_ref[...]
    ex = jnp.exp(logits - jnp.max(logits, axis=0, keepdims=True))
    lb = ex[0:1, :] / jnp.sum(ex, axis=0, keepdims=True)

    kk, pieces = [], []
    for t in tiles:
        fl = f_all[rows[t], :].astype(F32)
        log_f = jnp.log(lb + (1.0 - lb) * _sigmoid(fl))
        kk.append((1.0 - lb) * _sigmoid(-fl))
        pieces.append([p.astype(BF16) for p in _split3(log_f)])
    qz = chunk(COL_QC)
    xcols = [slice(h * XA_HD, (h + 1) * XA_HD) for h in range(XA_HEADS)]

    def xa_logits(hs):
        return [lax.dot_general(qz[:, xcols[h]], km_ref[:, xcols[h]], nt_dims,
                                preferred_element_type=F32) * (XA_HD ** -0.5 * LOG2E) for h in hs]

    def xa_softmax(logits):
        ps, dens = [], []
        for lg in logits:
            p = jnp.exp2(lg - jnp.max(lg, axis=-1, keepdims=True))
            dens.append(jnp.sum(p, axis=-1, keepdims=True))
            ps.append(p.astype(BF16))
        return ps, dens

    def xa_values(ps, dens, hs):
        return [jnp.dot(p, vm_ref[:, xcols[h]], preferred_element_type=F32) / den
                for p, den, h in zip(ps, dens, hs)]

    b = []
    for t in tiles:
        lmat = lmat_ref[...]
        hi, mid, lo = [jnp.dot(lmat, p, preferred_element_type=F32) for p in pieces[t]]
        b.append(hi + mid + lo)
    chunk(COL_QA, COL_QA)

    q_t, k_t32, k_t, dec = [], [], [], []
    for t in tiles:
        q_t.append((q_all[rows[t], :].astype(F32) * jnp.exp(b[t])).astype(BF16))
        kt = kk[t] * jnp.exp(-b[t])
        k_t32.append(kt)
        k_t.append(kt.astype(BF16))
        dec.append([jnp.exp(b[t][ci * c + c - 1:ci * c + c, :]) for ci in chunks])

    a = [[lax.dot_general(q_t[t][:, cols[h]], k_t[t][:, cols[h]], nt_dims, preferred_element_type=F32)
          for h in heads] for t in tiles]
    u_t = []
    for t in tiles:
        per_head = []
        for h in heads:
            vt = i_all[rows[t], cols[h]].astype(F32).T.astype(BF16)
            blocks = []
            for ci in chunks:
                slab = (k_t32[t][ci * c:(ci + 1) * c, cols[h]] * dec[t][ci][:, cols[h]]).astype(BF16)
                parts = ([jnp.zeros((c, ci * d), BF16)] if ci else []) + [slab]
                if ci < n_chunks - 1:
                    parts.append(jnp.zeros((c, (n_chunks - 1 - ci) * d), BF16))
                blocks.append(jnp.concatenate(parts, axis=1))
            kd_blk = jnp.concatenate(blocks, axis=0)
            per_head.append(jnp.dot(vt, kd_blk, preferred_element_type=F32))
        u_t.append(per_head)
    xa_lg = xa_logits((0, 1))
    chunk(COL_VA, COL_VA)
    o_intra = [[jnp.dot(a[t][h].astype(BF16) * lmat_ref[...], i_all[rows[t], cols[h]],
                        preferred_element_type=F32) for h in heads] for t in tiles]
    xa_o = xa_values(*xa_softmax(xa_lg), (0, 1))
    xa_lg = xa_logits((2, 3))
    chunk(COL_GATE_A, OUT_GATE_A)

    states = [[None] * HGRN_HEADS for _ in tiles]
    for h in heads:
        st = st_ref[h]
        for t in tiles:
            before = []
            for ci in chunks:
                before.append(st.astype(BF16))
                st = st * dec[t][ci][:, cols[h]] + u_t[t][h][:, ci * d:(ci + 1) * d]
            states[t][h] = before
        st_ref[h] = st
    xa_o += xa_values(*xa_softmax(xa_lg), (2, 3))
    chunk(COL_GATE_B, OUT_GATE_B)

    odd = (lax.broadcasted_iota(jnp.int32, (2 * c, d), 0) >= c)
    o_inter = []
    for t in tiles:
        per_head = []
        for h in heads:
            outs = []
            for ci in range(0, n_chunks, 2):
                qp = q_t[t][ci * c:(ci + 2) * c, cols[h]]
                zero = jnp.zeros_like(qp)
                lhs = jnp.concatenate([jnp.where(odd, zero, qp), jnp.where(odd, qp, zero)], axis=1)
                st2 = jnp.concatenate([states[t][h][ci], states[t][h][ci + 1]], axis=1)
                outs.append(lax.dot_general(lhs, st2, nt_dims, preferred_element_type=F32))
            per_head.append(outs)
        o_inter.append(per_head)
    for h in range(XA_HEADS):
        zc = qz[:, BRANCH_W + h * XA_HD:BRANCH_W + (h + 1) * XA_HD].astype(F32)
        yc_ref[:, xcols[h]] = (xa_o[h] * _silu(zc)).astype(yc_ref.dtype)
    chunk(COL_GATE_C, OUT_GATE_C)
    for t in tiles:
        for h in heads:
            o = o_intra[t][h] + jnp.concatenate(o_inter[t][h], axis=0)
            on = o * lax.rsqrt(jnp.mean(o * o, axis=-1, keepdims=True) + EPS) * hnw_ref[...]
            yb_ref[rows[t], cols[h]] = (on * _silu(g_all[rows[t], cols[h]].astype(F32))).astype(yb_ref.dtype)


def _in_proj(x2, norm_w, w_bf16, lb_logits, hgrn_norm_w, mem, mem_norm_w, wkv_bf16, s, tm=512):
    t, d = x2.shape
    n = w_bf16.shape[1]
    n_mem = mem.shape[1]
    spb = s // tm
    assert n == 8 * IN_PROJ_CHUNK and s % tm == 0 and tm % HGRN_TILE == 0
    r, c = HGRN_TILE, HGRN_CHUNK
    ri = np.arange(r)
    same = (ri[:, None] // c) == (ri[None, :] // c)
    tril = (same & (ri[None, :] <= ri[:, None])).astype(np.float32)
    const = lambda shape: pl.BlockSpec(shape, lambda i: (0,) * len(shape))
    return pl.pallas_call(
        functools.partial(_in_proj_body, steps_per_batch=spb),
        grid=(t // tm,),
        in_specs=[
            pl.BlockSpec((tm, d), lambda i: (i, 0)),
            const((1, d)),
            pl.BlockSpec((d, n), lambda i: (0, 0), pipeline_mode=pl.Buffered(1)),
            const((lb_logits.shape[0], BRANCH_W)),
            const((1, HGRN_D)),
            const((r, r)),
            pl.BlockSpec((1, n_mem, d), lambda i: (i // spb, 0, 0)),
            const((1, d)),
            pl.BlockSpec((d, 2 * BRANCH_W), lambda i: (0, 0), pipeline_mode=pl.Buffered(1)),
        ],
        out_specs=[pl.BlockSpec((tm, OUT_TOTAL), lambda i: (i, 0)), pl.BlockSpec((tm, BRANCH_W), lambda i: (i, 0)),
                   pl.BlockSpec((tm, BRANCH_W), lambda i: (i, 0))],
        out_shape=[jax.ShapeDtypeStruct((t, OUT_TOTAL), BF16), jax.ShapeDtypeStruct((t, BRANCH_W), BF16),
                   jax.ShapeDtypeStruct((t, BRANCH_W), BF16)],
        scratch_shapes=[pltpu.VMEM((HGRN_HEADS, HGRN_D, HGRN_D), F32),
                        pltpu.VMEM((n_mem, BRANCH_W), BF16), pltpu.VMEM((n_mem, BRANCH_W), BF16)],
        compiler_params=pltpu.CompilerParams(
            dimension_semantics=("arbitrary",), vmem_limit_bytes=VMEM_LIMIT),
        name="in_proj",
    )(x2, norm_w.reshape(1, d), w_bf16, lb_logits, hgrn_norm_w.reshape(1, HGRN_D), tril.astype(BF16),
      mem, mem_norm_w.reshape(1, d), wkv_bf16)


def _moba_constants(s):
    blk, hd, nbl = MOBA_BLOCK, MOBA_HD, MOBA_NBL
    row = np.arange(s)
    off = (row % blk).astype(np.float32)
    onehot = (row[:, None] // blk == np.arange(nbl)[None, :]).astype(np.float32)
    bias_lanes = np.concatenate([onehot, onehot, onehot, off[:, None], off[:, None], off[:, None],
                                 np.zeros((s, hd - 3 * nbl - 3), np.float32)], axis=1)
    zeros = np.zeros((s, hd), np.float32)
    a0 = np.concatenate([zeros, bias_lanes], axis=1)
    a1 = np.concatenate([bias_lanes, zeros], axis=1)
    aug = np.stack([a0, a1]).astype(BF16)
    avg = ((np.arange(nbl)[:, None] == (row // blk)[None, :]).astype(np.float32) / blk).astype(BF16)
    causal = np.where(np.arange(blk)[:, None] <= np.arange(blk)[None, :], 0.0, -np.inf).astype(np.float32)
    mask = np.concatenate([np.zeros((blk, blk), np.float32), causal])
    return aug, avg, mask


def _moba_body(slopes_ref, q_ref, qn_ref, k_ref, v_ref, z_ref, aug_ref, avg_ref, mask_ref, o_ref,
               ka_ref, vt_ref, kbp_ref, qa_ref, qan_ref, acc_ref, p_ref, m_ref, al_ref):
    s = k_ref.shape[0]
    nb = s // MOBA_BLOCK
    blk = MOBA_BLOCK
    nh = MOBA_HEADS
    hd = MOBA_HD
    nbl, gr = MOBA_NBL, MOBA_GATE_ROWS
    i = pl.program_id(1)

    def build_queries(src_ref, iq, dst_ref):
        nidx = lax.broadcasted_iota(jnp.int32, (nbl, blk), 0)
        nidx_f = nidx.astype(F32)
        qoff = lax.broadcasted_iota(jnp.int32, (nbl, blk), 1)
        dist0 = (qoff + (iq - nidx) * blk).astype(F32)
        qts = [(src_ref[:, hp * LANES:(hp + 1) * LANES].astype(F32) * (hd ** -0.5)).T
               for hp in range(nh // 2)]
        gates = [jnp.dot(kbp_ref[hp * 2 * gr:(hp + 1) * 2 * gr, :], qts[hp].astype(BF16),
                         preferred_element_type=F32) for hp in range(nh // 2)]
        for hp in range(nh // 2):
            qs = qts[hp] * LOG2E
            for e in range(2):
                h = 2 * hp + e
                gg = gates[hp][e * gr:(e + 1) * gr]
                g = gg[0:nbl] + gg[nbl:2 * nbl] + gg[2 * nbl:3 * nbl]
                gv = jnp.where(nidx < iq, g, -jnp.inf)
                sel = nidx == iq
                for _ in range(MOBA_TOPK):
                    mx = jnp.max(gv, axis=0, keepdims=True)
                    first = jnp.min(jnp.where(gv == mx, nidx_f, float(nbl)), axis=0, keepdims=True)
                    pick = nidx_f == jnp.where(mx > -jnp.inf, first, float(nbl))
                    sel = sel | pick
                    gv = jnp.where(pick, -jnp.inf, gv)
                sl = slopes_ref[h] * LOG2E
                bias = jnp.where(sel, -sl * dist0, NEG_BIG)
                b_hi, b_mid, b_lo = _split3(bias)
                s_hi, s_mid, s_lo = _split3(jnp.full((nbl, blk), sl, F32))
                srow = jnp.where(nidx == 0, s_hi, jnp.where(nidx == 1, s_mid, jnp.where(nidx == 2, s_lo, 0.0)))
                qh = qs[0:hd] if e == 0 else qs[hd:2 * hd]
                parts = [qh, b_hi, b_mid, b_lo, srow] if e == 0 else [b_hi, b_mid, b_lo, srow, qh]
                dst_ref[h] = jnp.concatenate(parts, axis=0).astype(BF16)

    @pl.when(i == 0)
    def _():
        lane = lax.broadcasted_iota(jnp.int32, (1, LANES), 1)
        first_half = lane < hd
        kbar = jnp.dot(avg_ref[...], k_ref[...], preferred_element_type=F32)
        for h in range(nh):
            hp, e = h // 2, h % 2
            cols = slice(hp * LANES, (hp + 1) * LANES)
            keep = first_half if e == 0 else jnp.logical_not(first_half)
            for j in range(nb):
                rows = slice(j * blk, (j + 1) * blk)
                ka_ref[h, j] = jnp.where(keep, k_ref[rows, cols], aug_ref[e, rows, :])
            pieces = _split3(jnp.where(keep, kbar[:, cols], 0.0))
            for p in range(3):
                kbp_ref[h * gr + p * nbl:h * gr + (p + 1) * nbl, :] = pieces[p].astype(BF16)
        ones_row = jnp.where(lax.broadcasted_iota(jnp.int32, (MOBA_VROWS - hd, blk), 0) == 0, 1.0, 0.0)
        for j in range(nb):
            vt = v_ref[j * blk:(j + 1) * blk, :].astype(F32).T
            for h in range(nh):
                vt_ref[j, h, 0:hd, :] = vt[h * hd:(h + 1) * hd].astype(BF16)
                vt_ref[j, h, hd:MOBA_VROWS, :] = ones_row.astype(BF16)
        build_queries(q_ref, 0, qa_ref)
        p_ref[...] = jnp.zeros(p_ref.shape, BF16)
        al_ref[...] = jnp.ones(al_ref.shape, F32)

    @pl.when(i > 0)
    def _():
        qa_ref[...] = qan_ref[...]

    m_ref[...] = jnp.full(m_ref.shape, -jnp.inf, F32)
    acc_ref[...] = jnp.zeros(acc_ref.shape, F32)

    def softmax(h, sT):
        m_old = m_ref[h]
        m_new = jnp.maximum(m_old, jnp.max(sT, axis=0, keepdims=True))
        m_ref[h] = m_new
        return jnp.exp2(sT - m_new).astype(BF16), jnp.exp2(m_old - m_new)

    def pair_scores(j0):
        return [jnp.dot(ka_ref[h, pl.ds(j0, 2)].reshape(2 * blk, LANES), qa_ref[h],
                        preferred_element_type=F32) for h in range(nh)]

    def pair_values(ja, jb, h):
        return jnp.concatenate([vt_ref[ja, h], vt_ref[jb, h]], axis=1)

    def pending_pv(j0):
        pa, pb = jnp.maximum(j0 - 2, 0), jnp.maximum(j0 - 1, 0)
        return [jnp.dot(pair_values(pa, pb, h), p_ref[h], preferred_element_type=F32) for h in range(nh)]

    def past(k, carry):
        scores = pair_scores(2 * k)
        pv = pending_pv(2 * k)
        sm = [softmax(h, scores[h]) for h in range(nh)]
        for h in range(nh):
            acc_ref[h] = al_ref[h] * acc_ref[h] + pv[h]
        for h in range(nh):
            p_ref[h], al_ref[h] = sm[h]
        return carry

    npairs = i // 2
    lax.fori_loop(0, npairs, past, 0)

    def finish(scores, last_values):
        pv = pending_pv(2 * npairs)
        build_queries(qn_ref, i + 1, qan_ref)
        sm = [softmax(h, scores[h]) for h in range(nh)]
        last = [jnp.dot(last_values(h), sm[h][0], preferred_element_type=F32) for h in range(nh)]
        outs = []
        for h in range(nh):
            acc = sm[h][1] * (al_ref[h] * acc_ref[h] + pv[h]) + last[h]
            outs.append(acc[0:hd] / acc[hd:hd + 1])
        ot = jnp.concatenate(outs, axis=0)
        o_ref[...] = (ot.T * _silu(z_ref[...].astype(F32))).astype(o_ref.dtype)

    @pl.when(i % 2 == 0)
    def _():
        finish([jnp.dot(ka_ref[h, i], qa_ref[h], preferred_element_type=F32) + mask_ref[blk:2 * blk, :]
                for h in range(nh)], lambda h: vt_ref[i, h])

    @pl.when(i % 2 == 1)
    def _():
        scores = pair_scores(i - 1)
        finish([scores[h] + mask_ref[...] for h in range(nh)], lambda h: pair_values(i - 1, i, h))


def _moba(proj, slopes, bsz, s):
    nb = s // MOBA_BLOCK
    assert nb <= MOBA_NBL and 2 * MOBA_HD == LANES and 3 * MOBA_NBL + 3 <= MOBA_HD
    blk = MOBA_BLOCK
    aug, avg, mask = _moba_constants(s)
    qblock = lambda c0: (lambda b, i, *_: (b * nb + i, c0 // BRANCH_W))
    qnext = lambda b, i, *_: (b * nb + jnp.minimum(i + 1, nb - 1), COL_QA // BRANCH_W)
    whole = lambda c0: (lambda b, i, *_: (b, c0 // BRANCH_W))
    return pl.pallas_call(
        _moba_body,
        grid_spec=pltpu.PrefetchScalarGridSpec(
            num_scalar_prefetch=1,
            grid=(bsz, nb),
            in_specs=[
                pl.BlockSpec((blk, BRANCH_W), qblock(COL_QA)),
                pl.BlockSpec((blk, BRANCH_W), qnext),
                pl.BlockSpec((s, BRANCH_W), whole(COL_KA)),
                pl.BlockSpec((s, BRANCH_W), whole(COL_VA)),
                pl.BlockSpec((blk, BRANCH_W), qblock(COL_ZA)),
                pl.BlockSpec((2, s, LANES), lambda b, i, *_: (0, 0, 0)),
                pl.BlockSpec((MOBA_NBL, s), lambda b, i, *_: (0, 0)),
                pl.BlockSpec((2 * blk, blk), lambda b, i, *_: (0, 0)),
            ],
            out_specs=pl.BlockSpec((blk, BRANCH_W), lambda b, i, *_: (b * nb + i, 0)),
            scratch_shapes=[
                pltpu.VMEM((MOBA_HEADS, nb, blk, LANES), BF16),
                pltpu.VMEM((nb, MOBA_HEADS, MOBA_VROWS, blk), BF16),
                pltpu.VMEM((MOBA_HEADS * MOBA_GATE_ROWS, LANES), BF16),
                pltpu.VMEM((MOBA_HEADS, LANES, blk), BF16),
                pltpu.VMEM((MOBA_HEADS, LANES, blk), BF16),
                pltpu.VMEM((MOBA_HEADS, MOBA_VROWS, blk), F32),
                pltpu.VMEM((MOBA_HEADS, 2 * blk, blk), BF16),
                pltpu.VMEM((MOBA_HEADS, 1, blk), F32),
                pltpu.VMEM((MOBA_HEADS, 1, blk), F32),
            ],
        ),
        out_shape=jax.ShapeDtypeStruct((bsz * s, BRANCH_W), BF16),
        compiler_params=pltpu.CompilerParams(
            dimension_semantics=("parallel", "arbitrary"), vmem_limit_bytes=VMEM_LIMIT),
        name="moba",
    )(slopes, proj, proj, proj, proj, proj, aug, avg, mask)


def _merge_body(x_ref, ga_ref, gb_ref, gc_ref, ya_ref, yb_ref, yc_ref,
                wa_ref, wb_ref, wc_ref, wo_ref, pw_ref, o_ref):
    tm = x_ref.shape[0]
    halves = [slice(0, tm // 2), slice(tm // 2, tm)]
    branches = ((ga_ref, ya_ref, wa_ref), (gb_ref, yb_ref, wb_ref), (gc_ref, yc_ref, wc_ref))
    proj = [[jnp.dot(y_ref[rows, :], w_ref[...], preferred_element_type=F32) for _, y_ref, w_ref in branches]
            for rows in halves]
    gated = [[_sigmoid(g_ref[rows, :].astype(F32)) * proj[k][n] for n, (g_ref, _, _) in enumerate(branches)]
             for k, rows in enumerate(halves)]
    merged = [gated[k][0] + gated[k][1] + gated[k][2] for k in range(2)]
    y = [jnp.dot(merged[k].astype(BF16), wo_ref[...], preferred_element_type=F32) for k in range(2)]
    for k, rows in enumerate(halves):
        yn = y[k] * lax.rsqrt(jnp.mean(y[k] * y[k], axis=-1, keepdims=True) + EPS) * pw_ref[...]
        o_ref[rows, :] = x_ref[rows, :] + yn


def _merge(x2, proj, ya, yb, yc, wa, wb, wc, wo, post_w, tm=1024):
    t, d = x2.shape
    row = lambda c: (lambda i: (i, c))
    const = lambda shape: pl.BlockSpec(shape, lambda i: (0, 0))
    return pl.pallas_call(
        _merge_body,
        grid=(t // tm,),
        in_specs=[
            pl.BlockSpec((tm, d), row(0)),
            pl.BlockSpec((tm, d), row(OUT_GATE_A // d)),
            pl.BlockSpec((tm, d), row(OUT_GATE_B // d)),
            pl.BlockSpec((tm, d), row(OUT_GATE_C // d)),
            pl.BlockSpec((tm, BRANCH_W), row(0)),
            pl.BlockSpec((tm, BRANCH_W), row(0)),
            pl.BlockSpec((tm, BRANCH_W), row(0)),
            const((BRANCH_W, d)), const((BRANCH_W, d)), const((BRANCH_W, d)),
            const((d, d)), const((1, d)),
        ],
        out_specs=pl.BlockSpec((tm, d), row(0)),
        out_shape=jax.ShapeDtypeStruct((t, d), F32),
        compiler_params=pltpu.CompilerParams(
            dimension_semantics=("parallel",), vmem_limit_bytes=VMEM_LIMIT),
        name="merge",
    )(x2, proj, proj, proj, ya, yb, yc, wa, wb, wc, wo, post_w.reshape(1, d))


def kernel(x, mem, pre_norm_w, w_in, hgrn_lb_logits, hgrn_norm_w, mem_norm_w, w_mem_kv,
           w_branch_a, w_branch_b, w_branch_c, w_out, post_norm_w):
    bsz, s, d = x.shape
    assert w_in.shape[0] == 1 and w_in.shape[2] == PROJ_TOTAL and d == 1024
    assert s % MOBA_BLOCK == 0
    x2 = x.reshape(bsz * s, d)
    proj, yb, yc = _in_proj(x2, pre_norm_w[0], w_in[0].astype(BF16), hgrn_lb_logits, hgrn_norm_w[0],
                            mem, mem_norm_w[0], w_mem_kv[0].astype(BF16), s)
    slopes = jnp.exp2(-8.0 * jnp.arange(1, MOBA_HEADS + 1, dtype=F32) / MOBA_HEADS)
    ya = _moba(proj, slopes, bsz, s)
    out = _merge(x2, proj, ya, yb, yc, w_branch_a[0].astype(BF16), w_branch_b[0].astype(BF16),
                 w_branch_c[0].astype(BF16), w_out[0].astype(BF16), post_norm_w[0])
    return out.reshape(bsz, s, d)
```

```python
import functools

import jax
import jax.numpy as jnp
import numpy as np
from jax import lax
from jax.experimental import pallas as pl
from jax.experimental.pallas import tpu as pltpu

F32 = jnp.float32
BF16 = jnp.bfloat16
EPS = 1e-6

MOBA_HEADS, MOBA_HD, MOBA_BLOCK, MOBA_TOPK = 8, 64, 256, 3
HGRN_HEADS, HGRN_D, HGRN_CHUNK = 4, 128, 32
XA_HEADS, XA_HD = 4, 128
BRANCH_W = 512
LANES = 128
COL_QA, COL_KA, COL_VA, COL_ZA = 0, 512, 1024, 1536
COL_FB, COL_IB, COL_QB, COL_GB = 2048, 2560, 3072, 3584
COL_QC, COL_ZC = 4096, 4608
COL_GATE_A, COL_GATE_B, COL_GATE_C = 5120, 6144, 7168
PROJ_TOTAL = COL_GATE_C + 1024

NEG_BIG = -1e30
LOG2E = 1.4426950408889634
MOBA_VROWS = MOBA_HD + 16
MOBA_NBL = 16
MOBA_GATE_ROWS = 3 * MOBA_NBL
VMEM_LIMIT = 56 * 1024 * 1024


def _sigmoid(z):
    return 1.0 / (1.0 + jnp.exp2(z * (-LOG2E)))


def _silu(z):
    return z * _sigmoid(z)


def _split3(a):
    hi = a.astype(BF16).astype(F32)
    r = a - hi
    mid = r.astype(BF16).astype(F32)
    lo = (r - mid).astype(BF16).astype(F32)
    return hi, mid, lo


IN_PROJ_CHUNK = 1024
HGRN_TILE = 256


def _in_proj_body(x_ref, nw_ref, w_ref, lbl_ref, hnw_ref, lmat_ref, mem_ref, mw_ref, wkv_ref,
                  o_ref, yb_ref, yc_ref, st_ref, km_ref, vm_ref, *, steps_per_batch):
    @pl.when(pl.program_id(0) % steps_per_batch == 0)
    def _():
        st_ref[...] = jnp.zeros_like(st_ref)
        m = mem_ref[0]
        mn = m * lax.rsqrt(jnp.mean(m * m, axis=-1, keepdims=True) + EPS) * mw_ref[...]
        kv = jnp.dot(mn.astype(BF16), wkv_ref[...], preferred_element_type=F32)
        km_ref[...] = kv[:, :BRANCH_W].astype(BF16)
        vm_ref[...] = kv[:, BRANCH_W:].astype(BF16)

    x = x_ref[...]
    ms = jnp.mean(x * x, axis=-1, keepdims=True)
    hx = (x * lax.rsqrt(ms + EPS) * nw_ref[...]).astype(BF16)

    def chunk(c0):
        out = jnp.dot(hx, w_ref[:, c0:c0 + IN_PROJ_CHUNK], preferred_element_type=F32).astype(o_ref.dtype)
        o_ref[:, c0:c0 + IN_PROJ_CHUNK] = out
        return out

    fi = chunk(COL_FB)
    qg = chunk(COL_QB)
    f_all, i_all = fi[:, :BRANCH_W], fi[:, BRANCH_W:]
    q_all, g_all = qg[:, :BRANCH_W], qg[:, BRANCH_W:]

    r, c, d = HGRN_TILE, HGRN_CHUNK, HGRN_D
    n_chunks = r // c
    heads = range(HGRN_HEADS)
    tiles = range(x_ref.shape[0] // r)
    chunks = range(n_chunks)
    cols = [slice(h * d, (h + 1) * d) for h in heads]
    rows = [slice(t * r, (t + 1) * r) for t in tiles]
    nt_dims = (((1,), (1,)), ((), ()))

    logits = lbl_ref[...]
    ex = jnp.exp(logits - jnp.max(logits, axis=0, keepdims=True))
    lb = ex[0:1, :] / jnp.sum(ex, axis=0, keepdims=True)

    kk, pieces = [], []
    for t in tiles:
        fl = f_all[rows[t], :].astype(F32)
        log_f = jnp.log(lb + (1.0 - lb) * _sigmoid(fl))
        kk.append((1.0 - lb) * _sigmoid(-fl))
        pieces.append([p.astype(BF16) for p in _split3(log_f)])
    qz = chunk(COL_QC)
    xcols = [slice(h * XA_HD, (h + 1) * XA_HD) for h in range(XA_HEADS)]

    def xa_logits(hs):
        return [lax.dot_general(qz[:, xcols[h]], km_ref[:, xcols[h]], nt_dims,
                                preferred_element_type=F32) * (XA_HD ** -0.5 * LOG2E) for h in hs]

    def xa_softmax(logits):
        ps, dens = [], []
        for lg in logits:
            p = jnp.exp2(lg - jnp.max(lg, axis=-1, keepdims=True))
            dens.append(jnp.sum(p, axis=-1, keepdims=True))
            ps.append(p.astype(BF16))
        return ps, dens

    def xa_values(ps, dens, hs):
        return [jnp.dot(p, vm_ref[:, xcols[h]], preferred_element_type=F32) / den
                for p, den, h in zip(ps, dens, hs)]

    b = []
    for t in tiles:
        lmat = lmat_ref[...]
        hi, mid, lo = [jnp.dot(lmat, p, preferred_element_type=F32) for p in pieces[t]]
        b.append(hi + mid + lo)
    chunk(COL_QA)

    q_t, k_t32, k_t, dec = [], [], [], []
    for t in tiles:
        q_t.append((q_all[rows[t], :].astype(F32) * jnp.exp(b[t])).astype(BF16))
        kt = kk[t] * jnp.exp(-b[t])
        k_t32.append(kt)
        k_t.append(kt.astype(BF16))
        dec.append([jnp.exp(b[t][ci * c + c - 1:ci * c + c, :]) for ci in chunks])

    a = [[lax.dot_general(q_t[t][:, cols[h]], k_t[t][:, cols[h]], nt_dims, preferred_element_type=F32)
          for h in heads] for t in tiles]
    u_t = []
    for t in tiles:
        per_head = []
        for h in heads:
            vt = i_all[rows[t], cols[h]].astype(F32).T.astype(BF16)
            blocks = []
            for ci in chunks:
                slab = (k_t32[t][ci * c:(ci + 1) * c, cols[h]] * dec[t][ci][:, cols[h]]).astype(BF16)
                parts = ([jnp.zeros((c, ci * d), BF16)] if ci else []) + [slab]
                if ci < n_chunks - 1:
                    parts.append(jnp.zeros((c, (n_chunks - 1 - ci) * d), BF16))
                blocks.append(jnp.concatenate(parts, axis=1))
            kd_blk = jnp.concatenate(blocks, axis=0)
            per_head.append(jnp.dot(vt, kd_blk, preferred_element_type=F32))
        u_t.append(per_head)
    xa_lg = xa_logits((0, 1))
    chunk(COL_VA)
    o_intra = [[jnp.dot(a[t][h].astype(BF16) * lmat_ref[...], i_all[rows[t], cols[h]],
                        preferred_element_type=F32) for h in heads] for t in tiles]
    xa_o = xa_values(*xa_softmax(xa_lg), (0, 1))
    xa_lg = xa_logits((2, 3))
    chunk(COL_GATE_A)

    states = [[None] * HGRN_HEADS for _ in tiles]
    for h in heads:
        st = st_ref[h]
        for t in tiles:
            before = []
            for ci in chunks:
                before.append(st.astype(BF16))
                st = st * dec[t][ci][:, cols[h]] + u_t[t][h][:, ci * d:(ci + 1) * d]
            states[t][h] = before
        st_ref[h] = st
    xa_o += xa_values(*xa_softmax(xa_lg), (2, 3))
    chunk(COL_GATE_B)

    odd = (lax.broadcasted_iota(jnp.int32, (2 * c, d), 0) >= c)
    o_inter = []
    for t in tiles:
        per_head = []
        for h in heads:
            outs = []
            for ci in range(0, n_chunks, 2):
                qp = q_t[t][ci * c:(ci + 2) * c, cols[h]]
                zero = jnp.zeros_like(qp)
                lhs = jnp.concatenate([jnp.where(odd, zero, qp), jnp.where(odd, qp, zero)], axis=1)
                st2 = jnp.concatenate([states[t][h][ci], states[t][h][ci + 1]], axis=1)
                outs.append(lax.dot_general(lhs, st2, nt_dims, preferred_element_type=F32))
            per_head.append(outs)
        o_inter.append(per_head)
    for h in range(XA_HEADS):
        zc = qz[:, BRANCH_W + h * XA_HD:BRANCH_W + (h + 1) * XA_HD].astype(F32)
        yc_ref[:, xcols[h]] = (xa_o[h] * _silu(zc)).astype(yc_ref.dtype)
    chunk(COL_GATE_C)
    for t in tiles:
        for h in heads:
            o = o_intra[t][h] + jnp.concatenate(o_inter[t][h], axis=0)
            on = o * lax.rsqrt(jnp.mean(o * o, axis=-1, keepdims=True) + EPS) * hnw_ref[...]
            yb_ref[rows[t], cols[h]] = (on * _silu(g_all[rows[t], cols[h]].astype(F32))).astype(yb_ref.dtype)


def _in_proj(x2, norm_w, w_bf16, lb_logits, hgrn_norm_w, mem, mem_norm_w, wkv_bf16, s, tm=512):
    t, d = x2.shape
    n = w_bf16.shape[1]
    n_mem = mem.shape[1]
    spb = s // tm
    assert n == 8 * IN_PROJ_CHUNK and s % tm == 0 and tm % HGRN_TILE == 0
    r, c = HGRN_TILE, HGRN_CHUNK
    ri = np.arange(r)
    same = (ri[:, None] // c) == (ri[None, :] // c)
    tril = (same & (ri[None, :] <= ri[:, None])).astype(np.float32)
    const = lambda shape: pl.BlockSpec(shape, lambda i: (0,) * len(shape))
    return pl.pallas_call(
        functools.partial(_in_proj_body, steps_per_batch=spb),
        grid=(t // tm,),
        in_specs=[
            pl.BlockSpec((tm, d), lambda i: (i, 0)),
            const((1, d)),
            pl.BlockSpec((d, n), lambda i: (0, 0), pipeline_mode=pl.Buffered(1)),
            const((lb_logits.shape[0], BRANCH_W)),
            const((1, HGRN_D)),
            const((r, r)),
            pl.BlockSpec((1, n_mem, d), lambda i: (i // spb, 0, 0)),
            const((1, d)),
            pl.BlockSpec((d, 2 * BRANCH_W), lambda i: (0, 0), pipeline_mode=pl.Buffered(1)),
        ],
        out_specs=[pl.BlockSpec((tm, n), lambda i: (i, 0)), pl.BlockSpec((tm, BRANCH_W), lambda i: (i, 0)),
                   pl.BlockSpec((tm, BRANCH_W), lambda i: (i, 0))],
        out_shape=[jax.ShapeDtypeStruct((t, n), BF16), jax.ShapeDtypeStruct((t, BRANCH_W), BF16),
                   jax.ShapeDtypeStruct((t, BRANCH_W), BF16)],
        scratch_shapes=[pltpu.VMEM((HGRN_HEADS, HGRN_D, HGRN_D), F32),
                        pltpu.VMEM((n_mem, BRANCH_W), BF16), pltpu.VMEM((n_mem, BRANCH_W), BF16)],
        compiler_params=pltpu.CompilerParams(
            dimension_semantics=("arbitrary",), vmem_limit_bytes=VMEM_LIMIT),
        name="in_proj",
    )(x2, norm_w.reshape(1, d), w_bf16, lb_logits, hgrn_norm_w.reshape(1, HGRN_D), tril.astype(BF16),
      mem, mem_norm_w.reshape(1, d), wkv_bf16)


def _moba_constants(s):
    blk, hd, nbl = MOBA_BLOCK, MOBA_HD, MOBA_NBL
    row = np.arange(s)
    off = (row % blk).astype(np.float32)
    onehot = (row[:, None] // blk == np.arange(nbl)[None, :]).astype(np.float32)
    bias_lanes = np.concatenate([onehot, onehot, onehot, off[:, None], off[:, None], off[:, None],
                                 np.zeros((s, hd - 3 * nbl - 3), np.float32)], axis=1)
    zeros = np.zeros((s, hd), np.float32)
    a0 = np.concatenate([zeros, bias_lanes], axis=1)
    a1 = np.concatenate([bias_lanes, zeros], axis=1)
    aug = np.stack([a0, a1]).astype(BF16)
    avg = ((np.arange(nbl)[:, None] == (row // blk)[None, :]).astype(np.float32) / blk).astype(BF16)
    causal = np.where(np.arange(blk)[:, None] <= np.arange(blk)[None, :], 0.0, -np.inf).astype(np.float32)
    mask = np.concatenate([np.zeros((blk, blk), np.float32), causal])
    return aug, avg, mask


def _moba_body(slopes_ref, q_ref, qn_ref, k_ref, v_ref, z_ref, aug_ref, avg_ref, mask_ref, o_ref,
               ka_ref, vt_ref, kbp_ref, qa_ref, qan_ref, acc_ref, p_ref, m_ref, al_ref):
    s = k_ref.shape[0]
    nb = s // MOBA_BLOCK
    blk = MOBA_BLOCK
    nh = MOBA_HEADS
    hd = MOBA_HD
    nbl, gr = MOBA_NBL, MOBA_GATE_ROWS
    i = pl.program_id(1)

    def build_queries(src_ref, iq, dst_ref):
        nidx = lax.broadcasted_iota(jnp.int32, (nbl, blk), 0)
        nidx_f = nidx.astype(F32)
        qoff = lax.broadcasted_iota(jnp.int32, (nbl, blk), 1)
        dist0 = (qoff + (iq - nidx) * blk).astype(F32)
        qts = [(src_ref[:, hp * LANES:(hp + 1) * LANES].astype(F32) * (hd ** -0.5)).T
               for hp in range(nh // 2)]
        gates = [jnp.dot(kbp_ref[hp * 2 * gr:(hp + 1) * 2 * gr, :], qts[hp].astype(BF16),
                         preferred_element_type=F32) for hp in range(nh // 2)]
        for hp in range(nh // 2):
            qs = qts[hp] * LOG2E
            for e in range(2):
                h = 2 * hp + e
                gg = gates[hp][e * gr:(e + 1) * gr]
                g = gg[0:nbl] + gg[nbl:2 * nbl] + gg[2 * nbl:3 * nbl]
                gv = jnp.where(nidx < iq, g, -jnp.inf)
                sel = nidx == iq
                for _ in range(MOBA_TOPK):
                    mx = jnp.max(gv, axis=0, keepdims=True)
                    first = jnp.min(jnp.where(gv == mx, nidx_f, float(nbl)), axis=0, keepdims=True)
                    pick = nidx_f == jnp.where(mx > -jnp.inf, first, float(nbl))
                    sel = sel | pick
                    gv = jnp.where(pick, -jnp.inf, gv)
                sl = slopes_ref[h] * LOG2E
                bias = jnp.where(sel, -sl * dist0, NEG_BIG)
                b_hi, b_mid, b_lo = _split3(bias)
                s_hi, s_mid, s_lo = _split3(jnp.full((nbl, blk), sl, F32))
                srow = jnp.where(nidx == 0, s_hi, jnp.where(nidx == 1, s_mid, jnp.where(nidx == 2, s_lo, 0.0)))
                qh = qs[0:hd] if e == 0 else qs[hd:2 * hd]
                parts = [qh, b_hi, b_mid, b_lo, srow] if e == 0 else [b_hi, b_mid, b_lo, srow, qh]
                dst_ref[h] = jnp.concatenate(parts, axis=0).astype(BF16)

    @pl.when(i == 0)
    def _():
        lane = lax.broadcasted_iota(jnp.int32, (1, LANES), 1)
        first_half = lane < hd
        kbar = jnp.dot(avg_ref[...], k_ref[...], preferred_element_type=F32)
        for h in range(nh):
            hp, e = h // 2, h % 2
            cols = slice(hp * LANES, (hp + 1) * LANES)
            keep = first_half if e == 0 else jnp.logical_not(first_half)
            for j in range(nb):
                rows = slice(j * blk, (j + 1) * blk)
                ka_ref[h, j] = aug_ref[e, rows, :]
                ka_ref[h, j, :, e * hd:(e + 1) * hd] = k_ref[rows, h * hd:(h + 1) * hd]
            pieces = _split3(jnp.where(keep, kbar[:, cols], 0.0))
            for p in range(3):
                kbp_ref[h * gr + p * nbl:h * gr + (p + 1) * nbl, :] = pieces[p].astype(BF16)
        ones_row = jnp.where(lax.broadcasted_iota(jnp.int32, (MOBA_VROWS - hd, blk), 0) == 0, 1.0, 0.0)
        for j in range(nb):
            vt = v_ref[j * blk:(j + 1) * blk, :].astype(F32).T
            for h in range(nh):
                vt_ref[j, h, 0:hd, :] = vt[h * hd:(h + 1) * hd].astype(BF16)
                vt_ref[j, h, hd:MOBA_VROWS, :] = ones_row.astype(BF16)
        build_queries(q_ref, 0, qa_ref)
        p_ref[...] = jnp.zeros(p_ref.shape, BF16)
        al_ref[...] = jnp.ones(al_ref.shape, F32)

    @pl.when(i > 0)
    def _():
        qa_ref[...] = qan_ref[...]

    m_ref[...] = jnp.full(m_ref.shape, -jnp.inf, F32)
    acc_ref[...] = jnp.zeros(acc_ref.shape, F32)

    def softmax(h, sT):
        m_old = m_ref[h]
        m_new = jnp.maximum(m_old, jnp.max(sT, axis=0, keepdims=True))
        m_ref[h] = m_new
        return jnp.exp2(sT - m_new).astype(BF16), jnp.exp2(m_old - m_new)

    def pair_scores(j0):
        return [jnp.dot(ka_ref[h, pl.ds(j0, 2)].reshape(2 * blk, LANES), qa_ref[h],
                        preferred_element_type=F32) for h in range(nh)]

    def pair_values(ja, jb, h):
        return jnp.concatenate([vt_ref[ja, h], vt_ref[jb, h]], axis=1)

    def pending_pv(j0):
        pa, pb = jnp.maximum(j0 - 2, 0), jnp.maximum(j0 - 1, 0)
        return [jnp.dot(pair_values(pa, pb, h), p_ref[h], preferred_element_type=F32) for h in range(nh)]

    def past(k, carry):
        scores = pair_scores(2 * k)
        pv = pending_pv(2 * k)
        sm = [softmax(h, scores[h]) for h in range(nh)]
        for h in range(nh):
            acc_ref[h] = al_ref[h] * acc_ref[h] + pv[h]
        for h in range(nh):
            p_ref[h], al_ref[h] = sm[h]
        return carry

    npairs = i // 2
    lax.fori_loop(0, npairs, past, 0)

    def finish(scores, last_values):
        pv = pending_pv(2 * npairs)
        build_queries(qn_ref, i + 1, qan_ref)
        sm = [softmax(h, scores[h]) for h in range(nh)]
        last = [jnp.dot(last_values(h), sm[h][0], preferred_element_type=F32) for h in range(nh)]
        outs = []
        for h in range(nh):
            acc = sm[h][1] * (al_ref[h] * acc_ref[h] + pv[h]) + last[h]
            outs.append(acc[0:hd] / acc[hd:hd + 1])
        ot = jnp.concatenate(outs, axis=0)
        o_ref[...] = (ot.T * _silu(z_ref[...].astype(F32))).astype(o_ref.dtype)

    @pl.when(i % 2 == 0)
    def _():
        finish([jnp.dot(ka_ref[h, i], qa_ref[h], preferred_element_type=F32) + mask_ref[blk:2 * blk, :]
                for h in range(nh)], lambda h: vt_ref[i, h])

    @pl.when(i % 2 == 1)
    def _():
        scores = pair_scores(i - 1)
        finish([scores[h] + mask_ref[...] for h in range(nh)], lambda h: pair_values(i - 1, i, h))


def _moba(proj, slopes, bsz, s):
    nb = s // MOBA_BLOCK
    assert nb <= MOBA_NBL and 2 * MOBA_HD == LANES and 3 * MOBA_NBL + 3 <= MOBA_HD
    blk = MOBA_BLOCK
    aug, avg, mask = _moba_constants(s)
    qblock = lambda c0: (lambda b, i, *_: (b * nb + i, c0 // BRANCH_W))
    qnext = lambda b, i, *_: (b * nb + jnp.minimum(i + 1, nb - 1), COL_QA // BRANCH_W)
    whole = lambda c0: (lambda b, i, *_: (b, c0 // BRANCH_W))
    return pl.pallas_call(
        _moba_body,
        grid_spec=pltpu.PrefetchScalarGridSpec(
            num_scalar_prefetch=1,
            grid=(bsz, nb),
            in_specs=[
                pl.BlockSpec((blk, BRANCH_W), qblock(COL_QA)),
                pl.BlockSpec((blk, BRANCH_W), qnext),
                pl.BlockSpec((s, BRANCH_W), whole(COL_KA)),
                pl.BlockSpec((s, BRANCH_W), whole(COL_VA)),
                pl.BlockSpec((blk, BRANCH_W), qblock(COL_ZA)),
                pl.BlockSpec((2, s, LANES), lambda b, i, *_: (0, 0, 0)),
                pl.BlockSpec((MOBA_NBL, s), lambda b, i, *_: (0, 0)),
                pl.BlockSpec((2 * blk, blk), lambda b, i, *_: (0, 0)),
            ],
            out_specs=pl.BlockSpec((blk, BRANCH_W), lambda b, i, *_: (b * nb + i, 0)),
            scratch_shapes=[
                pltpu.VMEM((MOBA_HEADS, nb, blk, LANES), BF16),
                pltpu.VMEM((nb, MOBA_HEADS, MOBA_VROWS, blk), BF16),
                pltpu.VMEM((MOBA_HEADS * MOBA_GATE_ROWS, LANES), BF16),
                pltpu.VMEM((MOBA_HEADS, LANES, blk), BF16),
                pltpu.VMEM((MOBA_HEADS, LANES, blk), BF16),
                pltpu.VMEM((MOBA_HEADS, MOBA_VROWS, blk), F32),
                pltpu.VMEM((MOBA_HEADS, 2 * blk, blk), BF16),
                pltpu.VMEM((MOBA_HEADS, 1, blk), F32),
                pltpu.VMEM((MOBA_HEADS, 1, blk), F32),
            ],
        ),
        out_shape=jax.ShapeDtypeStruct((bsz * s, BRANCH_W), BF16),
        compiler_params=pltpu.CompilerParams(
            dimension_semantics=("parallel", "arbitrary"), vmem_limit_bytes=VMEM_LIMIT),
        name="moba",
    )(slopes, proj, proj, proj, proj, proj, aug, avg, mask)


def _merge_body(x_ref, ga_ref, gb_ref, gc_ref, ya_ref, yb_ref, yc_ref,
                wa_ref, wb_ref, wc_ref, wo_ref, pw_ref, o_ref):
    tm = x_ref.shape[0]
    halves = [slice(0, tm // 2), slice(tm // 2, tm)]
    branches = ((ga_ref, ya_ref, wa_ref), (gb_ref, yb_ref, wb_ref), (gc_ref, yc_ref, wc_ref))
    proj = [[jnp.dot(y_ref[rows, :], w_ref[...], preferred_element_type=F32) for _, y_ref, w_ref in branches]
            for rows in halves]
    gated = [[_sigmoid(g_ref[rows, :].astype(F32)) * proj[k][n] for n, (g_ref, _, _) in enumerate(branches)]
             for k, rows in enumerate(halves)]
    merged = [gated[k][0] + gated[k][1] + gated[k][2] for k in range(2)]
    y = [jnp.dot(merged[k].astype(BF16), wo_ref[...], preferred_element_type=F32) for k in range(2)]
    for k, rows in enumerate(halves):
        yn = y[k] * lax.rsqrt(jnp.mean(y[k] * y[k], axis=-1, keepdims=True) + EPS) * pw_ref[...]
        o_ref[rows, :] = x_ref[rows, :] + yn


def _merge(x2, proj, ya, yb, yc, wa, wb, wc, wo, post_w, tm=1024):
    t, d = x2.shape
    row = lambda c: (lambda i: (i, c))
    const = lambda shape: pl.BlockSpec(shape, lambda i: (0, 0))
    return pl.pallas_call(
        _merge_body,
        grid=(t // tm,),
        in_specs=[
            pl.BlockSpec((tm, d), row(0)),
            pl.BlockSpec((tm, d), row(COL_GATE_A // d)),
            pl.BlockSpec((tm, d), row(COL_GATE_B // d)),
            pl.BlockSpec((tm, d), row(COL_GATE_C // d)),
            pl.BlockSpec((tm, BRANCH_W), row(0)),
            pl.BlockSpec((tm, BRANCH_W), row(0)),
            pl.BlockSpec((tm, BRANCH_W), row(0)),
            const((BRANCH_W, d)), const((BRANCH_W, d)), const((BRANCH_W, d)),
            const((d, d)), const((1, d)),
        ],
        out_specs=pl.BlockSpec((tm, d), row(0)),
        out_shape=jax.ShapeDtypeStruct((t, d), F32),
        compiler_params=pltpu.CompilerParams(
            dimension_semantics=("parallel",), vmem_limit_bytes=VMEM_LIMIT),
        name="merge",
    )(x2, proj, proj, proj, ya, yb, yc, wa, wb, wc, wo, post_w.reshape(1, d))


def kernel(x, mem, pre_norm_w, w_in, hgrn_lb_logits, hgrn_norm_w, mem_norm_w, w_mem_kv,
           w_branch_a, w_branch_b, w_branch_c, w_out, post_norm_w):
    bsz, s, d = x.shape
    assert w_in.shape[0] == 1 and w_in.shape[2] == PROJ_TOTAL and d == 1024
    assert s % MOBA_BLOCK == 0
    x2 = x.reshape(bsz * s, d)
    proj, yb, yc = _in_proj(x2, pre_norm_w[0], w_in[0].astype(BF16), hgrn_lb_logits, hgrn_norm_w[0],
                            mem, mem_norm_w[0], w_mem_kv[0].astype(BF16), s)
    slopes = jnp.exp2(-8.0 * jnp.arange(1, MOBA_HEADS + 1, dtype=F32) / MOBA_HEADS)
    ya = _moba(proj, slopes, bsz, s)
    out = _merge(x2, proj, ya, yb, yc, w_branch_a[0].astype(BF16), w_branch_b[0].astype(BF16),
                 w_branch_c[0].astype(BF16), w_out[0].astype(BF16), post_norm_w[0])
    return out.reshape(bsz, s, d)
```
